```python
import math
import jax, jax.numpy as jnp
from jax import lax
import numpy as np

D_MODEL = 1024
BATCH = 4
SEQ = 4096
DEPTH = 2

N_MIXERS = 4
HEAD_DIM = 64
GROUP_WIDTH = D_MODEL // N_MIXERS
GROUP_HEADS = GROUP_WIDTH // HEAD_DIM
EPS = 1e-6
ATTN_PATTERNS = ((128, 1), (512, 4), (2048, 16))
ROPE_THETA = 10000.0
SSM_STATE = 128
SSM_GROUPS = 2
SSM_CONV = 4
SSM_CHUNK = 128
SSM_XBC = GROUP_WIDTH + 2 * SSM_GROUPS * SSM_STATE
RET_CHUNK = 128
CONF_KERNEL = 31
D_FF = 4 * D_MODEL
PLE_DIM = 256
A_COLS = 3 * GROUP_WIDTH
B_COLS = GROUP_WIDTH + SSM_XBC + GROUP_HEADS
C_COLS = 4 * GROUP_WIDTH
D_COLS = 2 * GROUP_WIDTH
IN_COLS = A_COLS + B_COLS + C_COLS + D_COLS

kernel_name = 'hybrid_parallel_heads_dilated_ssd_retention_conformer'


def _rmsnorm(x, g):
    xf = x.astype(jnp.float32)
    y = xf * lax.rsqrt(jnp.mean(xf * xf, axis=-1, keepdims=True) + EPS) * g.astype(jnp.float32)
    return y.astype(x.dtype)


def _rms_f32(x, g=None):
    y = x * lax.rsqrt(jnp.mean(x * x, axis=-1, keepdims=True) + EPS)
    return y if g is None else y * g.astype(jnp.float32)


def _layernorm_f32(x, g, b):
    mu = jnp.mean(x, axis=-1, keepdims=True)
    xc = x - mu
    var = jnp.mean(xc * xc, axis=-1, keepdims=True)
    return xc * lax.rsqrt(var + EPS) * g.astype(jnp.float32) + b.astype(jnp.float32)


def _rotary(x, angles):
    seq = x.shape[1]
    half = x.shape[-1] // 2
    pos = jnp.arange(seq, dtype=jnp.float32)
    ang = pos[:, None] * angles[None, :]
    cos = jnp.cos(ang)[None, :, None, :]
    sin = jnp.sin(ang)[None, :, None, :]
    x1, x2 = x[..., :half], x[..., half:]
    return jnp.concatenate([x1 * cos - x2 * sin, x2 * cos + x1 * sin], axis=-1)


def _causal_dwconv(x, w, b):
    k = w.shape[0]
    y = lax.conv_general_dilated(
        x, w[:, None, :].astype(x.dtype), window_strides=(1,), padding=((k - 1, 0),),
        dimension_numbers=('NWC', 'WIO', 'NWC'), feature_group_count=x.shape[-1])
    return y + b.astype(y.dtype)


def _dilated_pattern(q, k, v, window, dil):
    bsz, seq, h, d = q.shape
    blk = window // dil
    length = seq // dil
    nb = -(-length // blk)
    lp = nb * blk

    def to_blocks(t):
        t = t.reshape(bsz, length, dil, h, d).transpose(0, 2, 1, 3, 4)
        t = jnp.pad(t, ((0, 0), (0, 0), (0, lp - length), (0, 0), (0, 0)))
        return t.reshape(bsz, dil, nb, blk, h, d)

    def with_prev(t):
        prev = jnp.pad(t, ((0, 0), (0, 0), (1, 0), (0, 0), (0, 0), (0, 0)))[:, :, :-1]
        return jnp.concatenate([prev, t], axis=3)

    qb = to_blocks(q)
    kw = with_prev(to_blocks(k))
    vw = with_prev(to_blocks(v))
    s = jnp.einsum('brnqhd,brnkhd->brnhqk', qb, kw)
    qi = jnp.arange(blk)[:, None]
    ki = jnp.arange(2 * blk)[None, :]
    band = (ki >= qi) & (ki <= qi + blk)
    has_prev = (jnp.arange(nb)[:, None, None] > 0) | (ki[None] >= blk)
    mask = band[None] & has_prev
    s = jnp.where(mask[:, None], s, -jnp.inf)
    m = jnp.max(s, axis=-1, keepdims=True)
    e = jnp.exp(s - m)
    den = jnp.sum(e, axis=-1, keepdims=True)
    o = jnp.einsum('brnhqk,brnkhd->brnqhd', e, vw)
    o = o / jnp.swapaxes(den, 3, 4)
    lse = jnp.swapaxes((m + jnp.log(den))[..., 0], 3, 4)

    def from_blocks(t):
        t = t.reshape((bsz, dil, lp) + t.shape[4:])[:, :, :length]
        t = jnp.swapaxes(t, 1, 2)
        return t.reshape((bsz, seq) + t.shape[3:])

    return from_blocks(o), from_blocks(lse)


def _attention_mixer(qkv, q_gain, k_gain):
    bsz, seq, _ = qkv.shape
    q, k, v = jnp.split(qkv.astype(jnp.float32), 3, axis=-1)
    q = q.reshape(bsz, seq, GROUP_HEADS, HEAD_DIM)
    k = k.reshape(bsz, seq, GROUP_HEADS, HEAD_DIM)
    v = v.reshape(bsz, seq, GROUP_HEADS, HEAD_DIM)
    angles = ROPE_THETA ** (-jnp.arange(0, HEAD_DIM, 2, dtype=jnp.float32) / HEAD_DIM)
    q = _rotary(_rms_f32(q, q_gain), angles) * (HEAD_DIM ** -0.5)
    k = _rotary(_rms_f32(k, k_gain), angles)
    outs, lses = [], []
    for window, dil in ATTN_PATTERNS:
        o, lse = _dilated_pattern(q, k, v, window, dil)
        outs.append(o)
        lses.append(lse)
    w = jax.nn.softmax(jnp.stack(lses, axis=0), axis=0)
    o = jnp.sum(w[..., None] * jnp.stack(outs, axis=0), axis=0)
    return o.reshape(bsz, seq, GROUP_WIDTH)


def _ssd_scan(x, dt, a, bm, cm):
    bsz, seq, h, pdim = x.shape
    n = bm.shape[-1]
    q = SSM_CHUNK
    nc = seq // q
    xc = x.reshape(bsz, nc, q, h, pdim)
    dtc = dt.reshape(bsz, nc, q, h)
    bc = bm.reshape(bsz, nc, q, h, n)
    cc = cm.reshape(bsz, nc, q, h, n)
    a_cum = jnp.cumsum(dtc * a, axis=2)
    causal = jnp.tril(jnp.ones((q, q), dtype=bool))
    diff = a_cum[:, :, :, None, :] - a_cum[:, :, None, :, :]
    lmat = jnp.exp(jnp.where(causal[None, None, :, :, None], diff, -jnp.inf))
    cb = jnp.einsum('bcihn,bcjhn->bcijh', cc, bc)
    y_diag = jnp.einsum('bcijh,bcjh,bcjhp->bcihp', cb * lmat, dtc, xc)
    decay_states = jnp.exp(a_cum[:, :, -1:, :] - a_cum)
    states = jnp.einsum('bcjhn,bcjh,bcjhp->bchpn', bc, decay_states * dtc, xc)
    chunk_decay = jnp.exp(a_cum[:, :, -1, :])

    def step(hs, inp):
        st, dec = inp
        return dec[:, :, None, None] * hs + st, hs

    h0 = jnp.zeros((bsz, h, pdim, n), dtype=jnp.float32)
    _, prev = lax.scan(step, h0, (jnp.moveaxis(states, 1, 0), jnp.moveaxis(chunk_decay, 1, 0)))
    prev = jnp.moveaxis(prev, 0, 1)
    y_off = jnp.einsum('bcihn,bchpn,bcih->bcihp', cc, prev, jnp.exp(a_cum))
    return (y_diag + y_off).reshape(bsz, seq, h, pdim)


def _ssd_mixer(u, conv_w, conv_b, dt_bias, a_log, d_skip, norm_g):
    bsz, seq, _ = u.shape
    z, xbc, dt = jnp.split(u, [GROUP_WIDTH, GROUP_WIDTH + SSM_XBC], axis=-1)
    xbc = jax.nn.silu(_causal_dwconv(xbc, conv_w, conv_b).astype(jnp.float32))
    xs, bm, cm = jnp.split(xbc, [GROUP_WIDTH, GROUP_WIDTH + SSM_GROUPS * SSM_STATE], axis=-1)
    xs = xs.reshape(bsz, seq, GROUP_HEADS, HEAD_DIM)
    rep = GROUP_HEADS // SSM_GROUPS
    bm = jnp.repeat(bm.reshape(bsz, seq, SSM_GROUPS, SSM_STATE), rep, axis=2)
    cm = jnp.repeat(cm.reshape(bsz, seq, SSM_GROUPS, SSM_STATE), rep, axis=2)
    dt = jax.nn.softplus(dt.astype(jnp.float32) + dt_bias.astype(jnp.float32))
    a = -jnp.exp(a_log.astype(jnp.float32))
    y = _ssd_scan(xs, dt, a, bm, cm) + d_skip.astype(jnp.float32)[:, None] * xs
    y = y.reshape(bsz, seq, GROUP_WIDTH) * jax.nn.silu(z.astype(jnp.float32))
    y = _rms_f32(y.reshape(bsz, seq, SSM_GROUPS, GROUP_WIDTH // SSM_GROUPS))
    return y.reshape(bsz, seq, GROUP_WIDTH) * norm_g.astype(jnp.float32)


def _retention_mixer(u):
    bsz, seq, _ = u.shape
    q, k, v, g = jnp.split(u.astype(jnp.float32), 4, axis=-1)
    q = q.reshape(bsz, seq, GROUP_HEADS, HEAD_DIM)
    k = k.reshape(bsz, seq, GROUP_HEADS, HEAD_DIM)
    v = v.reshape(bsz, seq, GROUP_HEADS, HEAD_DIM)
    angles = 1.0 / (10000.0 ** jnp.linspace(0.0, 1.0, HEAD_DIM // 2, dtype=jnp.float32))
    q = _rotary(q, angles)
    k = _rotary(k, angles) * (HEAD_DIM ** -0.5)
    log_g = jnp.log(1.0 - 2.0 ** (-5.0 - jnp.arange(GROUP_HEADS, dtype=jnp.float32)))
    cq = RET_CHUNK
    nc = seq // cq
    qc = q.reshape(bsz, nc, cq, GROUP_HEADS, HEAD_DIM)
    kc = k.reshape(bsz, nc, cq, GROUP_HEADS, HEAD_DIM)
    vc = v.reshape(bsz, nc, cq, GROUP_HEADS, HEAD_DIM)
    idx = jnp.arange(cq, dtype=jnp.float32)
    dist = idx[:, None] - idx[None, :]
    dmat = jnp.where((dist >= 0)[..., None], jnp.exp(jnp.maximum(dist, 0.0)[..., None] * log_g), 0.0)
    inner = jnp.einsum('bcihd,bcjhd->bcijh', qc, kc) * dmat
    y_in = jnp.einsum('bcijh,bcjhe->bcihe', inner, vc)
    zeta = jnp.exp((cq - 1 - idx)[:, None] * log_g)
    states = jnp.einsum('bcjhd,jh,bcjhe->bchde', kc, zeta, vc)
    g_chunk = jnp.exp(cq * log_g)

    def step(r, st):
        return g_chunk[:, None, None] * r + st, r

    r0 = jnp.zeros((bsz, GROUP_HEADS, HEAD_DIM, HEAD_DIM), dtype=jnp.float32)
    _, prev = lax.scan(step, r0, jnp.moveaxis(states, 1, 0))
    prev = jnp.moveaxis(prev, 0, 1)
    xi = jnp.exp((idx + 1.0)[:, None] * log_g)
    y_cross = jnp.einsum('bcihd,bchde,ih->bcihe', qc, prev, xi)
    y = _rms_f32((y_in + y_cross).reshape(bsz, seq, GROUP_HEADS, HEAD_DIM))
    return y.reshape(bsz, seq, GROUP_WIDTH) * jax.nn.silu(g)


def _conv_module(u, dw_w, dw_b, ln_g, ln_b):
    val, gate = jnp.split(u, 2, axis=-1)
    hglu = val * jax.nn.sigmoid(gate)
    hc = _causal_dwconv(hglu, dw_w, dw_b).astype(jnp.float32)
    return jax.nn.silu(_layernorm_f32(hc, ln_g, ln_b))


def setup_inputs(seed: int = 0) -> dict:
    key = jax.random.key(seed)
    ks = jax.random.split(key, 24)
    f32 = jnp.float32

    def nrm(k, shape, scale):
        return jax.random.normal(k, shape, f32) * scale

    def gain(k, shape):
        return 1.0 + 0.05 * jax.random.normal(k, shape, f32)

    dt = jnp.exp(jax.random.uniform(ks[8], (DEPTH, GROUP_HEADS), f32,
                                    math.log(0.001), math.log(0.1)))
    return {
        'x': nrm(ks[0], (BATCH, SEQ, D_MODEL), 1.0),
        'p': nrm(ks[1], (DEPTH, BATCH, SEQ, PLE_DIM), 1.0),
        'norm_mix': gain(ks[2], (DEPTH, D_MODEL)),
        'w_in': nrm(ks[3], (DEPTH, D_MODEL, IN_COLS), D_MODEL ** -0.5),
        'attn_q_norm': gain(ks[4], (DEPTH, HEAD_DIM)),
        'attn_k_norm': gain(ks[5], (DEPTH, HEAD_DIM)),
        'ssm_conv_w': nrm(ks[6], (DEPTH, SSM_CONV, SSM_XBC), SSM_CONV ** -0.5),
        'ssm_conv_b': nrm(ks[7], (DEPTH, SSM_XBC), 0.02),
        'ssm_dt_bias': dt + jnp.log(-jnp.expm1(-dt)),
        'ssm_a_log': jnp.log(jax.random.uniform(ks[9], (DEPTH, GROUP_HEADS), f32, 1.0, 16.0)),
        'ssm_d': gain(ks[10], (DEPTH, GROUP_HEADS)),
        'ssm_norm': gain(ks[11], (DEPTH, GROUP_WIDTH)),
        'conv_dw_w': nrm(ks[12], (DEPTH, CONF_KERNEL, GROUP_WIDTH), CONF_KERNEL ** -0.5),
        'conv_dw_b': nrm(ks[13], (DEPTH, GROUP_WIDTH), 0.02),
        'conv_ln_g': gain(ks[14], (DEPTH, GROUP_WIDTH)),
        'conv_ln_b': nrm(ks[15], (DEPTH, GROUP_WIDTH), 0.02),
        'w_out': nrm(ks[16], (DEPTH, D_MODEL, D_MODEL), D_MODEL ** -0.5),
        'norm_ffn': gain(ks[17], (DEPTH, D_MODEL)),
        'w_up': nrm(ks[18], (DEPTH, D_MODEL, D_FF), D_MODEL ** -0.5),
        'w_down': nrm(ks[19], (DEPTH, D_FF, D_MODEL), D_FF ** -0.5),
        'norm_ple': gain(ks[20], (DEPTH, D_MODEL)),
        'w_ple': nrm(ks[21], (DEPTH, PLE_DIM, D_MODEL), PLE_DIM ** -0.5),
        'w_ple_gate': nrm(ks[22], (DEPTH, D_MODEL, D_MODEL), D_MODEL ** -0.5),
    }


def reference(x, p, norm_mix, w_in, attn_q_norm, attn_k_norm, ssm_conv_w, ssm_conv_b,
              ssm_dt_bias, ssm_a_log, ssm_d, ssm_norm, conv_dw_w, conv_dw_b, conv_ln_g,
              conv_ln_b, w_out, norm_ffn, w_up, w_down, norm_ple, w_ple, w_ple_gate):
    for i in range(DEPTH):
        h = _rmsnorm(x, norm_mix[i])
        proj = jnp.einsum('bsd,dc->bsc', h, w_in[i])
        a_in, b_in, c_in, d_in = jnp.split(
            proj, [A_COLS, A_COLS + B_COLS, A_COLS + B_COLS + C_COLS], axis=-1)
        y_a = _attention_mixer(a_in, attn_q_norm[i], attn_k_norm[i])
        y_b = _ssd_mixer(b_in, ssm_conv_w[i], ssm_conv_b[i], ssm_dt_bias[i], ssm_a_log[i],
                         ssm_d[i], ssm_norm[i])
        y_c = _retention_mixer(c_in)
        y_d = _conv_module(d_in, conv_dw_w[i], conv_dw_b[i], conv_ln_g[i], conv_ln_b[i])
        y = jnp.concatenate([y_a, y_b, y_c, y_d], axis=-1).astype(x.dtype)
        x = x + jnp.einsum('bsc,cd->bsd', y, w_out[i])
        hf = _rmsnorm(x, norm_ffn[i])
        up = jax.nn.relu(jnp.einsum('bsd,df->bsf', hf, w_up[i]))
        x = x + jnp.einsum('bsf,fd->bsd', up * up, w_down[i])
        gate = jax.nn.sigmoid(jnp.einsum('bsd,de->bse', _rmsnorm(x, norm_ple[i]), w_ple_gate[i]))
        x = x + jnp.einsum('bsk,kd->bsd', p[i], w_ple[i]) * gate
    return x
```

```python
import functools
import math

import numpy as np
import jax
import jax.numpy as jnp
from jax import lax
from jax.experimental import pallas as pl
from jax.experimental.pallas import tpu as pltpu

F32 = jnp.float32
BF16 = jnp.bfloat16

D_MODEL = 1024
GROUP_WIDTH = 256
GROUP_HEADS = 4
HEAD_DIM = 64
EPS = 1e-6
ATTN_PATTERNS = ((128, 1), (512, 4), (2048, 16))
ATTN_BLOCK = 128
ROPE_THETA = 10000.0
SSM_STATE = 128
SSM_GROUPS = 2
SSM_CONV = 4
CHUNK = 128
CONF_KERNEL = 31
D_FF = 4 * D_MODEL
PLE_DIM = 256
NEG = -1e30

_A0 = 0
_B0 = 3 * GROUP_WIDTH
_BZ, _BX, _BB, _BC = _B0, _B0 + 256, _B0 + 512, _B0 + 768
_BDT = _B0 + 1024
_C0 = _BDT + GROUP_HEADS
_D0 = _C0 + 4 * GROUP_WIDTH
IN_COLS = _D0 + 2 * GROUP_WIDTH

_LANE = np.arange(GROUP_WIDTH)
_PERM = ((_LANE % 128) // 32) * HEAD_DIM + (_LANE // 128) * 32 + (_LANE % 32)
_HEAD_PERM = (_LANE % 128) // 32
_HEAD_NAT = _LANE // HEAD_DIM

N_PROJ = 14 * GROUP_WIDTH


def _proj_columns():
    nat = np.arange(GROUP_WIDTH)
    cols = [
        _A0 + _PERM, _A0 + 256 + _PERM, _A0 + 512 + nat,
        _BZ + nat, _BX + nat, _BB + nat, _BC + nat, _BDT + _HEAD_NAT,
        _C0 + _PERM, _C0 + 256 + _PERM, _C0 + 512 + nat, _C0 + 768 + nat,
        _D0 + nat, _D0 + 256 + nat,
    ]
    return np.concatenate(cols).astype(np.int32)


_PROJ_COLS = _proj_columns()


def _sigmoid(x):
    return 1.0 / (1.0 + jnp.exp(-x))


def _silu(x):
    return x * _sigmoid(x)


def _softplus(x):
    return jnp.maximum(x, 0.0) + jnp.log(1.0 + jnp.exp(-jnp.abs(x)))


def _dot(a, b):
    return jnp.dot(a, b, preferred_element_type=F32)


def _dot_nt(a, b):
    return lax.dot_general(a, b, (((1,), (1,)), ((), ())), preferred_element_type=F32)


def _dot_tn(a, b):
    return lax.dot_general(a, b, (((0,), (0,)), ((), ())), preferred_element_type=F32)


def _inproj_kernel(x_ref, g_ref, w_ref, cosa_ref, sina_ref, cosc_ref, sinc_ref,
                   gq_ref, gk_ref, bd_ref, oa_ref, ob_ref, oc_ref, od_ref, hb_ref):
    x = x_ref[...]
    h = x * lax.rsqrt(jnp.mean(x * x, axis=-1, keepdims=True) + EPS) * g_ref[...]
    hb_ref[...] = h.astype(BF16)

    def mm(j):
        return _dot(hb_ref[...], w_ref[:, j * 256:(j + 1) * 256])

    def rot(t, cos, sin):
        t1, t2 = t[:, :128], t[:, 128:]
        return jnp.concatenate([t1 * cos - t2 * sin, t2 * cos + t1 * sin], axis=1)

    def headnorm(t, gain):
        ss = _dot((t * t).astype(BF16), bd_ref[...])
        return t * lax.rsqrt(ss * (1.0 / HEAD_DIM) + EPS) * gain

    cosa, sina = cosa_ref[...], sina_ref[...]
    cosc, sinc = cosc_ref[...], sinc_ref[...]
    scale = HEAD_DIM ** -0.5
    oa_ref[:, 0:256] = rot(headnorm(mm(0), gq_ref[...]), cosa, sina) * scale
    oa_ref[:, 256:512] = rot(headnorm(mm(1), gk_ref[...]), cosa, sina)
    oa_ref[:, 512:768] = mm(2)
    for j in range(5):
        ob_ref[:, j * 256:(j + 1) * 256] = mm(3 + j)
    oc_ref[:, 0:256] = rot(mm(8), cosc, sinc)
    oc_ref[:, 256:512] = rot(mm(9), cosc, sinc) * scale
    oc_ref[:, 512:768] = mm(10)
    oc_ref[:, 768:1024] = mm(11)
    od_ref[:, 0:256] = mm(12)
    od_ref[:, 256:512] = mm(13)


def _inproj(x2d, g, w_cat, cosa, sina, cosc, sinc, gq, gk, bd, seq, tm):
    t = x2d.shape[0]
    nseq = seq // tm
    const = lambda i: (0, 0)
    tok = lambda i: (i, 0)
    pos = lambda i: (i % nseq, 0)
    return pl.pallas_call(
        _inproj_kernel,
        grid=(t // tm,),
        in_specs=[
            pl.BlockSpec((tm, D_MODEL), tok),
            pl.BlockSpec((1, D_MODEL), const),
            pl.BlockSpec((D_MODEL, N_PROJ), const),
            pl.BlockSpec((tm, 128), pos), pl.BlockSpec((tm, 128), pos),
            pl.BlockSpec((tm, 128), pos), pl.BlockSpec((tm, 128), pos),
            pl.BlockSpec((1, 256), const), pl.BlockSpec((1, 256), const),
            pl.BlockSpec((256, 256), const),
        ],
        out_specs=[
            pl.BlockSpec((tm, 768), tok), pl.BlockSpec((tm, 1280), tok),
            pl.BlockSpec((tm, 1024), tok), pl.BlockSpec((tm, 512), tok),
        ],
        out_shape=[
            jax.ShapeDtypeStruct((t, 768), F32), jax.ShapeDtypeStruct((t, 1280), F32),
            jax.ShapeDtypeStruct((t, 1024), F32), jax.ShapeDtypeStruct((t, 512), F32),
        ],
        scratch_shapes=[pltpu.VMEM((tm, D_MODEL), BF16)],
        compiler_params=pltpu.CompilerParams(
            dimension_semantics=("arbitrary",), vmem_limit_bytes=56 * 1024 * 1024),
        name="inproj",
    )(x2d, g, w_cat, cosa, sina, cosc, sinc, gq, gk, bd)


ATTN_TILE = 2048


def _attn_kernel(*refs):
    q_refs, kp_refs, kc_refs, vp_refs, vc_refs = (refs[2 * i:2 * i + 2] for i in range(5))
    qm_ref, o_ref = refs[10], refs[11]
    m_refs, l_refs, u_refs = refs[12:16], refs[16:20], refs[20:22]
    tile = pl.program_id(1)
    for r_ in m_refs:
        r_[...] = jnp.full(r_.shape, NEG, F32)
    for r_ in l_refs + u_refs:
        r_[...] = jnp.zeros(r_.shape, F32)

    ii = lax.broadcasted_iota(jnp.int32, (128, 256), 0)
    jj = lax.broadcasted_iota(jnp.int32, (128, 256), 1)
    band_bias = jnp.where((jj >= ii) & (jj <= ii + ATTN_BLOCK), 0.0, NEG).astype(F32)
    noprev_bias = jnp.where(jj < ATTN_BLOCK, NEG, 0.0).astype(F32)
    first_bias = band_bias + jnp.where(tile == 0, 1.0, 0.0).astype(F32) * noprev_bias
    lo64 = lax.broadcasted_iota(jnp.int32, (128, 128), 1) < HEAD_DIM

    def rows(start, dil):
        if dil == 1:
            return pl.ds(pl.multiple_of(start, 128), 128)
        return pl.ds(start, 128, stride=dil)

    def wide(pair, rr):
        return jnp.concatenate([pair[0][rr, :], pair[1][rr, :]], axis=1)

    def block(qs, klo_refs, vlo_refs, lo_s, hi_s, dil, bias):
        qr, lo, hi = rows(qs, dil), rows(lo_s, dil), rows(hi_s, dil)
        q = wide(q_refs, qr)
        k = jnp.concatenate([wide(klo_refs, lo), wide(kc_refs, hi)], axis=0).astype(BF16)
        v = [jnp.concatenate([vlo_refs[c][lo, :], vc_refs[c][hi, :]], axis=0).astype(BF16) for c in range(2)]
        alphas, pvs = [], []
        for h in range(GROUP_HEADS):
            qh = (q * qm_ref[h:h + 1, :]).astype(BF16)
            s = _dot_nt(qh, k) + bias
            m_old = m_refs[h][qr, :]
            m_new = jnp.maximum(m_old, jnp.max(s, axis=-1, keepdims=True))
            p = jnp.exp(s - jnp.concatenate([m_new, m_new], axis=1))
            alpha = jnp.exp(m_old - m_new)
            l_refs[h][qr, :] = alpha * l_refs[h][qr, :] + jnp.sum(p, axis=-1, keepdims=True)
            m_refs[h][qr, :] = m_new
            pvs.append(_dot(p.astype(BF16), v[h // 2]))
            alphas.append(alpha)
        for c in range(2):
            a_c = jnp.where(lo64, alphas[2 * c], alphas[2 * c + 1])
            pv_c = jnp.where(lo64, pvs[2 * c], pvs[2 * c + 1])
            u_refs[c][qr, :] = u_refs[c][qr, :] * a_c + pv_c

    for _, dil in ATTN_PATTERNS:
        span = ATTN_BLOCK * dil
        nblk = ATTN_TILE // span

        def per_class(r, carry, dil=dil, span=span, nblk=nblk):
            block(r, kp_refs, vp_refs, ATTN_TILE - span + r, r, dil, first_bias)

            def later(mb, c):
                base = r + mb * span
                block(base + span, kc_refs, vc_refs, base, base + span, dil, band_bias)
                return c

            if nblk > 1:
                lax.fori_loop(0, nblk - 1, later, 0)
            return carry

        lax.fori_loop(0, dil, per_class, 0)

    lo64w = lax.broadcasted_iota(jnp.int32, (256, 128), 1) < HEAD_DIM

    def finish(c, carry):
        rr = pl.ds(pl.multiple_of(c * 256, 256), 256)
        for hlf in range(2):
            den = jnp.where(lo64w, l_refs[2 * hlf][rr, :], l_refs[2 * hlf + 1][rr, :])
            o_ref[rr, hlf * 128:(hlf + 1) * 128] = u_refs[hlf][rr, :] / den
        return carry

    lax.fori_loop(0, ATTN_TILE // 256, finish, 0)


def _attention(qkv, qmask, bsz, seq):
    nt = seq // ATTN_TILE
    cur = lambda c: (lambda b, n: (b * nt + n, c))
    prev = lambda c: (lambda b, n: (b * nt + jnp.maximum(n - 1, 0), c))
    blk = (ATTN_TILE, 128)
    col_maps = [cur(0), cur(1),
                prev(2), prev(3), cur(2), cur(3),
                prev(4), prev(5), cur(4), cur(5)]
    return pl.pallas_call(
        _attn_kernel,
        grid=(bsz, nt),
        in_specs=[pl.BlockSpec(blk, m) for m in col_maps] + [pl.BlockSpec((8, 256), lambda b, n: (0, 0))],
        out_specs=pl.BlockSpec((ATTN_TILE, 256), lambda b, n: (b * nt + n, 0)),
        out_shape=jax.ShapeDtypeStruct((bsz * seq, GROUP_WIDTH), F32),
        scratch_shapes=[pltpu.VMEM((ATTN_TILE, 128), F32)] * 10,
        compiler_params=pltpu.CompilerParams(
            dimension_semantics=("arbitrary", "arbitrary"), vmem_limit_bytes=56 * 1024 * 1024),
        name="attn",
    )(*([qkv] * 10), qmask)


SEQ_TILE = 512


def _ssd_kernel(u_ref, cw_ref, cb_ref, dtb_ref, a_ref, dsk_ref, ng_ref, ltri_ref, smask_ref,
                vm_ref, o_ref, ext_ref, hs_ref):
    ts = SEQ_TILE

    @pl.when(pl.program_id(1) == 0)
    def _():
        ext_ref[0:8, :] = jnp.zeros((8, 768), F32)
        hs_ref[...] = jnp.zeros(hs_ref.shape, F32)

    ext_ref[8:8 + ts, :] = u_ref[:, 256:1024]
    conv = cb_ref[...]
    for k in range(SSM_CONV):
        conv = conv + cw_ref[k:k + 1, :] * ext_ref[pl.ds(8 - (SSM_CONV - 1) + k, ts), :]
    ext_ref[0:8, :] = ext_ref[ts:ts + 8, :]
    ext_ref[8:8 + ts, :] = _silu(conv)

    ii = lax.broadcasted_iota(jnp.int32, (CHUNK, CHUNK), 0)
    jj = lax.broadcasted_iota(jnp.int32, (CHUNK, CHUNK), 1)
    causal = jj <= ii

    def chunk(c, carry):
        r0 = pl.multiple_of(c * CHUNK, CHUNK)
        rr = pl.ds(r0, CHUNK)
        er = pl.ds(r0 + 8, CHUNK)
        z = u_ref[rr, 0:256]
        xs = ext_ref[er, 0:256]
        bm = ext_ref[er, 256:512]
        cm = ext_ref[er, 512:768]
        dt = _softplus(u_ref[rr, 1024:1280] + dtb_ref[...])
        da = dt * (-jnp.exp(a_ref[...]))
        acum = jnp.dot(ltri_ref[...], da, preferred_element_type=F32, precision=lax.Precision.HIGHEST)
        total = acum[CHUNK - 1:CHUNK, :]
        hs = hs_ref[...]
        bmb = bm.astype(BF16)
        cmb = cm.astype(BF16)
        y = _dot(cmb, hs.astype(BF16)) * jnp.exp(acum)
        xdt = xs * dt
        acum_t = (jnp.transpose(acum[:, 0:128]), jnp.transpose(acum[:, 128:256]))
        cbs = [_dot_nt(cmb[:, g * 128:(g + 1) * 128], bmb[:, g * 128:(g + 1) * 128]) for g in range(SSM_GROUPS)]
        for h in range(GROUP_HEADS):
            col = acum[:, h * HEAD_DIM:h * HEAD_DIM + 1]
            rsel = (h % 2) * HEAD_DIM
            row = acum_t[h // 2][rsel:rsel + 1, :]
            lmat = jnp.exp(jnp.where(causal, col - row, NEG))
            mh = (cbs[h // 2] * lmat).astype(BF16)
            y = y + _dot(mh, (xdt * vm_ref[h:h + 1, :]).astype(BF16))
        xw = (xs * (jnp.exp(total - acum) * dt)).astype(BF16)
        hs_ref[...] = jnp.exp(total) * hs + _dot_tn(bmb, xw) * smask_ref[...]
        y = (y + dsk_ref[...] * xs) * _silu(z)
        halves = []
        for g in range(SSM_GROUPS):
            yg = y[:, g * 128:(g + 1) * 128]
            halves.append(yg * lax.rsqrt(jnp.mean(yg * yg, axis=-1, keepdims=True) + EPS))
        o_ref[rr, :] = jnp.concatenate(halves, axis=1) * ng_ref[...]
        return carry

    lax.fori_loop(0, ts // CHUNK, chunk, 0)


def _ssd(ub, cw, cb, dtb, a_exp, dsk, ng, ltri, smask, vmask, bsz, seq):
    ts = SEQ_TILE
    nt = seq // ts
    const = lambda b, n: (0, 0)
    tok = lambda b, n: (b * nt + n, 0)
    return pl.pallas_call(
        _ssd_kernel,
        grid=(bsz, nt),
        in_specs=[
            pl.BlockSpec((ts, 1280), tok),
            pl.BlockSpec((8, 768), const), pl.BlockSpec((1, 768), const),
            pl.BlockSpec((1, 256), const), pl.BlockSpec((1, 256), const),
            pl.BlockSpec((1, 256), const), pl.BlockSpec((1, 256), const),
            pl.BlockSpec((CHUNK, CHUNK), const), pl.BlockSpec((256, 256), const),
            pl.BlockSpec((8, 256), const),
        ],
        out_specs=pl.BlockSpec((ts, 256), tok),
        out_shape=jax.ShapeDtypeStruct((bsz * seq, GROUP_WIDTH), F32),
        scratch_shapes=[pltpu.VMEM((ts + 8, 768), F32), pltpu.VMEM((256, 256), F32)],
        compiler_params=pltpu.CompilerParams(dimension_semantics=("arbitrary", "arbitrary")),
        name="ssd",
    )(ub, cw, cb, dtb, a_exp, dsk, ng, ltri, smask, vmask)


def _ret_kernel(u_ref, dmat_ref, zeta_ref, xi_ref, gch_ref, bd_ref, rmask_ref, qm_ref, vm_ref,
                o_ref, r_ref):
    @pl.when(pl.program_id(1) == 0)
    def _():
        r_ref[...] = jnp.zeros(r_ref.shape, F32)

    def chunk(c, carry):
        rr = pl.ds(pl.multiple_of(c * CHUNK, CHUNK), CHUNK)
        q = u_ref[rr, 0:256]
        k = u_ref[rr, 256:512]
        v = u_ref[rr, 512:768]
        g = u_ref[rr, 768:1024]
        kb = k.astype(BF16)
        r = r_ref[...]
        y = _dot(q.astype(BF16), r.astype(BF16)) * xi_ref[...]
        for h in range(GROUP_HEADS):
            s = _dot_nt((q * qm_ref[h:h + 1, :]).astype(BF16), kb) * dmat_ref[h]
            y = y + _dot(s.astype(BF16), (v * vm_ref[h:h + 1, :]).astype(BF16))
        ss = _dot((y * y).astype(BF16), bd_ref[...])
        o_ref[rr, :] = y * lax.rsqrt(ss * (1.0 / HEAD_DIM) + EPS) * _silu(g)
        kz = (k * zeta_ref[...]).astype(BF16)
        r_ref[...] = gch_ref[...] * r + _dot_tn(kz, v.astype(BF16)) * rmask_ref[...]
        return carry

    lax.fori_loop(0, SEQ_TILE // CHUNK, chunk, 0)


def _retention(uc, dmat, zeta, xi, gch, bdn, rmask, qmask, vmask, bsz, seq):
    ts = SEQ_TILE
    nt = seq // ts
    const = lambda b, n: (0, 0)
    tok = lambda b, n: (b * nt + n, 0)
    return pl.pallas_call(
        _ret_kernel,
        grid=(bsz, nt),
        in_specs=[
            pl.BlockSpec((ts, 1024), tok),
            pl.BlockSpec((GROUP_HEADS, CHUNK, CHUNK), lambda b, n: (0, 0, 0)),
            pl.BlockSpec((CHUNK, 256), const), pl.BlockSpec((CHUNK, 256), const),
            pl.BlockSpec((1, 256), const), pl.BlockSpec((256, 256), const),
            pl.BlockSpec((256, 256), const),
            pl.BlockSpec((8, 256), const), pl.BlockSpec((8, 256), const),
        ],
        out_specs=pl.BlockSpec((ts, 256), tok),
        out_shape=jax.ShapeDtypeStruct((bsz * seq, GROUP_WIDTH), F32),
        scratch_shapes=[pltpu.VMEM((256, 256), F32)],
        compiler_params=pltpu.CompilerParams(dimension_semantics=("arbitrary", "arbitrary")),
        name="retention",
    )(uc, dmat, zeta, xi, gch, bdn, rmask, qmask, vmask)


CONF_HALO = 32
CONF_ROWS = 64


def _conf_kernel(u_ref, w_ref, b_ref, lg_ref, lb_ref, o_ref, ext_ref):
    ts = SEQ_TILE

    @pl.when(pl.program_id(1) == 0)
    def _():
        ext_ref[0:CONF_HALO, :] = jnp.zeros((CONF_HALO, 256), F32)

    ext_ref[CONF_HALO:CONF_HALO + ts, :] = u_ref[:, 0:256] * _sigmoid(u_ref[:, 256:512])

    for c in range(ts // CONF_ROWS):
        r0 = c * CONF_ROWS
        acc = jnp.zeros((CONF_ROWS, 256), F32) + b_ref[...]
        for k in range(CONF_KERNEL):
            acc = acc + w_ref[k:k + 1, :] * ext_ref[pl.ds(r0 + CONF_HALO - (CONF_KERNEL - 1) + k, CONF_ROWS), :]
        mu = jnp.mean(acc, axis=-1, keepdims=True)
        xc = acc - mu
        var = jnp.mean(xc * xc, axis=-1, keepdims=True)
        o_ref[pl.ds(r0, CONF_ROWS), :] = _silu(xc * lax.rsqrt(var + EPS) * lg_ref[...] + lb_ref[...])
    ext_ref[0:CONF_HALO, :] = ext_ref[ts:ts + CONF_HALO, :]


def _conformer(ud, w, b, lg, lb, bsz, seq):
    ts = SEQ_TILE
    nt = seq // ts
    const = lambda b_, n: (0, 0)
    tok = lambda b_, n: (b_ * nt + n, 0)
    return pl.pallas_call(
        _conf_kernel,
        grid=(bsz, nt),
        in_specs=[
            pl.BlockSpec((ts, 512), tok),
            pl.BlockSpec((32, 256), const), pl.BlockSpec((1, 256), const),
            pl.BlockSpec((1, 256), const), pl.BlockSpec((1, 256), const),
        ],
        out_specs=pl.BlockSpec((ts, 256), tok),
        out_shape=jax.ShapeDtypeStruct((bsz * seq, GROUP_WIDTH), F32),
        scratch_shapes=[pltpu.VMEM((ts + CONF_HALO, 256), F32)],
        compiler_params=pltpu.CompilerParams(dimension_semantics=("arbitrary", "arbitrary")),
        name="conformer",
    )(ud, w, b, lg, lb)


FF_CHUNK = 512


def _rms_rows(x, g):
    return x * lax.rsqrt(jnp.mean(x * x, axis=-1, keepdims=True) + EPS) * g


def _ffn_kernel(x_ref, ya_ref, yb_ref, yc_ref, yd_ref, p_ref, wout_ref, gf_ref, wup_ref, wdn_ref,
                gp_ref, wple_ref, wgate_ref, o_ref, hb_ref):
    acc = x_ref[...]
    for m, y_ref in enumerate((ya_ref, yb_ref, yc_ref, yd_ref)):
        acc = acc + _dot(y_ref[...].astype(BF16), wout_ref[m * 256:(m + 1) * 256, :])
    hb_ref[...] = _rms_rows(acc, gf_ref[...]).astype(BF16)

    def down(c):
        up = jnp.maximum(_dot(hb_ref[...], wup_ref[:, c * FF_CHUNK:(c + 1) * FF_CHUNK]), 0.0)
        return _dot((up * up).astype(BF16), wdn_ref[c * FF_CHUNK:(c + 1) * FF_CHUNK, :])

    ff = down(0)
    for c in range(1, D_FF // FF_CHUNK):
        ff = ff + down(c)
    x2 = acc + ff
    gate = _sigmoid(_dot(_rms_rows(x2, gp_ref[...]).astype(BF16), wgate_ref[...]))
    o_ref[...] = x2 + _dot(p_ref[...].astype(BF16), wple_ref[...]) * gate


def _ffn(x2d, ya, yb, yc, yd, p2d, wout, gf, wup, wdn, gp, wple, wgate, tm):
    t = x2d.shape[0]
    const = lambda i: (0, 0)
    tok = lambda i: (i, 0)
    once = dict()
    return pl.pallas_call(
        _ffn_kernel,
        grid=(t // tm,),
        in_specs=[
            pl.BlockSpec((tm, D_MODEL), tok),
            pl.BlockSpec((tm, 256), tok), pl.BlockSpec((tm, 256), tok),
            pl.BlockSpec((tm, 256), tok), pl.BlockSpec((tm, 256), tok),
            pl.BlockSpec((tm, PLE_DIM), tok),
            pl.BlockSpec((D_MODEL, D_MODEL), const, **once),
            pl.BlockSpec((1, D_MODEL), const),
            pl.BlockSpec((D_MODEL, D_FF), const, **once),
            pl.BlockSpec((D_FF, D_MODEL), const, **once),
            pl.BlockSpec((1, D_MODEL), const),
            pl.BlockSpec((PLE_DIM, D_MODEL), const, **once),
            pl.BlockSpec((D_MODEL, D_MODEL), const, **once),
        ],
        out_specs=pl.BlockSpec((tm, D_MODEL), tok),
        out_shape=jax.ShapeDtypeStruct((t, D_MODEL), F32),
        scratch_shapes=[pltpu.VMEM((tm, D_MODEL), BF16)],
        compiler_params=pltpu.CompilerParams(
            dimension_semantics=("arbitrary",), vmem_limit_bytes=56 * 1024 * 1024),
        name="ffn",
    )(x2d, ya, yb, yc, yd, p2d, wout, gf, wup, wdn, gp, wple, wgate)


def _head_mask(head_of_lane):
    m = np.zeros((8, GROUP_WIDTH), np.float32)
    for h in range(GROUP_HEADS):
        m[h] = head_of_lane == h
    return m


def _tables(seq):
    pos = jnp.arange(seq, dtype=F32)
    ang_a = ROPE_THETA ** (-jnp.arange(0, HEAD_DIM, 2, dtype=F32) / HEAD_DIM)
    ang_c = 1.0 / (10000.0 ** jnp.linspace(0.0, 1.0, HEAD_DIM // 2, dtype=F32))

    def cs(ang):
        a = pos[:, None] * ang[None, :]
        return jnp.tile(jnp.cos(a), (1, 4)), jnp.tile(jnp.sin(a), (1, 4))

    cosa, sina = cs(ang_a)
    cosc, sinc = cs(ang_c)
    log_g = jnp.log(1.0 - 2.0 ** (-5.0 - jnp.arange(GROUP_HEADS, dtype=F32)))
    idx = jnp.arange(CHUNK, dtype=F32)
    dist = idx[:, None] - idx[None, :]
    dmat = jnp.where((dist >= 0)[None], jnp.exp(jnp.maximum(dist, 0.0)[None] * log_g[:, None, None]), 0.0)
    zeta = jnp.exp((CHUNK - 1 - idx)[:, None] * log_g)[:, _HEAD_PERM]
    xi = jnp.exp((idx + 1.0)[:, None] * log_g)[:, _HEAD_NAT]
    gch = jnp.exp(CHUNK * log_g)[_HEAD_NAT][None, :]
    return dict(
        cosa=cosa, sina=sina, cosc=cosc, sinc=sinc, dmat=dmat, zeta=zeta, xi=xi, gch=gch,
        bd_perm=jnp.asarray(_HEAD_PERM[:, None] == _HEAD_PERM[None, :], BF16),
        bd_nat=jnp.asarray(_HEAD_NAT[:, None] == _HEAD_NAT[None, :], BF16),
        rmask=jnp.asarray(_HEAD_PERM[:, None] == _HEAD_NAT[None, :], F32),
        smask=jnp.asarray((_LANE[:, None] // SSM_STATE) == (_HEAD_NAT[None, :] // (GROUP_HEADS // SSM_GROUPS)), F32),
        qmask=jnp.asarray(_head_mask(_HEAD_PERM)),
        vmask=jnp.asarray(_head_mask(_HEAD_NAT)),
        ltri=jnp.asarray(np.tril(np.ones((CHUNK, CHUNK), np.float32))),
    )


def kernel(x, p, norm_mix, w_in, attn_q_norm, attn_k_norm, ssm_conv_w, ssm_conv_b, ssm_dt_bias,
           ssm_a_log, ssm_d, ssm_norm, conv_dw_w, conv_dw_b, conv_ln_g, conv_ln_b, w_out,
           norm_ffn, w_up, w_down, norm_ple, w_ple, w_ple_gate):
    bsz, seq, _ = x.shape
    depth = w_in.shape[0]
    tm = 512
    tb = _tables(seq)
    x2d = x.reshape(bsz * seq, D_MODEL)
    for i in range(depth):
        w_cat = w_in[i][:, _PROJ_COLS].astype(BF16)
        oa, ob, oc, od = _inproj(
            x2d, norm_mix[i][None, :], w_cat, tb["cosa"], tb["sina"], tb["cosc"], tb["sinc"],
            attn_q_norm[i][_PERM % HEAD_DIM][None, :], attn_k_norm[i][_PERM % HEAD_DIM][None, :],
            tb["bd_perm"], seq, tm)
        ya = _attention(oa, tb["qmask"], bsz, seq)
        cw = jnp.zeros((8, 768), F32).at[:SSM_CONV].set(ssm_conv_w[i])
        yb = _ssd(
            ob, cw, ssm_conv_b[i][None, :], ssm_dt_bias[i][_HEAD_NAT][None, :],
            ssm_a_log[i][_HEAD_NAT][None, :], ssm_d[i][_HEAD_NAT][None, :],
            ssm_norm[i][None, :], tb["ltri"], tb["smask"], tb["vmask"], bsz, seq)
        yc = _retention(oc, tb["dmat"], tb["zeta"], tb["xi"], tb["gch"], tb["bd_nat"], tb["rmask"],
                        tb["qmask"], tb["vmask"], bsz, seq)
        dw = jnp.zeros((32, 256), F32).at[:CONF_KERNEL].set(conv_dw_w[i])
        yd = _conformer(od, dw, conv_dw_b[i][None, :], conv_ln_g[i][None, :], conv_ln_b[i][None, :], bsz, seq)
        x2d = _ffn(
            x2d, ya, yb, yc, yd, p[i].reshape(bsz * seq, PLE_DIM), w_out[i].astype(BF16),
            norm_ffn[i][None, :], w_up[i].astype(BF16), w_down[i].astype(BF16), norm_ple[i][None, :],
            w_ple[i].astype(BF16), w_ple_gate[i].astype(BF16), 256)
    return x2d.reshape(bsz, seq, D_MODEL)
```

```python
import functools
import math

import numpy as np
import jax
import jax.numpy as jnp
from jax import lax
from jax.experimental import pallas as pl
from jax.experimental.pallas import tpu as pltpu

F32 = jnp.float32
BF16 = jnp.bfloat16

D_MODEL = 1024
GROUP_WIDTH = 256
GROUP_HEADS = 4
HEAD_DIM = 64
EPS = 1e-6
ATTN_PATTERNS = ((128, 1), (512, 4), (2048, 16))
ATTN_BLOCK = 128
ROPE_THETA = 10000.0
SSM_STATE = 128
SSM_GROUPS = 2
SSM_CONV = 4
CHUNK = 128
CONF_KERNEL = 31
D_FF = 4 * D_MODEL
PLE_DIM = 256
NEG = -1e30
LOG2E = 1.4426950408889634

_A0 = 0
_B0 = 3 * GROUP_WIDTH
_BZ, _BX, _BB, _BC = _B0, _B0 + 256, _B0 + 512, _B0 + 768
_BDT = _B0 + 1024
_C0 = _BDT + GROUP_HEADS
_D0 = _C0 + 4 * GROUP_WIDTH
IN_COLS = _D0 + 2 * GROUP_WIDTH

_LANE = np.arange(GROUP_WIDTH)
_PERM = ((_LANE % 128) // 32) * HEAD_DIM + (_LANE // 128) * 32 + (_LANE % 32)
_HEAD_PERM = (_LANE % 128) // 32
_HEAD_NAT = _LANE // HEAD_DIM

N_PROJ = 14 * GROUP_WIDTH


def _proj_columns():
    nat = np.arange(GROUP_WIDTH)
    cols = [
        _A0 + _PERM, _A0 + 256 + _PERM, _A0 + 512 + nat,
        _BZ + nat, _BX + nat, _BB + nat, _BC + nat, _BDT + _HEAD_NAT,
        _C0 + _PERM, _C0 + 256 + _PERM, _C0 + 512 + nat, _C0 + 768 + nat,
        _D0 + nat, _D0 + 256 + nat,
    ]
    return np.concatenate(cols).astype(np.int32)


_PROJ_COLS = _proj_columns()


def _sigmoid(x):
    return 1.0 / (1.0 + jnp.exp(-x))


def _silu(x):
    return x * _sigmoid(x)


def _softplus(x):
    return jnp.maximum(x, 0.0) + jnp.log(1.0 + jnp.exp(-jnp.abs(x)))


def _dot(a, b):
    return jnp.dot(a, b, preferred_element_type=F32)


def _dot_nt(a, b):
    return lax.dot_general(a, b, (((1,), (1,)), ((), ())), preferred_element_type=F32)


def _dot_tn(a, b):
    return lax.dot_general(a, b, (((0,), (0,)), ((), ())), preferred_element_type=F32)


def _inproj_kernel(x_ref, g_ref, w_ref, cosa_ref, sina_ref, cosc_ref, sinc_ref,
                   gq_ref, gk_ref, bd_ref, oa_ref, ob_ref, oc_ref, od_ref, hb_ref):
    x = x_ref[...]
    h = x * lax.rsqrt(jnp.mean(x * x, axis=-1, keepdims=True) + EPS) * g_ref[...]
    hb_ref[...] = h.astype(BF16)

    def mm(j):
        return _dot(hb_ref[...], w_ref[:, j * 256:(j + 1) * 256])

    def rot(t, cos, sin):
        t1, t2 = t[:, :128], t[:, 128:]
        return jnp.concatenate([t1 * cos - t2 * sin, t2 * cos + t1 * sin], axis=1)

    def headnorm(t, gain):
        ss = _dot((t * t).astype(BF16), bd_ref[...])
        return t * lax.rsqrt(ss * (1.0 / HEAD_DIM) + EPS) * gain

    cosa, sina = cosa_ref[...], sina_ref[...]
    cosc, sinc = cosc_ref[...], sinc_ref[...]
    scale = HEAD_DIM ** -0.5
    oa_ref[:, 0:256] = rot(headnorm(mm(0), gq_ref[...]), cosa, sina) * (scale * LOG2E)
    oa_ref[:, 256:512] = rot(headnorm(mm(1), gk_ref[...]), cosa, sina)
    oa_ref[:, 512:768] = mm(2)
    for j in range(5):
        ob_ref[:, j * 256:(j + 1) * 256] = mm(3 + j)
    oc_ref[:, 0:256] = rot(mm(8), cosc, sinc)
    oc_ref[:, 256:512] = rot(mm(9), cosc, sinc) * scale
    oc_ref[:, 512:768] = mm(10)
    oc_ref[:, 768:1024] = mm(11)
    od_ref[:, 0:256] = mm(12)
    od_ref[:, 256:512] = mm(13)


def _inproj(x2d, g, w_cat, cosa, sina, cosc, sinc, gq, gk, bd, seq, tm):
    t = x2d.shape[0]
    nseq = seq // tm
    const = lambda i: (0, 0)
    tok = lambda i: (i, 0)
    pos = lambda i: (i % nseq, 0)
    return pl.pallas_call(
        _inproj_kernel,
        grid=(t // tm,),
        in_specs=[
            pl.BlockSpec((tm, D_MODEL), tok),
            pl.BlockSpec((1, D_MODEL), const),
            pl.BlockSpec((D_MODEL, N_PROJ), const),
            pl.BlockSpec((tm, 128), pos), pl.BlockSpec((tm, 128), pos),
            pl.BlockSpec((tm, 128), pos), pl.BlockSpec((tm, 128), pos),
            pl.BlockSpec((1, 256), const), pl.BlockSpec((1, 256), const),
            pl.BlockSpec((256, 256), const),
        ],
        out_specs=[
            pl.BlockSpec((tm, 768), tok), pl.BlockSpec((tm, 1280), tok),
            pl.BlockSpec((tm, 1024), tok), pl.BlockSpec((tm, 512), tok),
        ],
        out_shape=[
            jax.ShapeDtypeStruct((t, 768), F32), jax.ShapeDtypeStruct((t, 1280), F32),
            jax.ShapeDtypeStruct((t, 1024), F32), jax.ShapeDtypeStruct((t, 512), F32),
        ],
        scratch_shapes=[pltpu.VMEM((tm, D_MODEL), BF16)],
        compiler_params=pltpu.CompilerParams(
            dimension_semantics=("arbitrary",), vmem_limit_bytes=56 * 1024 * 1024),
        name="inproj",
    )(x2d, g, w_cat, cosa, sina, cosc, sinc, gq, gk, bd)


ATTN_TILE = 2048


def _attn_kernel(*refs):
    q_refs, kp_refs, kc_refs, vp_refs, vc_refs = (refs[2 * i:2 * i + 2] for i in range(5))
    qm_ref, o_ref = refs[10], refs[11]
    m_refs, l_refs, u_refs = refs[12:16], refs[16:20], refs[20:22]
    tile = pl.program_id(1)
    for r_ in m_refs:
        r_[...] = jnp.full(r_.shape, NEG, F32)
    for r_ in l_refs + u_refs:
        r_[...] = jnp.zeros(r_.shape, F32)

    ii = lax.broadcasted_iota(jnp.int32, (128, 256), 0)
    jj = lax.broadcasted_iota(jnp.int32, (128, 256), 1)
    band_bias = jnp.where((jj >= ii) & (jj <= ii + ATTN_BLOCK), 0.0, NEG).astype(F32)
    noprev_bias = jnp.where(jj < ATTN_BLOCK, NEG, 0.0).astype(F32)
    first_bias = band_bias + jnp.where(tile == 0, 1.0, 0.0).astype(F32) * noprev_bias
    lo64 = lax.broadcasted_iota(jnp.int32, (128, 128), 1) < HEAD_DIM

    def rows(start, dil):
        if dil == 1:
            return pl.ds(pl.multiple_of(start, 128), 128)
        return pl.ds(start, 128, stride=dil)

    def wide(pair, rr):
        return jnp.concatenate([pair[0][rr, :], pair[1][rr, :]], axis=1)

    def blocks(descs, dil):
        st = []
        for qs, klo_refs, vlo_refs, lo_s, hi_s, bias in descs:
            qr, lo, hi = rows(qs, dil), rows(lo_s, dil), rows(hi_s, dil)
            q = wide(q_refs, qr)
            k = jnp.concatenate([wide(klo_refs, lo), wide(kc_refs, hi)], axis=0).astype(BF16)
            v = [jnp.concatenate([vlo_refs[c][lo, :], vc_refs[c][hi, :]], axis=0).astype(BF16) for c in range(2)]
            q4 = jnp.concatenate([(q * qm_ref[h:h + 1, :]).astype(BF16) for h in range(GROUP_HEADS)], axis=0)
            st.append(dict(qr=qr, v=v, bias=bias, s_all=_dot_nt(q4, k)))
        for d in st:
            d["m_old"] = [m_refs[h][d["qr"], :] for h in range(GROUP_HEADS)]
            d["l_old"] = [l_refs[h][d["qr"], :] for h in range(GROUP_HEADS)]
        for d in st:
            d["m_new"], d["l_new"], d["alpha"], ps = [], [], [], []
            for h in range(GROUP_HEADS):
                s = d["s_all"][h * 128:(h + 1) * 128, :] + d["bias"]
                m_new = jnp.maximum(d["m_old"][h], jnp.max(s, axis=-1, keepdims=True))
                p = jnp.exp2(s - jnp.concatenate([m_new, m_new], axis=1))
                alpha = jnp.exp2(d["m_old"][h] - m_new)
                d["l_new"].append(alpha * d["l_old"][h] + jnp.sum(p, axis=-1, keepdims=True))
                d["m_new"].append(m_new)
                d["alpha"].append(alpha)
                ps.append(p.astype(BF16))
            d["u_new"] = []
            for c in range(2):
                pv = _dot(jnp.concatenate(ps[2 * c:2 * c + 2], axis=0), d["v"][c])
                a_c = jnp.where(lo64, d["alpha"][2 * c], d["alpha"][2 * c + 1])
                pv_c = jnp.where(lo64, pv[0:128, :], pv[128:256, :])
                d["u_new"].append((a_c, pv_c))
        for d in st:
            d["u_old"] = [u_refs[c][d["qr"], :] for c in range(2)]
        for d in st:
            for h in range(GROUP_HEADS):
                m_refs[h][d["qr"], :] = d["m_new"][h]
                l_refs[h][d["qr"], :] = d["l_new"][h]
            for c in range(2):
                a_c, pv_c = d["u_new"][c]
                u_refs[c][d["qr"], :] = d["u_old"][c] * a_c + pv_c

    def first_desc(r, span):
        return (r, kp_refs, vp_refs, ATTN_TILE - span + r, r, first_bias)

    def later_desc(r, m, span):
        base = r + (m - 1) * span
        return (base + span, kc_refs, vc_refs, base, base + span, band_bias)

    for _, dil in ATTN_PATTERNS:
        span = ATTN_BLOCK * dil
        nblk = ATTN_TILE // span
        if nblk == 1:
            def class_pair(i, carry, dil=dil, span=span):
                blocks([first_desc(2 * i, span), first_desc(2 * i + 1, span)], dil)
                return carry

            lax.fori_loop(0, dil // 2, class_pair, 0)
        else:
            def per_class(r, carry, dil=dil, span=span, nblk=nblk):
                blocks([first_desc(r, span), later_desc(r, 1, span)], dil)

                def later_pair(i, c):
                    blocks([later_desc(r, 2 * i + 2, span), later_desc(r, 2 * i + 3, span)], dil)
                    return c

                lax.fori_loop(0, (nblk - 2) // 2, later_pair, 0)
                return carry

            lax.fori_loop(0, dil, per_class, 0)

    lo64w = lax.broadcasted_iota(jnp.int32, (256, 128), 1) < HEAD_DIM

    def finish(c, carry):
        rr = pl.ds(pl.multiple_of(c * 256, 256), 256)
        for hlf in range(2):
            den = jnp.where(lo64w, l_refs[2 * hlf][rr, :], l_refs[2 * hlf + 1][rr, :])
            o_ref[rr, hlf * 128:(hlf + 1) * 128] = u_refs[hlf][rr, :] / den
        return carry

    lax.fori_loop(0, ATTN_TILE // 256, finish, 0)


def _attention(qkv, qmask, bsz, seq):
    nt = seq // ATTN_TILE
    cur = lambda c: (lambda b, n: (b * nt + n, c))
    prev = lambda c: (lambda b, n: (b * nt + jnp.maximum(n - 1, 0), c))
    blk = (ATTN_TILE, 128)
    col_maps = [cur(0), cur(1),
                prev(2), prev(3), cur(2), cur(3),
                prev(4), prev(5), cur(4), cur(5)]
    return pl.pallas_call(
        _attn_kernel,
        grid=(bsz, nt),
        in_specs=[pl.BlockSpec(blk, m) for m in col_maps] + [pl.BlockSpec((8, 256), lambda b, n: (0, 0))],
        out_specs=pl.BlockSpec((ATTN_TILE, 256), lambda b, n: (b * nt + n, 0)),
        out_shape=jax.ShapeDtypeStruct((bsz * seq, GROUP_WIDTH), F32),
        scratch_shapes=[pltpu.VMEM((ATTN_TILE, 128), F32)] * 10,
        compiler_params=pltpu.CompilerParams(
            dimension_semantics=("arbitrary", "arbitrary"), vmem_limit_bytes=56 * 1024 * 1024),
        name="attn",
    )(*([qkv] * 10), qmask)


SEQ_TILE = 512


def _ssd_kernel(u_ref, cw_ref, cb_ref, dtb_ref, a_ref, dsk_ref, ng_ref, ltri_ref, smask_ref,
                vm_ref, o_ref, ext_ref, hs_ref):
    ts = SEQ_TILE

    @pl.when(pl.program_id(1) == 0)
    def _():
        ext_ref[0:8, :] = jnp.zeros((8, 768), F32)
        hs_ref[...] = jnp.zeros(hs_ref.shape, F32)

    ext_ref[8:8 + ts, :] = u_ref[:, 256:1024]
    conv = cb_ref[...]
    for k in range(SSM_CONV):
        conv = conv + cw_ref[k:k + 1, :] * ext_ref[pl.ds(8 - (SSM_CONV - 1) + k, ts), :]
    ext_ref[0:8, :] = ext_ref[ts:ts + 8, :]
    ext_ref[8:8 + ts, :] = _silu(conv)

    ii = lax.broadcasted_iota(jnp.int32, (CHUNK, CHUNK), 0)
    jj = lax.broadcasted_iota(jnp.int32, (CHUNK, CHUNK), 1)
    causal = jj <= ii

    def chunk(c, carry):
        r0 = pl.multiple_of(c * CHUNK, CHUNK)
        rr = pl.ds(r0, CHUNK)
        er = pl.ds(r0 + 8, CHUNK)
        z = u_ref[rr, 0:256]
        xs = ext_ref[er, 0:256]
        bm = ext_ref[er, 256:512]
        cm = ext_ref[er, 512:768]
        dt = _softplus(u_ref[rr, 1024:1280] + dtb_ref[...])
        da = dt * (-jnp.exp(a_ref[...]))
        acum = jnp.dot(ltri_ref[...], da, preferred_element_type=F32, precision=lax.Precision.HIGHEST)
        total = acum[CHUNK - 1:CHUNK, :]
        hs = hs_ref[...]
        bmb = bm.astype(BF16)
        cmb = cm.astype(BF16)
        y = _dot(cmb, hs.astype(BF16)) * jnp.exp(acum)
        xdt = xs * dt
        acum_t = (jnp.transpose(acum[:, 0:128]), jnp.transpose(acum[:, 128:256]))
        cbs = [_dot_nt(cmb[:, g * 128:(g + 1) * 128], bmb[:, g * 128:(g + 1) * 128]) for g in range(SSM_GROUPS)]
        for h in range(GROUP_HEADS):
            col = acum[:, h * HEAD_DIM:h * HEAD_DIM + 1]
            rsel = (h % 2) * HEAD_DIM
            row = acum_t[h // 2][rsel:rsel + 1, :]
            lmat = jnp.exp(jnp.where(causal, col - row, NEG))
            mh = (cbs[h // 2] * lmat).astype(BF16)
            y = y + _dot(mh, (xdt * vm_ref[h:h + 1, :]).astype(BF16))
        xw = (xs * (jnp.exp(total - acum) * dt)).astype(BF16)
        hs_ref[...] = jnp.exp(total) * hs + _dot_tn(bmb, xw) * smask_ref[...]
        y = (y + dsk_ref[...] * xs) * _silu(z)
        halves = []
        for g in range(SSM_GROUPS):
            yg = y[:, g * 128:(g + 1) * 128]
            halves.append(yg * lax.rsqrt(jnp.mean(yg * yg, axis=-1, keepdims=True) + EPS))
        o_ref[rr, :] = jnp.concatenate(halves, axis=1) * ng_ref[...]
        return carry

    lax.fori_loop(0, ts // CHUNK, chunk, 0)


def _ssd(ub, cw, cb, dtb, a_exp, dsk, ng, ltri, smask, vmask, bsz, seq):
    ts = SEQ_TILE
    nt = seq // ts
    const = lambda b, n: (0, 0)
    tok = lambda b, n: (b * nt + n, 0)
    return pl.pallas_call(
        _ssd_kernel,
        grid=(bsz, nt),
        in_specs=[
            pl.BlockSpec((ts, 1280), tok),
            pl.BlockSpec((8, 768), const), pl.BlockSpec((1, 768), const),
            pl.BlockSpec((1, 256), const), pl.BlockSpec((1, 256), const),
            pl.BlockSpec((1, 256), const), pl.BlockSpec((1, 256), const),
            pl.BlockSpec((CHUNK, CHUNK), const), pl.BlockSpec((256, 256), const),
            pl.BlockSpec((8, 256), const),
        ],
        out_specs=pl.BlockSpec((ts, 256), tok),
        out_shape=jax.ShapeDtypeStruct((bsz * seq, GROUP_WIDTH), F32),
        scratch_shapes=[pltpu.VMEM((ts + 8, 768), F32), pltpu.VMEM((256, 256), F32)],
        compiler_params=pltpu.CompilerParams(dimension_semantics=("arbitrary", "arbitrary")),
        name="ssd",
    )(ub, cw, cb, dtb, a_exp, dsk, ng, ltri, smask, vmask)


def _ret_kernel(u_ref, dmat_ref, zeta_ref, xi_ref, gch_ref, bd_ref, rmask_ref, qm_ref, vm_ref,
                o_ref, r_ref):
    @pl.when(pl.program_id(1) == 0)
    def _():
        r_ref[...] = jnp.zeros(r_ref.shape, F32)

    def chunk(c, carry):
        rr = pl.ds(pl.multiple_of(c * CHUNK, CHUNK), CHUNK)
        q = u_ref[rr, 0:256]
        k = u_ref[rr, 256:512]
        v = u_ref[rr, 512:768]
        g = u_ref[rr, 768:1024]
        kb = k.astype(BF16)
        r = r_ref[...]
        y = _dot(q.astype(BF16), r.astype(BF16)) * xi_ref[...]
        for h in range(GROUP_HEADS):
            s = _dot_nt((q * qm_ref[h:h + 1, :]).astype(BF16), kb) * dmat_ref[h]
            y = y + _dot(s.astype(BF16), (v * vm_ref[h:h + 1, :]).astype(BF16))
        ss = _dot((y * y).astype(BF16), bd_ref[...])
        o_ref[rr, :] = y * lax.rsqrt(ss * (1.0 / HEAD_DIM) + EPS) * _silu(g)
        kz = (k * zeta_ref[...]).astype(BF16)
        r_ref[...] = gch_ref[...] * r + _dot_tn(kz, v.astype(BF16)) * rmask_ref[...]
        return carry

    lax.fori_loop(0, SEQ_TILE // CHUNK, chunk, 0)


def _retention(uc, dmat, zeta, xi, gch, bdn, rmask, qmask, vmask, bsz, seq):
    ts = SEQ_TILE
    nt = seq // ts
    const = lambda b, n: (0, 0)
    tok = lambda b, n: (b * nt + n, 0)
    return pl.pallas_call(
        _ret_kernel,
        grid=(bsz, nt),
        in_specs=[
            pl.BlockSpec((ts, 1024), tok),
            pl.BlockSpec((GROUP_HEADS, CHUNK, CHUNK), lambda b, n: (0, 0, 0)),
            pl.BlockSpec((CHUNK, 256), const), pl.BlockSpec((CHUNK, 256), const),
            pl.BlockSpec((1, 256), const), pl.BlockSpec((256, 256), const),
            pl.BlockSpec((256, 256), const),
            pl.BlockSpec((8, 256), const), pl.BlockSpec((8, 256), const),
        ],
        out_specs=pl.BlockSpec((ts, 256), tok),
        out_shape=jax.ShapeDtypeStruct((bsz * seq, GROUP_WIDTH), F32),
        scratch_shapes=[pltpu.VMEM((256, 256), F32)],
        compiler_params=pltpu.CompilerParams(dimension_semantics=("arbitrary", "arbitrary")),
        name="retention",
    )(uc, dmat, zeta, xi, gch, bdn, rmask, qmask, vmask)


CONF_HALO = 32
CONF_ROWS = 64


def _conf_kernel(u_ref, w_ref, b_ref, lg_ref, lb_ref, o_ref, ext_ref):
    ts = SEQ_TILE

    @pl.when(pl.program_id(1) == 0)
    def _():
        ext_ref[0:CONF_HALO, :] = jnp.zeros((CONF_HALO, 256), F32)

    ext_ref[CONF_HALO:CONF_HALO + ts, :] = u_ref[:, 0:256] * _sigmoid(u_ref[:, 256:512])

    for c in range(ts // CONF_ROWS):
        r0 = c * CONF_ROWS
        acc = jnp.zeros((CONF_ROWS, 256), F32) + b_ref[...]
        for k in range(CONF_KERNEL):
            acc = acc + w_ref[k:k + 1, :] * ext_ref[pl.ds(r0 + CONF_HALO - (CONF_KERNEL - 1) + k, CONF_ROWS), :]
        mu = jnp.mean(acc, axis=-1, keepdims=True)
        xc = acc - mu
        var = jnp.mean(xc * xc, axis=-1, keepdims=True)
        o_ref[pl.ds(r0, CONF_ROWS), :] = _silu(xc * lax.rsqrt(var + EPS) * lg_ref[...] + lb_ref[...])
    ext_ref[0:CONF_HALO, :] = ext_ref[ts:ts + CONF_HALO, :]


def _conformer(ud, w, b, lg, lb, bsz, seq):
    ts = SEQ_TILE
    nt = seq // ts
    const = lambda b_, n: (0, 0)
    tok = lambda b_, n: (b_ * nt + n, 0)
    return pl.pallas_call(
        _conf_kernel,
        grid=(bsz, nt),
        in_specs=[
            pl.BlockSpec((ts, 512), tok),
            pl.BlockSpec((32, 256), const), pl.BlockSpec((1, 256), const),
            pl.BlockSpec((1, 256), const), pl.BlockSpec((1, 256), const),
        ],
        out_specs=pl.BlockSpec((ts, 256), tok),
        out_shape=jax.ShapeDtypeStruct((bsz * seq, GROUP_WIDTH), F32),
        scratch_shapes=[pltpu.VMEM((ts + CONF_HALO, 256), F32)],
        compiler_params=pltpu.CompilerParams(dimension_semantics=("arbitrary", "arbitrary")),
        name="conformer",
    )(ud, w, b, lg, lb)


FF_CHUNK = 512


def _rms_rows(x, g):
    return x * lax.rsqrt(jnp.mean(x * x, axis=-1, keepdims=True) + EPS) * g


def _ffn_kernel(x_ref, ya_ref, yb_ref, yc_ref, yd_ref, p_ref, wout_ref, gf_ref, wup_ref, wdn_ref,
                gp_ref, wple_ref, wgate_ref, o_ref, hb_ref):
    acc = x_ref[...]
    for m, y_ref in enumerate((ya_ref, yb_ref, yc_ref, yd_ref)):
        acc = acc + _dot(y_ref[...].astype(BF16), wout_ref[m * 256:(m + 1) * 256, :])
    hb_ref[...] = _rms_rows(acc, gf_ref[...]).astype(BF16)

    def down(c):
        up = jnp.maximum(_dot(hb_ref[...], wup_ref[:, c * FF_CHUNK:(c + 1) * FF_CHUNK]), 0.0)
        return _dot((up * up).astype(BF16), wdn_ref[c * FF_CHUNK:(c + 1) * FF_CHUNK, :])

    ff = down(0)
    for c in range(1, D_FF // FF_CHUNK):
        ff = ff + down(c)
    x2 = acc + ff
    gate = _sigmoid(_dot(_rms_rows(x2, gp_ref[...]).astype(BF16), wgate_ref[...]))
    o_ref[...] = x2 + _dot(p_ref[...].astype(BF16), wple_ref[...]) * gate


def _ffn(x2d, ya, yb, yc, yd, p2d, wout, gf, wup, wdn, gp, wple, wgate, tm):
    t = x2d.shape[0]
    const = lambda i: (0, 0)
    tok = lambda i: (i, 0)
    once = dict()
    return pl.pallas_call(
        _ffn_kernel,
        grid=(t // tm,),
        in_specs=[
            pl.BlockSpec((tm, D_MODEL), tok),
            pl.BlockSpec((tm, 256), tok), pl.BlockSpec((tm, 256), tok),
            pl.BlockSpec((tm, 256), tok), pl.BlockSpec((tm, 256), tok),
            pl.BlockSpec((tm, PLE_DIM), tok),
            pl.BlockSpec((D_MODEL, D_MODEL), const, **once),
            pl.BlockSpec((1, D_MODEL), const),
            pl.BlockSpec((D_MODEL, D_FF), const, **once),
            pl.BlockSpec((D_FF, D_MODEL), const, **once),
            pl.BlockSpec((1, D_MODEL), const),
            pl.BlockSpec((PLE_DIM, D_MODEL), const, **once),
            pl.BlockSpec((D_MODEL, D_MODEL), const, **once),
        ],
        out_specs=pl.BlockSpec((tm, D_MODEL), tok),
        out_shape=jax.ShapeDtypeStruct((t, D_MODEL), F32),
        scratch_shapes=[pltpu.VMEM((tm, D_MODEL), BF16)],
        compiler_params=pltpu.CompilerParams(
            dimension_semantics=("arbitrary",), vmem_limit_bytes=56 * 1024 * 1024),
        name="ffn",
    )(x2d, ya, yb, yc, yd, p2d, wout, gf, wup, wdn, gp, wple, wgate)


def _head_mask(head_of_lane):
    m = np.zeros((8, GROUP_WIDTH), np.float32)
    for h in range(GROUP_HEADS):
        m[h] = head_of_lane == h
    return m


def _tables(seq):
    pos = jnp.arange(seq, dtype=F32)
    ang_a = ROPE_THETA ** (-jnp.arange(0, HEAD_DIM, 2, dtype=F32) / HEAD_DIM)
    ang_c = 1.0 / (10000.0 ** jnp.linspace(0.0, 1.0, HEAD_DIM // 2, dtype=F32))

    def cs(ang):
        a = pos[:, None] * ang[None, :]
        return jnp.tile(jnp.cos(a), (1, 4)), jnp.tile(jnp.sin(a), (1, 4))

    cosa, sina = cs(ang_a)
    cosc, sinc = cs(ang_c)
    log_g = jnp.log(1.0 - 2.0 ** (-5.0 - jnp.arange(GROUP_HEADS, dtype=F32)))
    idx = jnp.arange(CHUNK, dtype=F32)
    dist = idx[:, None] - idx[None, :]
    dmat = jnp.where((dist >= 0)[None], jnp.exp(jnp.maximum(dist, 0.0)[None] * log_g[:, None, None]), 0.0)
    zeta = jnp.exp((CHUNK - 1 - idx)[:, None] * log_g)[:, _HEAD_PERM]
    xi = jnp.exp((idx + 1.0)[:, None] * log_g)[:, _HEAD_NAT]
    gch = jnp.exp(CHUNK * log_g)[_HEAD_NAT][None, :]
    return dict(
        cosa=cosa, sina=sina, cosc=cosc, sinc=sinc, dmat=dmat, zeta=zeta, xi=xi, gch=gch,
        bd_perm=jnp.asarray(_HEAD_PERM[:, None] == _HEAD_PERM[None, :], BF16),
        bd_nat=jnp.asarray(_HEAD_NAT[:, None] == _HEAD_NAT[None, :], BF16),
        rmask=jnp.asarray(_HEAD_PERM[:, None] == _HEAD_NAT[None, :], F32),
        smask=jnp.asarray((_LANE[:, None] // SSM_STATE) == (_HEAD_NAT[None, :] // (GROUP_HEADS // SSM_GROUPS)), F32),
        qmask=jnp.asarray(_head_mask(_HEAD_PERM)),
        vmask=jnp.asarray(_head_mask(_HEAD_NAT)),
        ltri=jnp.asarray(np.tril(np.ones((CHUNK, CHUNK), np.float32))),
    )


def kernel(x, p, norm_mix, w_in, attn_q_norm, attn_k_norm, ssm_conv_w, ssm_conv_b, ssm_dt_bias,
           ssm_a_log, ssm_d, ssm_norm, conv_dw_w, conv_dw_b, conv_ln_g, conv_ln_b, w_out,
           norm_ffn, w_up, w_down, norm_ple, w_ple, w_ple_gate):
    bsz, seq, _ = x.shape
    depth = w_in.shape[0]
    tm = 512
    tb = _tables(seq)
    x2d = x.reshape(bsz * seq, D_MODEL)
    for i in range(depth):
        w_cat = w_in[i][:, _PROJ_COLS].astype(BF16)
        oa, ob, oc, od = _inproj(
            x2d, norm_mix[i][None, :], w_cat, tb["cosa"], tb["sina"], tb["cosc"], tb["sinc"],
            attn_q_norm[i][_PERM % HEAD_DIM][None, :], attn_k_norm[i][_PERM % HEAD_DIM][None, :],
            tb["bd_perm"], seq, tm)
        ya = _attention(oa, tb["qmask"], bsz, seq)
        cw = jnp.zeros((8, 768), F32).at[:SSM_CONV].set(ssm_conv_w[i])
        yb = _ssd(
            ob, cw, ssm_conv_b[i][None, :], ssm_dt_bias[i][_HEAD_NAT][None, :],
            ssm_a_log[i][_HEAD_NAT][None, :], ssm_d[i][_HEAD_NAT][None, :],
            ssm_norm[i][None, :], tb["ltri"], tb["smask"], tb["vmask"], bsz, seq)
        yc = _retention(oc, tb["dmat"], tb["zeta"], tb["xi"], tb["gch"], tb["bd_nat"], tb["rmask"],
                        tb["qmask"], tb["vmask"], bsz, seq)
        dw = jnp.zeros((32, 256), F32).at[:CONF_KERNEL].set(conv_dw_w[i])
        yd = _conformer(od, dw, conv_dw_b[i][None, :], conv_ln_g[i][None, :], conv_ln_b[i][None, :], bsz, seq)
        x2d = _ffn(
            x2d, ya, yb, yc, yd, p[i].reshape(bsz * seq, PLE_DIM), w_out[i].astype(BF16),
            norm_ffn[i][None, :], w_up[i].astype(BF16), w_down[i].astype(BF16), norm_ple[i][None, :],
            w_ple[i].astype(BF16), w_ple_gate[i].astype(BF16), 256)
    return x2d.reshape(bsz, seq, D_MODEL)
```

```python
import functools
import math

import numpy as np
import jax
import jax.numpy as jnp
from jax import lax
from jax.experimental import pallas as pl
from jax.experimental.pallas import tpu as pltpu

F32 = jnp.float32
BF16 = jnp.bfloat16

D_MODEL = 1024
GROUP_WIDTH = 256
GROUP_HEADS = 4
HEAD_DIM = 64
EPS = 1e-6
ATTN_PATTERNS = ((128, 1), (512, 4), (2048, 16))
ATTN_BLOCK = 128
ROPE_THETA = 10000.0
SSM_STATE = 128
SSM_GROUPS = 2
SSM_CONV = 4
CHUNK = 128
CONF_KERNEL = 31
D_FF = 4 * D_MODEL
PLE_DIM = 256
NEG = -1e30
LOG2E = 1.4426950408889634

_A0 = 0
_B0 = 3 * GROUP_WIDTH
_BZ, _BX, _BB, _BC = _B0, _B0 + 256, _B0 + 512, _B0 + 768
_BDT = _B0 + 1024
_C0 = _BDT + GROUP_HEADS
_D0 = _C0 + 4 * GROUP_WIDTH
IN_COLS = _D0 + 2 * GROUP_WIDTH

_LANE = np.arange(GROUP_WIDTH)
_PERM = ((_LANE % 128) // 32) * HEAD_DIM + (_LANE // 128) * 32 + (_LANE % 32)
_HEAD_PERM = (_LANE % 128) // 32
_HEAD_NAT = _LANE // HEAD_DIM

N_PROJ = 14 * GROUP_WIDTH


def _proj_columns():
    nat = np.arange(GROUP_WIDTH)
    cols = [
        _A0 + _PERM, _A0 + 256 + _PERM, _A0 + 512 + nat,
        _BZ + nat, _BX + nat, _BB + nat, _BC + nat, _BDT + _HEAD_NAT,
        _C0 + _PERM, _C0 + 256 + _PERM, _C0 + 512 + nat, _C0 + 768 + nat,
        _D0 + nat, _D0 + 256 + nat,
    ]
    return np.concatenate(cols).astype(np.int32)


_PROJ_COLS = _proj_columns()


def _sigmoid(x):
    return 1.0 / (1.0 + jnp.exp(-x))


def _silu(x):
    return x * _sigmoid(x)


def _softplus(x):
    return jnp.maximum(x, 0.0) + jnp.log(1.0 + jnp.exp(-jnp.abs(x)))


def _dot(a, b):
    return jnp.dot(a, b, preferred_element_type=F32)


def _dot_nt(a, b):
    return lax.dot_general(a, b, (((1,), (1,)), ((), ())), preferred_element_type=F32)


def _dot_tn(a, b):
    return lax.dot_general(a, b, (((0,), (0,)), ((), ())), preferred_element_type=F32)


def _inproj_kernel(x_ref, g_ref, w_ref, cosa_ref, sina_ref, cosc_ref, sinc_ref,
                   gq_ref, gk_ref, bd_ref, oa_ref, ob_ref, oc_ref, od_ref, hb_ref):
    x = x_ref[...]
    h = x * lax.rsqrt(jnp.mean(x * x, axis=-1, keepdims=True) + EPS) * g_ref[...]
    hb_ref[...] = h.astype(BF16)

    def mm(j):
        return _dot(hb_ref[...], w_ref[:, j * 256:(j + 1) * 256])

    def rot(t, cos, sin):
        t1, t2 = t[:, :128], t[:, 128:]
        return jnp.concatenate([t1 * cos - t2 * sin, t2 * cos + t1 * sin], axis=1)

    def headnorm(t, gain):
        ss = _dot((t * t).astype(BF16), bd_ref[...])
        return t * lax.rsqrt(ss * (1.0 / HEAD_DIM) + EPS) * gain

    cosa, sina = cosa_ref[...], sina_ref[...]
    cosc, sinc = cosc_ref[...], sinc_ref[...]
    scale = HEAD_DIM ** -0.5
    oa_ref[:, 0:256] = rot(headnorm(mm(0), gq_ref[...]), cosa, sina) * (scale * LOG2E)
    oa_ref[:, 256:512] = rot(headnorm(mm(1), gk_ref[...]), cosa, sina)
    oa_ref[:, 512:768] = mm(2)
    for j in range(5):
        ob_ref[:, j * 256:(j + 1) * 256] = mm(3 + j)
    oc_ref[:, 0:256] = rot(mm(8), cosc, sinc)
    oc_ref[:, 256:512] = rot(mm(9), cosc, sinc) * scale
    oc_ref[:, 512:768] = mm(10)
    oc_ref[:, 768:1024] = mm(11)
    od_ref[:, 0:256] = mm(12)
    od_ref[:, 256:512] = mm(13)


def _inproj(x2d, g, w_cat, cosa, sina, cosc, sinc, gq, gk, bd, seq, tm, layer):
    t = x2d.shape[0]
    nseq = seq // tm
    const = lambda i: (0, 0)
    lay = lambda i: (layer, 0, 0)
    tok = lambda i: (i, 0)
    pos = lambda i: (i % nseq, 0)
    return pl.pallas_call(
        _inproj_kernel,
        grid=(t // tm,),
        in_specs=[
            pl.BlockSpec((tm, D_MODEL), tok),
            pl.BlockSpec((None, 1, D_MODEL), lay),
            pl.BlockSpec((None, D_MODEL, N_PROJ), lay),
            pl.BlockSpec((tm, 128), pos), pl.BlockSpec((tm, 128), pos),
            pl.BlockSpec((tm, 128), pos), pl.BlockSpec((tm, 128), pos),
            pl.BlockSpec((None, 1, 256), lay), pl.BlockSpec((None, 1, 256), lay),
            pl.BlockSpec((256, 256), const),
        ],
        out_specs=[
            pl.BlockSpec((tm, 768), tok), pl.BlockSpec((tm, 1280), tok),
            pl.BlockSpec((tm, 1024), tok), pl.BlockSpec((tm, 512), tok),
        ],
        out_shape=[
            jax.ShapeDtypeStruct((t, 768), F32), jax.ShapeDtypeStruct((t, 1280), F32),
            jax.ShapeDtypeStruct((t, 1024), F32), jax.ShapeDtypeStruct((t, 512), F32),
        ],
        scratch_shapes=[pltpu.VMEM((tm, D_MODEL), BF16)],
        compiler_params=pltpu.CompilerParams(
            dimension_semantics=("arbitrary",), vmem_limit_bytes=56 * 1024 * 1024),
        name="inproj",
    )(x2d, g, w_cat, cosa, sina, cosc, sinc, gq, gk, bd)


ATTN_TILE = 2048


def _attn_kernel(*refs):
    q_refs, kp_refs, kc_refs, vp_refs, vc_refs = (refs[2 * i:2 * i + 2] for i in range(5))
    qm_ref, o_ref = refs[10], refs[11]
    m_refs, l_refs, u_refs = refs[12:16], refs[16:20], refs[20:22]
    tile = pl.program_id(1)
    for r_ in m_refs:
        r_[...] = jnp.full(r_.shape, NEG, F32)
    for r_ in l_refs + u_refs:
        r_[...] = jnp.zeros(r_.shape, F32)

    ii = lax.broadcasted_iota(jnp.int32, (128, 256), 0)
    jj = lax.broadcasted_iota(jnp.int32, (128, 256), 1)
    band_bias = jnp.where((jj >= ii) & (jj <= ii + ATTN_BLOCK), 0.0, NEG).astype(F32)
    noprev_bias = jnp.where(jj < ATTN_BLOCK, NEG, 0.0).astype(F32)
    first_bias = band_bias + jnp.where(tile == 0, 1.0, 0.0).astype(F32) * noprev_bias
    lo64 = lax.broadcasted_iota(jnp.int32, (128, 128), 1) < HEAD_DIM

    def rows(start, dil):
        if dil == 1:
            return pl.ds(pl.multiple_of(start, 128), 128)
        return pl.ds(start, 128, stride=dil)

    def wide(pair, rr):
        return jnp.concatenate([pair[0][rr, :], pair[1][rr, :]], axis=1)

    def blocks(descs, dil):
        st = []
        for qs, klo_refs, vlo_refs, lo_s, hi_s, bias in descs:
            qr, lo, hi = rows(qs, dil), rows(lo_s, dil), rows(hi_s, dil)
            q = wide(q_refs, qr)
            k = jnp.concatenate([wide(klo_refs, lo), wide(kc_refs, hi)], axis=0).astype(BF16)
            v = [jnp.concatenate([vlo_refs[c][lo, :], vc_refs[c][hi, :]], axis=0).astype(BF16) for c in range(2)]
            q4 = jnp.concatenate([(q * qm_ref[h:h + 1, :]).astype(BF16) for h in range(GROUP_HEADS)], axis=0)
            st.append(dict(qr=qr, v=v, bias=bias, s_all=_dot_nt(q4, k)))
        for d in st:
            d["m_old"] = [m_refs[h][d["qr"], :] for h in range(GROUP_HEADS)]
            d["l_old"] = [l_refs[h][d["qr"], :] for h in range(GROUP_HEADS)]
        for d in st:
            d["m_new"], d["l_new"], d["alpha"], ps = [], [], [], []
            for h in range(GROUP_HEADS):
                s = d["s_all"][h * 128:(h + 1) * 128, :] + d["bias"]
                m_new = jnp.maximum(d["m_old"][h], jnp.max(s, axis=-1, keepdims=True))
                p = jnp.exp2(s - jnp.concatenate([m_new, m_new], axis=1))
                alpha = jnp.exp2(d["m_old"][h] - m_new)
                d["l_new"].append(alpha * d["l_old"][h] + jnp.sum(p, axis=-1, keepdims=True))
                d["m_new"].append(m_new)
                d["alpha"].append(alpha)
                ps.append(p.astype(BF16))
            d["u_new"] = []
            for c in range(2):
                pv = _dot(jnp.concatenate(ps[2 * c:2 * c + 2], axis=0), d["v"][c])
                a_c = jnp.where(lo64, d["alpha"][2 * c], d["alpha"][2 * c + 1])
                pv_c = jnp.where(lo64, pv[0:128, :], pv[128:256, :])
                d["u_new"].append((a_c, pv_c))
        for d in st:
            d["u_old"] = [u_refs[c][d["qr"], :] for c in range(2)]
        for d in st:
            for h in range(GROUP_HEADS):
                m_refs[h][d["qr"], :] = d["m_new"][h]
                l_refs[h][d["qr"], :] = d["l_new"][h]
            for c in range(2):
                a_c, pv_c = d["u_new"][c]
                u_refs[c][d["qr"], :] = d["u_old"][c] * a_c + pv_c

    def first_desc(r, span):
        return (r, kp_refs, vp_refs, ATTN_TILE - span + r, r, first_bias)

    def later_desc(r, m, span):
        base = r + (m - 1) * span
        return (base + span, kc_refs, vc_refs, base, base + span, band_bias)

    for _, dil in ATTN_PATTERNS:
        span = ATTN_BLOCK * dil
        nblk = ATTN_TILE // span
        if nblk == 1:
            def class_pair(i, carry, dil=dil, span=span):
                blocks([first_desc(2 * i, span), first_desc(2 * i + 1, span)], dil)
                return carry

            lax.fori_loop(0, dil // 2, class_pair, 0)
        else:
            def per_class(r, carry, dil=dil, span=span, nblk=nblk):
                blocks([first_desc(r, span), later_desc(r, 1, span)], dil)

                def later_pair(i, c):
                    blocks([later_desc(r, 2 * i + 2, span), later_desc(r, 2 * i + 3, span)], dil)
                    return c

                lax.fori_loop(0, (nblk - 2) // 2, later_pair, 0)
                return carry

            lax.fori_loop(0, dil, per_class, 0)

    lo64w = lax.broadcasted_iota(jnp.int32, (256, 128), 1) < HEAD_DIM

    def finish(c, carry):
        rr = pl.ds(pl.multiple_of(c * 256, 256), 256)
        for hlf in range(2):
            den = jnp.where(lo64w, l_refs[2 * hlf][rr, :], l_refs[2 * hlf + 1][rr, :])
            o_ref[rr, hlf * 128:(hlf + 1) * 128] = u_refs[hlf][rr, :] / den
        return carry

    lax.fori_loop(0, ATTN_TILE // 256, finish, 0)


def _attention(qkv, qmask, bsz, seq):
    nt = seq // ATTN_TILE
    cur = lambda c: (lambda b, n: (b * nt + n, c))
    prev = lambda c: (lambda b, n: (b * nt + jnp.maximum(n - 1, 0), c))
    blk = (ATTN_TILE, 128)
    col_maps = [cur(0), cur(1),
                prev(2), prev(3), cur(2), cur(3),
                prev(4), prev(5), cur(4), cur(5)]
    return pl.pallas_call(
        _attn_kernel,
        grid=(bsz, nt),
        in_specs=[pl.BlockSpec(blk, m) for m in col_maps] + [pl.BlockSpec((8, 256), lambda b, n: (0, 0))],
        out_specs=pl.BlockSpec((ATTN_TILE, 256), lambda b, n: (b * nt + n, 0)),
        out_shape=jax.ShapeDtypeStruct((bsz * seq, GROUP_WIDTH), F32),
        scratch_shapes=[pltpu.VMEM((ATTN_TILE, 128), F32)] * 10,
        compiler_params=pltpu.CompilerParams(
            dimension_semantics=("arbitrary", "arbitrary"), vmem_limit_bytes=56 * 1024 * 1024),
        name="attn",
    )(*([qkv] * 10), qmask)


SEQ_TILE = 512


def _ssd_kernel(u_ref, cw_ref, cb_ref, dtb_ref, a_ref, dsk_ref, ng_ref, ltri_ref, smask_ref,
                vm_ref, o_ref, ext_ref, hs_ref):
    ts = SEQ_TILE

    @pl.when(pl.program_id(1) == 0)
    def _():
        ext_ref[0:8, :] = jnp.zeros((8, 768), F32)
        hs_ref[...] = jnp.zeros(hs_ref.shape, F32)

    ext_ref[8:8 + ts, :] = u_ref[:, 256:1024]
    conv = cb_ref[...]
    for k in range(SSM_CONV):
        conv = conv + cw_ref[k:k + 1, :] * ext_ref[pl.ds(8 - (SSM_CONV - 1) + k, ts), :]
    ext_ref[0:8, :] = ext_ref[ts:ts + 8, :]
    ext_ref[8:8 + ts, :] = _silu(conv)

    ii = lax.broadcasted_iota(jnp.int32, (CHUNK, CHUNK), 0)
    jj = lax.broadcasted_iota(jnp.int32, (CHUNK, CHUNK), 1)
    causal = jj <= ii

    hs = hs_ref[...]
    for c in range(ts // CHUNK):
        rr = pl.ds(c * CHUNK, CHUNK)
        er = pl.ds(c * CHUNK + 8, CHUNK)
        z = u_ref[rr, 0:256]
        xs = ext_ref[er, 0:256]
        bm = ext_ref[er, 256:512]
        cm = ext_ref[er, 512:768]
        dt = _softplus(u_ref[rr, 1024:1280] + dtb_ref[...])
        da = dt * (-jnp.exp(a_ref[...]))
        acum = jnp.dot(ltri_ref[...], da, preferred_element_type=F32, precision=lax.Precision.HIGHEST)
        total = acum[CHUNK - 1:CHUNK, :]
        bmb = bm.astype(BF16)
        cmb = cm.astype(BF16)
        y = _dot(cmb, hs.astype(BF16)) * jnp.exp(acum)
        xdt = xs * dt
        acum_t = (jnp.transpose(acum[:, 0:128]), jnp.transpose(acum[:, 128:256]))
        cbs = [_dot_nt(cmb[:, g * 128:(g + 1) * 128], bmb[:, g * 128:(g + 1) * 128]) for g in range(SSM_GROUPS)]
        for h in range(GROUP_HEADS):
            col = acum[:, h * HEAD_DIM:h * HEAD_DIM + 1]
            rsel = (h % 2) * HEAD_DIM
            row = acum_t[h // 2][rsel:rsel + 1, :]
            lmat = jnp.exp(jnp.where(causal, col - row, NEG))
            mh = (cbs[h // 2] * lmat).astype(BF16)
            y = y + _dot(mh, (xdt * vm_ref[h:h + 1, :]).astype(BF16))
        xw = (xs * (jnp.exp(total - acum) * dt)).astype(BF16)
        hs = jnp.exp(total) * hs + _dot_tn(bmb, xw) * smask_ref[...]
        y = (y + dsk_ref[...] * xs) * _silu(z)
        halves = []
        for g in range(SSM_GROUPS):
            yg = y[:, g * 128:(g + 1) * 128]
            halves.append(yg * lax.rsqrt(jnp.mean(yg * yg, axis=-1, keepdims=True) + EPS))
        o_ref[rr, :] = jnp.concatenate(halves, axis=1) * ng_ref[...]
    hs_ref[...] = hs


def _ssd(ub, cw, cb, dtb, a_exp, dsk, ng, ltri, smask, vmask, bsz, seq, layer):
    ts = SEQ_TILE
    nt = seq // ts
    const = lambda b, n: (0, 0)
    lay = lambda b, n: (layer, 0, 0)
    tok = lambda b, n: (b * nt + n, 0)
    return pl.pallas_call(
        _ssd_kernel,
        grid=(bsz, nt),
        in_specs=[
            pl.BlockSpec((ts, 1280), tok),
            pl.BlockSpec((None, 8, 768), lay), pl.BlockSpec((None, 1, 768), lay),
            pl.BlockSpec((None, 1, 256), lay), pl.BlockSpec((None, 1, 256), lay),
            pl.BlockSpec((None, 1, 256), lay), pl.BlockSpec((None, 1, 256), lay),
            pl.BlockSpec((CHUNK, CHUNK), const), pl.BlockSpec((256, 256), const),
            pl.BlockSpec((8, 256), const),
        ],
        out_specs=pl.BlockSpec((ts, 256), tok),
        out_shape=jax.ShapeDtypeStruct((bsz * seq, GROUP_WIDTH), F32),
        scratch_shapes=[pltpu.VMEM((ts + 8, 768), F32), pltpu.VMEM((256, 256), F32)],
        compiler_params=pltpu.CompilerParams(dimension_semantics=("arbitrary", "arbitrary")),
        name="ssd",
    )(ub, cw, cb, dtb, a_exp, dsk, ng, ltri, smask, vmask)


def _ret_kernel(u_ref, dmat_ref, zeta_ref, xi_ref, gch_ref, bd_ref, rmask_ref, qm_ref, vm_ref,
                o_ref, r_ref):
    @pl.when(pl.program_id(1) == 0)
    def _():
        r_ref[...] = jnp.zeros(r_ref.shape, F32)

    r = r_ref[...]
    for c in range(SEQ_TILE // CHUNK):
        rr = pl.ds(c * CHUNK, CHUNK)
        q = u_ref[rr, 0:256]
        k = u_ref[rr, 256:512]
        v = u_ref[rr, 512:768]
        g = u_ref[rr, 768:1024]
        kb = k.astype(BF16)
        y = _dot(q.astype(BF16), r.astype(BF16)) * xi_ref[...]
        for h in range(GROUP_HEADS):
            s = _dot_nt((q * qm_ref[h:h + 1, :]).astype(BF16), kb) * dmat_ref[h]
            y = y + _dot(s.astype(BF16), (v * vm_ref[h:h + 1, :]).astype(BF16))
        ss = _dot((y * y).astype(BF16), bd_ref[...])
        o_ref[rr, :] = y * lax.rsqrt(ss * (1.0 / HEAD_DIM) + EPS) * _silu(g)
        kz = (k * zeta_ref[...]).astype(BF16)
        r = gch_ref[...] * r + _dot_tn(kz, v.astype(BF16)) * rmask_ref[...]
    r_ref[...] = r


def _retention(uc, dmat, zeta, xi, gch, bdn, rmask, qmask, vmask, bsz, seq):
    ts = SEQ_TILE
    nt = seq // ts
    const = lambda b, n: (0, 0)
    tok = lambda b, n: (b * nt + n, 0)
    return pl.pallas_call(
        _ret_kernel,
        grid=(bsz, nt),
        in_specs=[
            pl.BlockSpec((ts, 1024), tok),
            pl.BlockSpec((GROUP_HEADS, CHUNK, CHUNK), lambda b, n: (0, 0, 0)),
            pl.BlockSpec((CHUNK, 256), const), pl.BlockSpec((CHUNK, 256), const),
            pl.BlockSpec((1, 256), const), pl.BlockSpec((256, 256), const),
            pl.BlockSpec((256, 256), const),
            pl.BlockSpec((8, 256), const), pl.BlockSpec((8, 256), const),
        ],
        out_specs=pl.BlockSpec((ts, 256), tok),
        out_shape=jax.ShapeDtypeStruct((bsz * seq, GROUP_WIDTH), F32),
        scratch_shapes=[pltpu.VMEM((256, 256), F32)],
        compiler_params=pltpu.CompilerParams(dimension_semantics=("arbitrary", "arbitrary")),
        name="retention",
    )(uc, dmat, zeta, xi, gch, bdn, rmask, qmask, vmask)


CONF_HALO = 32
CONF_ROWS = 64


def _conf_kernel(u_ref, w_ref, b_ref, lg_ref, lb_ref, o_ref, ext_ref):
    ts = SEQ_TILE

    @pl.when(pl.program_id(1) == 0)
    def _():
        ext_ref[0:CONF_HALO, :] = jnp.zeros((CONF_HALO, 256), F32)
        ext_ref[CONF_HALO + ts:CONF_HALO + ts + 8, :] = jnp.zeros((8, 256), F32)

    ext_ref[CONF_HALO:CONF_HALO + ts, :] = u_ref[:, 0:256] * _sigmoid(u_ref[:, 256:512])

    nwin = CONF_ROWS + CONF_HALO + 8
    for c in range(ts // CONF_ROWS):
        r0 = c * CONF_ROWS
        win = ext_ref[pl.ds(r0, nwin), :]
        acc = jnp.zeros((CONF_ROWS, 256), F32) + b_ref[...]
        for off in range(8):
            shifted = win if off == 0 else pltpu.roll(win, nwin - off, axis=0)
            for sh in range(CONF_HALO - CONF_KERNEL + 1, CONF_HALO + 1):
                if sh % 8 == off:
                    k = sh - (CONF_HALO - CONF_KERNEL + 1)
                    acc = acc + w_ref[k:k + 1, :] * shifted[sh - off:sh - off + CONF_ROWS, :]
        mu = jnp.mean(acc, axis=-1, keepdims=True)
        xc = acc - mu
        var = jnp.mean(xc * xc, axis=-1, keepdims=True)
        o_ref[pl.ds(r0, CONF_ROWS), :] = _silu(xc * lax.rsqrt(var + EPS) * lg_ref[...] + lb_ref[...])
    ext_ref[0:CONF_HALO, :] = ext_ref[ts:ts + CONF_HALO, :]


def _conformer(ud, w, b, lg, lb, bsz, seq, layer):
    ts = SEQ_TILE
    nt = seq // ts
    lay = lambda b_, n: (layer, 0, 0)
    tok = lambda b_, n: (b_ * nt + n, 0)
    return pl.pallas_call(
        _conf_kernel,
        grid=(bsz, nt),
        in_specs=[
            pl.BlockSpec((ts, 512), tok),
            pl.BlockSpec((None, 32, 256), lay), pl.BlockSpec((None, 1, 256), lay),
            pl.BlockSpec((None, 1, 256), lay), pl.BlockSpec((None, 1, 256), lay),
        ],
        out_specs=pl.BlockSpec((ts, 256), tok),
        out_shape=jax.ShapeDtypeStruct((bsz * seq, GROUP_WIDTH), F32),
        scratch_shapes=[pltpu.VMEM((ts + CONF_HALO + 8, 256), F32)],
        compiler_params=pltpu.CompilerParams(dimension_semantics=("arbitrary", "arbitrary")),
        name="conformer",
    )(ud, w, b, lg, lb)


FF_CHUNK = 512


def _rms_rows(x, g):
    return x * lax.rsqrt(jnp.mean(x * x, axis=-1, keepdims=True) + EPS) * g


def _ffn_kernel(x_ref, ya_ref, yb_ref, yc_ref, yd_ref, p_ref, wout_ref, gf_ref, wup_ref, wdn_ref,
                gp_ref, wple_ref, wgate_ref, o_ref, hb_ref):
    tm = x_ref.shape[0]
    halves = [pl.ds(0, tm // 2), pl.ds(tm // 2, tm // 2)]
    x1 = []
    for rr in halves:
        acc = x_ref[rr, :]
        for m, y_ref in enumerate((ya_ref, yb_ref, yc_ref, yd_ref)):
            acc = acc + _dot(y_ref[rr, :].astype(BF16), wout_ref[m * 256:(m + 1) * 256, :])
        hb_ref[rr, :] = _rms_rows(acc, gf_ref[...]).astype(BF16)
        x1.append(acc)

    def down(rr, c):
        up = jnp.maximum(_dot(hb_ref[rr, :], wup_ref[:, c * FF_CHUNK:(c + 1) * FF_CHUNK]), 0.0)
        return _dot((up * up).astype(BF16), wdn_ref[c * FF_CHUNK:(c + 1) * FF_CHUNK, :])

    ffs = [down(rr, 0) for rr in halves]
    for c in range(1, D_FF // FF_CHUNK):
        ffs = [ff + down(rr, c) for ff, rr in zip(ffs, halves)]
    for rr, acc, ff in zip(halves, x1, ffs):
        x2 = acc + ff
        gate = _sigmoid(_dot(_rms_rows(x2, gp_ref[...]).astype(BF16), wgate_ref[...]))
        o_ref[rr, :] = x2 + _dot(p_ref[rr, :].astype(BF16), wple_ref[...]) * gate


def _ffn(x2d, ya, yb, yc, yd, p_all, wout, gf, wup, wdn, gp, wple, wgate, tm, layer):
    t = x2d.shape[0]
    tok = lambda i: (i, 0)
    lay = lambda i: (layer, 0, 0)
    once = dict(pipeline_mode=pl.Buffered(1))
    return pl.pallas_call(
        _ffn_kernel,
        grid=(t // tm,),
        in_specs=[
            pl.BlockSpec((tm, D_MODEL), tok),
            pl.BlockSpec((tm, 256), tok), pl.BlockSpec((tm, 256), tok),
            pl.BlockSpec((tm, 256), tok), pl.BlockSpec((tm, 256), tok),
            pl.BlockSpec((None, tm, PLE_DIM), lambda i: (layer, i, 0)),
            pl.BlockSpec((None, D_MODEL, D_MODEL), lay, **once),
            pl.BlockSpec((None, 1, D_MODEL), lay),
            pl.BlockSpec((None, D_MODEL, D_FF), lay, **once),
            pl.BlockSpec((None, D_FF, D_MODEL), lay, **once),
            pl.BlockSpec((None, 1, D_MODEL), lay),
            pl.BlockSpec((None, PLE_DIM, D_MODEL), lay, **once),
            pl.BlockSpec((None, D_MODEL, D_MODEL), lay, **once),
        ],
        out_specs=pl.BlockSpec((tm, D_MODEL), tok),
        out_shape=jax.ShapeDtypeStruct((t, D_MODEL), F32),
        scratch_shapes=[pltpu.VMEM((tm, D_MODEL), BF16)],
        compiler_params=pltpu.CompilerParams(
            dimension_semantics=("arbitrary",), vmem_limit_bytes=56 * 1024 * 1024),
        name="ffn",
    )(x2d, ya, yb, yc, yd, p_all, wout, gf, wup, wdn, gp, wple, wgate)


def _head_mask(head_of_lane):
    m = np.zeros((8, GROUP_WIDTH), np.float32)
    for h in range(GROUP_HEADS):
        m[h] = head_of_lane == h
    return m


def _tables(seq):
    pos = jnp.arange(seq, dtype=F32)
    ang_a = ROPE_THETA ** (-jnp.arange(0, HEAD_DIM, 2, dtype=F32) / HEAD_DIM)
    ang_c = 1.0 / (10000.0 ** jnp.linspace(0.0, 1.0, HEAD_DIM // 2, dtype=F32))

    def cs(ang):
        a = pos[:, None] * ang[None, :]
        return jnp.tile(jnp.cos(a), (1, 4)), jnp.tile(jnp.sin(a), (1, 4))

    cosa, sina = cs(ang_a)
    cosc, sinc = cs(ang_c)
    log_g = jnp.log(1.0 - 2.0 ** (-5.0 - jnp.arange(GROUP_HEADS, dtype=F32)))
    idx = jnp.arange(CHUNK, dtype=F32)
    dist = idx[:, None] - idx[None, :]
    dmat = jnp.where((dist >= 0)[None], jnp.exp(jnp.maximum(dist, 0.0)[None] * log_g[:, None, None]), 0.0)
    zeta = jnp.exp((CHUNK - 1 - idx)[:, None] * log_g)[:, _HEAD_PERM]
    xi = jnp.exp((idx + 1.0)[:, None] * log_g)[:, _HEAD_NAT]
    gch = jnp.exp(CHUNK * log_g)[_HEAD_NAT][None, :]
    return dict(
        cosa=cosa, sina=sina, cosc=cosc, sinc=sinc, dmat=dmat, zeta=zeta, xi=xi, gch=gch,
        bd_perm=jnp.asarray(_HEAD_PERM[:, None] == _HEAD_PERM[None, :], BF16),
        bd_nat=jnp.asarray(_HEAD_NAT[:, None] == _HEAD_NAT[None, :], BF16),
        rmask=jnp.asarray(_HEAD_PERM[:, None] == _HEAD_NAT[None, :], F32),
        smask=jnp.asarray((_LANE[:, None] // SSM_STATE) == (_HEAD_NAT[None, :] // (GROUP_HEADS // SSM_GROUPS)), F32),
        qmask=jnp.asarray(_head_mask(_HEAD_PERM)),
        vmask=jnp.asarray(_head_mask(_HEAD_NAT)),
        ltri=jnp.asarray(np.tril(np.ones((CHUNK, CHUNK), np.float32))),
    )


def _build_w_cat(w_in):
    w = w_in.astype(BF16)
    depth = w.shape[0]

    def nat(c0):
        return w[:, :, c0:c0 + 256]

    def perm(c0):
        blk = w[:, :, c0:c0 + 256].reshape(depth, D_MODEL, GROUP_HEADS, 2, 32)
        return blk.transpose(0, 1, 3, 2, 4).reshape(depth, D_MODEL, 256)

    dt = jnp.repeat(w[:, :, _BDT:_BDT + GROUP_HEADS], HEAD_DIM, axis=2)
    return jnp.concatenate([
        perm(_A0), perm(_A0 + 256), nat(_A0 + 512),
        nat(_BZ), nat(_BX), nat(_BB), nat(_BC), dt,
        perm(_C0), perm(_C0 + 256), nat(_C0 + 512), nat(_C0 + 768),
        nat(_D0), nat(_D0 + 256)], axis=2)


def kernel(x, p, norm_mix, w_in, attn_q_norm, attn_k_norm, ssm_conv_w, ssm_conv_b, ssm_dt_bias,
           ssm_a_log, ssm_d, ssm_norm, conv_dw_w, conv_dw_b, conv_ln_g, conv_ln_b, w_out,
           norm_ffn, w_up, w_down, norm_ple, w_ple, w_ple_gate):
    bsz, seq, _ = x.shape
    depth = w_in.shape[0]
    tm = 512
    tb = _tables(seq)
    row = lambda a: a[:, None, :]
    w_cat = _build_w_cat(w_in)
    g_mix, g_ffn, g_ple = row(norm_mix), row(norm_ffn), row(norm_ple)
    gq, gk = row(attn_q_norm[:, _PERM % HEAD_DIM]), row(attn_k_norm[:, _PERM % HEAD_DIM])
    cw = jnp.pad(ssm_conv_w, ((0, 0), (0, 8 - SSM_CONV), (0, 0)))
    dw = jnp.pad(conv_dw_w, ((0, 0), (0, 32 - CONF_KERNEL), (0, 0)))
    dtb, alog, dsk = row(ssm_dt_bias[:, _HEAD_NAT]), row(ssm_a_log[:, _HEAD_NAT]), row(ssm_d[:, _HEAD_NAT])
    wout, wup, wdn = w_out.astype(BF16), w_up.astype(BF16), w_down.astype(BF16)
    wple, wgate = w_ple.astype(BF16), w_ple_gate.astype(BF16)
    p_all = p.reshape(depth, bsz * seq, PLE_DIM)
    x2d = x.reshape(bsz * seq, D_MODEL)
    for i in range(depth):
        oa, ob, oc, od = _inproj(x2d, g_mix, w_cat, tb["cosa"], tb["sina"], tb["cosc"], tb["sinc"],
                                 gq, gk, tb["bd_perm"], seq, tm, i)
        ya = _attention(oa, tb["qmask"], bsz, seq)
        yb = _ssd(ob, cw, row(ssm_conv_b), dtb, alog, dsk, row(ssm_norm), tb["ltri"], tb["smask"],
                  tb["vmask"], bsz, seq, i)
        yc = _retention(oc, tb["dmat"], tb["zeta"], tb["xi"], tb["gch"], tb["bd_nat"], tb["rmask"],
                        tb["qmask"], tb["vmask"], bsz, seq)
        yd = _conformer(od, dw, row(conv_dw_b), row(conv_ln_g), row(conv_ln_b), bsz, seq, i)
        x2d = _ffn(x2d, ya, yb, yc, yd, p_all, wout, g_ffn, wup, wdn, g_ple, wple, wgate, tm, i)
    return x2d.reshape(bsz, seq, D_MODEL)
```

```python
import functools
import math

import numpy as np
import jax
import jax.numpy as jnp
from jax import lax
from jax.experimental import pallas as pl
from jax.experimental.pallas import tpu as pltpu

F32 = jnp.float32
BF16 = jnp.bfloat16

D_MODEL = 1024
GROUP_WIDTH = 256
GROUP_HEADS = 4
HEAD_DIM = 64
EPS = 1e-6
ATTN_PATTERNS = ((128, 1), (512, 4), (2048, 16))
ATTN_BLOCK = 128
ROPE_THETA = 10000.0
SSM_STATE = 128
SSM_GROUPS = 2
SSM_CONV = 4
CHUNK = 128
CONF_KERNEL = 31
D_FF = 4 * D_MODEL
PLE_DIM = 256
NEG = -1e30
LOG2E = 1.4426950408889634

_A0 = 0
_B0 = 3 * GROUP_WIDTH
_BZ, _BX, _BB, _BC = _B0, _B0 + 256, _B0 + 512, _B0 + 768
_BDT = _B0 + 1024
_C0 = _BDT + GROUP_HEADS
_D0 = _C0 + 4 * GROUP_WIDTH
IN_COLS = _D0 + 2 * GROUP_WIDTH

_LANE = np.arange(GROUP_WIDTH)
_PERM = ((_LANE % 128) // 32) * HEAD_DIM + (_LANE // 128) * 32 + (_LANE % 32)
_HEAD_PERM = (_LANE % 128) // 32
_HEAD_NAT = _LANE // HEAD_DIM

N_PROJ = 14 * GROUP_WIDTH


def _proj_columns():
    nat = np.arange(GROUP_WIDTH)
    cols = [
        _A0 + _PERM, _A0 + 256 + _PERM, _A0 + 512 + nat,
        _BZ + nat, _BX + nat, _BB + nat, _BC + nat, _BDT + _HEAD_NAT,
        _C0 + _PERM, _C0 + 256 + _PERM, _C0 + 512 + nat, _C0 + 768 + nat,
        _D0 + nat, _D0 + 256 + nat,
    ]
    return np.concatenate(cols).astype(np.int32)


_PROJ_COLS = _proj_columns()


def _sigmoid(x):
    return 1.0 / (1.0 + jnp.exp(-x))


def _silu(x):
    return x * _sigmoid(x)


def _softplus(x):
    return jnp.maximum(x, 0.0) + jnp.log(1.0 + jnp.exp(-jnp.abs(x)))


def _dot(a, b):
    return jnp.dot(a, b, preferred_element_type=F32)


def _dot_nt(a, b):
    return lax.dot_general(a, b, (((1,), (1,)), ((), ())), preferred_element_type=F32)


def _dot_tn(a, b):
    return lax.dot_general(a, b, (((0,), (0,)), ((), ())), preferred_element_type=F32)


def _inproj_kernel(x0_ref, xn_ref, g_ref, w_ref, cosa_ref, sina_ref, cosc_ref, sinc_ref,
                   gq_ref, gk_ref, bd_ref, oa_ref, ob_ref, oc_ref, od_ref, hb0_ref, hb1_ref):
    step = pl.program_id(0)

    def norm_into(src_ref, dst_ref):
        x = src_ref[...]
        h = x * lax.rsqrt(jnp.mean(x * x, axis=-1, keepdims=True) + EPS) * g_ref[...]
        dst_ref[...] = h.astype(BF16)

    @pl.when(step == 0)
    def _():
        norm_into(x0_ref, hb0_ref)

    @pl.when(step % 2 == 0)
    def _():
        norm_into(xn_ref, hb1_ref)
        _inproj_body(hb0_ref, w_ref, cosa_ref, sina_ref, cosc_ref, sinc_ref, gq_ref, gk_ref, bd_ref,
                     oa_ref, ob_ref, oc_ref, od_ref)

    @pl.when(step % 2 == 1)
    def _():
        norm_into(xn_ref, hb0_ref)
        _inproj_body(hb1_ref, w_ref, cosa_ref, sina_ref, cosc_ref, sinc_ref, gq_ref, gk_ref, bd_ref,
                     oa_ref, ob_ref, oc_ref, od_ref)


def _inproj_body(hb_ref, w_ref, cosa_ref, sina_ref, cosc_ref, sinc_ref, gq_ref, gk_ref, bd_ref,
                 oa_ref, ob_ref, oc_ref, od_ref):
    def mm(j):
        return _dot(hb_ref[...], w_ref[:, j * 256:(j + 1) * 256])

    def rot(t, cos, sin):
        t1, t2 = t[:, :128], t[:, 128:]
        return jnp.concatenate([t1 * cos - t2 * sin, t2 * cos + t1 * sin], axis=1)

    def headnorm(t, gain):
        ss = _dot((t * t).astype(BF16), bd_ref[...])
        return t * lax.rsqrt(ss * (1.0 / HEAD_DIM) + EPS) * gain

    cosa, sina = cosa_ref[...], sina_ref[...]
    cosc, sinc = cosc_ref[...], sinc_ref[...]
    scale = HEAD_DIM ** -0.5
    oa_ref[:, 0:256] = mm(0)
    oa_ref[:, 256:512] = mm(1)
    oa_ref[:, 512:768] = mm(2)
    for j in range(5):
        ob_ref[:, j * 256:(j + 1) * 256] = mm(3 + j)
    oa_ref[:, 0:256] = rot(headnorm(oa_ref[:, 0:256], gq_ref[...]), cosa, sina) * (scale * LOG2E)
    oa_ref[:, 256:512] = rot(headnorm(oa_ref[:, 256:512], gk_ref[...]), cosa, sina)
    oc_ref[:, 0:256] = rot(mm(8), cosc, sinc)
    oc_ref[:, 256:512] = rot(mm(9), cosc, sinc) * scale
    oc_ref[:, 512:768] = mm(10)
    oc_ref[:, 768:1024] = mm(11)
    od_ref[:, 0:256] = mm(12)
    od_ref[:, 256:512] = mm(13)


def _inproj(x2d, g, w_cat, cosa, sina, cosc, sinc, gq, gk, bd, seq, tm, layer):
    t = x2d.shape[0]
    nseq = seq // tm
    const = lambda i: (0, 0)
    lay = lambda i: (layer, 0, 0)
    tok = lambda i: (i, 0)
    pos = lambda i: (i % nseq, 0)
    nxt = lambda i: (jnp.minimum(i + 1, t // tm - 1), 0)
    return pl.pallas_call(
        _inproj_kernel,
        grid=(t // tm,),
        in_specs=[
            pl.BlockSpec((tm, D_MODEL), const), pl.BlockSpec((tm, D_MODEL), nxt),
            pl.BlockSpec((None, 1, D_MODEL), lay),
            pl.BlockSpec((None, D_MODEL, N_PROJ), lay),
            pl.BlockSpec((tm, 128), pos), pl.BlockSpec((tm, 128), pos),
            pl.BlockSpec((tm, 128), pos), pl.BlockSpec((tm, 128), pos),
            pl.BlockSpec((None, 1, 256), lay), pl.BlockSpec((None, 1, 256), lay),
            pl.BlockSpec((256, 256), const),
        ],
        out_specs=[
            pl.BlockSpec((tm, 768), tok), pl.BlockSpec((tm, 1280), tok),
            pl.BlockSpec((tm, 1024), tok), pl.BlockSpec((tm, 512), tok),
        ],
        out_shape=[
            jax.ShapeDtypeStruct((t, 768), F32), jax.ShapeDtypeStruct((t, 1280), F32),
            jax.ShapeDtypeStruct((t, 1024), F32), jax.ShapeDtypeStruct((t, 512), F32),
        ],
        scratch_shapes=[pltpu.VMEM((tm, D_MODEL), BF16), pltpu.VMEM((tm, D_MODEL), BF16)],
        compiler_params=pltpu.CompilerParams(
            dimension_semantics=("arbitrary",), vmem_limit_bytes=56 * 1024 * 1024),
        name="inproj",
    )(x2d, x2d, g, w_cat, cosa, sina, cosc, sinc, gq, gk, bd)


ATTN_TILE = 2048


def _attn_kernel(*refs):
    q_refs, kp_refs, kc_refs, vp_refs, vc_refs = (refs[2 * i:2 * i + 2] for i in range(5))
    qm_ref, o_ref = refs[10], refs[11]
    m_refs, l_refs, u_refs = refs[12:16], refs[16:20], refs[20:22]
    tile = pl.program_id(1)

    ii = lax.broadcasted_iota(jnp.int32, (128, 256), 0)
    jj = lax.broadcasted_iota(jnp.int32, (128, 256), 1)
    band_bias = jnp.where((jj >= ii) & (jj <= ii + ATTN_BLOCK), 0.0, NEG).astype(F32)
    noprev_bias = jnp.where(jj < ATTN_BLOCK, NEG, 0.0).astype(F32)
    first_bias = band_bias + jnp.where(tile == 0, 1.0, 0.0).astype(F32) * noprev_bias
    lo64 = lax.broadcasted_iota(jnp.int32, (128, 128), 1) < HEAD_DIM

    def rows(start, dil):
        if dil == 1:
            return pl.ds(pl.multiple_of(start, 128), 128)
        return pl.ds(start, 128, stride=dil)

    def wide(pair, rr):
        return jnp.concatenate([pair[0][rr, :], pair[1][rr, :]], axis=1)

    def blocks(descs, dil, init):
        st = []
        for qs, klo_refs, vlo_refs, lo_s, hi_s, bias in descs:
            qr, lo, hi = rows(qs, dil), rows(lo_s, dil), rows(hi_s, dil)
            q = wide(q_refs, qr)
            k = jnp.concatenate([wide(klo_refs, lo), wide(kc_refs, hi)], axis=0).astype(BF16)
            v = [jnp.concatenate([vlo_refs[c][lo, :], vc_refs[c][hi, :]], axis=0).astype(BF16) for c in range(2)]
            q4 = jnp.concatenate([(q * qm_ref[h:h + 1, :]).astype(BF16) for h in range(GROUP_HEADS)], axis=0)
            st.append(dict(qr=qr, v=v, bias=bias, s_all=_dot_nt(q4, k)))
        if not init:
            for d in st:
                d["m_old"] = [m_refs[h][d["qr"], :] for h in range(GROUP_HEADS)]
                d["l_old"] = [l_refs[h][d["qr"], :] for h in range(GROUP_HEADS)]
        for d in st:
            d["m_new"], d["l_new"], d["alpha"], ps = [], [], [], []
            for h in range(GROUP_HEADS):
                s = d["s_all"][h * 128:(h + 1) * 128, :] + d["bias"]
                rmax = jnp.max(s, axis=-1, keepdims=True)
                m_new = jnp.broadcast_to(rmax, (128, 128)) if init else jnp.maximum(d["m_old"][h], rmax)
                p = jnp.exp2(s - jnp.concatenate([m_new, m_new], axis=1))
                rsum = jnp.sum(p, axis=-1, keepdims=True)
                if init:
                    d["l_new"].append(jnp.broadcast_to(rsum, (128, 128)))
                else:
                    alpha = jnp.exp2(d["m_old"][h] - m_new)
                    d["l_new"].append(alpha * d["l_old"][h] + rsum)
                    d["alpha"].append(alpha)
                d["m_new"].append(m_new)
                ps.append(p.astype(BF16))
            d["pv"] = []
            for c in range(2):
                pv = _dot(jnp.concatenate(ps[2 * c:2 * c + 2], axis=0), d["v"][c])
                d["pv"].append(jnp.where(lo64, pv[0:128, :], pv[128:256, :]))
        if not init:
            for d in st:
                d["u_old"] = [u_refs[c][d["qr"], :] for c in range(2)]
        for d in st:
            for h in range(GROUP_HEADS):
                m_refs[h][d["qr"], :] = d["m_new"][h]
                l_refs[h][d["qr"], :] = d["l_new"][h]
            for c in range(2):
                if init:
                    u_refs[c][d["qr"], :] = d["pv"][c]
                else:
                    a_c = jnp.where(lo64, d["alpha"][2 * c], d["alpha"][2 * c + 1])
                    u_refs[c][d["qr"], :] = d["u_old"][c] * a_c + d["pv"][c]

    def first_desc(r, span):
        return (r, kp_refs, vp_refs, ATTN_TILE - span + r, r, first_bias)

    def later_desc(r, m, span):
        base = r + (m - 1) * span
        return (base + span, kc_refs, vc_refs, base, base + span, band_bias)

    for pi, (_, dil) in enumerate(reversed(ATTN_PATTERNS)):
        span = ATTN_BLOCK * dil
        nblk = ATTN_TILE // span
        init = pi == 0
        if nblk == 1:
            def class_pair(i, carry, dil=dil, span=span, init=init):
                blocks([first_desc(2 * i, span), first_desc(2 * i + 1, span)], dil, init)
                return carry

            lax.fori_loop(0, dil // 2, class_pair, 0)
        else:
            def per_class(r, carry, dil=dil, span=span, nblk=nblk, init=init):
                blocks([first_desc(r, span), later_desc(r, 1, span)], dil, init)

                def later_pair(i, c):
                    blocks([later_desc(r, 2 * i + 2, span), later_desc(r, 2 * i + 3, span)], dil, init)
                    return c

                lax.fori_loop(0, (nblk - 2) // 2, later_pair, 0)
                return carry

            lax.fori_loop(0, dil, per_class, 0)

    lo64w = lax.broadcasted_iota(jnp.int32, (256, 128), 1) < HEAD_DIM

    def finish(c, carry):
        rr = pl.ds(pl.multiple_of(c * 256, 256), 256)
        for hlf in range(2):
            den = jnp.where(lo64w, l_refs[2 * hlf][rr, :], l_refs[2 * hlf + 1][rr, :])
            o_ref[rr, hlf * 128:(hlf + 1) * 128] = u_refs[hlf][rr, :] / den
        return carry

    lax.fori_loop(0, ATTN_TILE // 256, finish, 0)


def _attention(qkv, qmask, bsz, seq):
    nt = seq // ATTN_TILE
    cur = lambda c: (lambda b, n: (b * nt + n, c))
    prev = lambda c: (lambda b, n: (b * nt + jnp.maximum(n - 1, 0), c))
    blk = (ATTN_TILE, 128)
    col_maps = [cur(0), cur(1),
                prev(2), prev(3), cur(2), cur(3),
                prev(4), prev(5), cur(4), cur(5)]
    return pl.pallas_call(
        _attn_kernel,
        grid=(bsz, nt),
        in_specs=[pl.BlockSpec(blk, m) for m in col_maps] + [pl.BlockSpec((8, 256), lambda b, n: (0, 0))],
        out_specs=pl.BlockSpec((ATTN_TILE, 256), lambda b, n: (b * nt + n, 0)),
        out_shape=jax.ShapeDtypeStruct((bsz * seq, GROUP_WIDTH), F32),
        scratch_shapes=[pltpu.VMEM((ATTN_TILE, 128), F32)] * 10,
        compiler_params=pltpu.CompilerParams(
            dimension_semantics=("arbitrary", "arbitrary"), vmem_limit_bytes=56 * 1024 * 1024),
        name="attn",
    )(*([qkv] * 10), qmask)


SEQ_TILE = 512


def _ssd_kernel(u_ref, cw_ref, cb_ref, dtb_ref, a_ref, dsk_ref, ng_ref, ltri_ref, smask_ref,
                vm_ref, o_ref, ext_ref, hs_ref):
    ts = SEQ_TILE

    @pl.when(pl.program_id(1) == 0)
    def _():
        ext_ref[0:8, :] = jnp.zeros((8, 768), F32)
        hs_ref[...] = jnp.zeros(hs_ref.shape, F32)

    ext_ref[8:8 + ts, :] = u_ref[:, 256:1024]
    conv = cb_ref[...]
    for k in range(SSM_CONV):
        conv = conv + cw_ref[k:k + 1, :] * ext_ref[pl.ds(8 - (SSM_CONV - 1) + k, ts), :]
    ext_ref[0:8, :] = ext_ref[ts:ts + 8, :]
    ext_ref[8:8 + ts, :] = _silu(conv)

    ii = lax.broadcasted_iota(jnp.int32, (CHUNK, CHUNK), 0)
    jj = lax.broadcasted_iota(jnp.int32, (CHUNK, CHUNK), 1)
    causal = jj <= ii

    hs = hs_ref[...]
    for c in range(ts // CHUNK):
        rr = pl.ds(c * CHUNK, CHUNK)
        er = pl.ds(c * CHUNK + 8, CHUNK)
        z = u_ref[rr, 0:256]
        xs = ext_ref[er, 0:256]
        bm = ext_ref[er, 256:512]
        cm = ext_ref[er, 512:768]
        dt = _softplus(u_ref[rr, 1024:1280] + dtb_ref[...])
        da = dt * (-jnp.exp(a_ref[...]))
        acum = jnp.dot(ltri_ref[...], da, preferred_element_type=F32, precision=lax.Precision.HIGHEST)
        total = acum[CHUNK - 1:CHUNK, :]
        bmb = bm.astype(BF16)
        cmb = cm.astype(BF16)
        y = _dot(cmb, hs.astype(BF16)) * jnp.exp(acum)
        xdt = xs * dt
        acum_t = (jnp.transpose(acum[:, 0:128]), jnp.transpose(acum[:, 128:256]))
        cbs = [_dot_nt(cmb[:, g * 128:(g + 1) * 128], bmb[:, g * 128:(g + 1) * 128]) for g in range(SSM_GROUPS)]
        for h in range(GROUP_HEADS):
            col = acum[:, h * HEAD_DIM:h * HEAD_DIM + 1]
            rsel = (h % 2) * HEAD_DIM
            row = acum_t[h // 2][rsel:rsel + 1, :]
            lmat = jnp.exp(jnp.where(causal, col - row, NEG))
            mh = (cbs[h // 2] * lmat).astype(BF16)
            y = y + _dot(mh, (xdt * vm_ref[h:h + 1, :]).astype(BF16))
        xw = (xs * (jnp.exp(total - acum) * dt)).astype(BF16)
        hs = jnp.exp(total) * hs + _dot_tn(bmb, xw) * smask_ref[...]
        y = (y + dsk_ref[...] * xs) * _silu(z)
        halves = []
        for g in range(SSM_GROUPS):
            yg = y[:, g * 128:(g + 1) * 128]
            halves.append(yg * lax.rsqrt(jnp.mean(yg * yg, axis=-1, keepdims=True) + EPS))
        o_ref[rr, :] = jnp.concatenate(halves, axis=1) * ng_ref[...]
    hs_ref[...] = hs


def _ssd(ub, cw, cb, dtb, a_exp, dsk, ng, ltri, smask, vmask, bsz, seq, layer):
    ts = SEQ_TILE
    nt = seq // ts
    const = lambda b, n: (0, 0)
    lay = lambda b, n: (layer, 0, 0)
    tok = lambda b, n: (b * nt + n, 0)
    return pl.pallas_call(
        _ssd_kernel,
        grid=(bsz, nt),
        in_specs=[
            pl.BlockSpec((ts, 1280), tok),
            pl.BlockSpec((None, 8, 768), lay), pl.BlockSpec((None, 1, 768), lay),
            pl.BlockSpec((None, 1, 256), lay), pl.BlockSpec((None, 1, 256), lay),
            pl.BlockSpec((None, 1, 256), lay), pl.BlockSpec((None, 1, 256), lay),
            pl.BlockSpec((CHUNK, CHUNK), const), pl.BlockSpec((256, 256), const),
            pl.BlockSpec((8, 256), const),
        ],
        out_specs=pl.BlockSpec((ts, 256), tok),
        out_shape=jax.ShapeDtypeStruct((bsz * seq, GROUP_WIDTH), F32),
        scratch_shapes=[pltpu.VMEM((ts + 8, 768), F32), pltpu.VMEM((256, 256), F32)],
        compiler_params=pltpu.CompilerParams(dimension_semantics=("arbitrary", "arbitrary")),
        name="ssd",
    )(ub, cw, cb, dtb, a_exp, dsk, ng, ltri, smask, vmask)


def _ret_kernel(u_ref, dmat_ref, zeta_ref, xi_ref, gch_ref, bd_ref, rmask_ref, qm_ref, vm_ref,
                o_ref, r_ref):
    @pl.when(pl.program_id(1) == 0)
    def _():
        r_ref[...] = jnp.zeros(r_ref.shape, F32)

    r = r_ref[...]
    for c in range(SEQ_TILE // CHUNK):
        rr = pl.ds(c * CHUNK, CHUNK)
        q = u_ref[rr, 0:256]
        k = u_ref[rr, 256:512]
        v = u_ref[rr, 512:768]
        g = u_ref[rr, 768:1024]
        kb = k.astype(BF16)
        y = _dot(q.astype(BF16), r.astype(BF16)) * xi_ref[...]
        for h in range(GROUP_HEADS):
            s = _dot_nt((q * qm_ref[h:h + 1, :]).astype(BF16), kb) * dmat_ref[h]
            y = y + _dot(s.astype(BF16), (v * vm_ref[h:h + 1, :]).astype(BF16))
        ss = _dot((y * y).astype(BF16), bd_ref[...])
        o_ref[rr, :] = y * lax.rsqrt(ss * (1.0 / HEAD_DIM) + EPS) * _silu(g)
        kz = (k * zeta_ref[...]).astype(BF16)
        r = gch_ref[...] * r + _dot_tn(kz, v.astype(BF16)) * rmask_ref[...]
    r_ref[...] = r


def _retention(uc, dmat, zeta, xi, gch, bdn, rmask, qmask, vmask, bsz, seq):
    ts = SEQ_TILE
    nt = seq // ts
    const = lambda b, n: (0, 0)
    tok = lambda b, n: (b * nt + n, 0)
    return pl.pallas_call(
        _ret_kernel,
        grid=(bsz, nt),
        in_specs=[
            pl.BlockSpec((ts, 1024), tok),
            pl.BlockSpec((GROUP_HEADS, CHUNK, CHUNK), lambda b, n: (0, 0, 0)),
            pl.BlockSpec((CHUNK, 256), const), pl.BlockSpec((CHUNK, 256), const),
            pl.BlockSpec((1, 256), const), pl.BlockSpec((256, 256), const),
            pl.BlockSpec((256, 256), const),
            pl.BlockSpec((8, 256), const), pl.BlockSpec((8, 256), const),
        ],
        out_specs=pl.BlockSpec((ts, 256), tok),
        out_shape=jax.ShapeDtypeStruct((bsz * seq, GROUP_WIDTH), F32),
        scratch_shapes=[pltpu.VMEM((256, 256), F32)],
        compiler_params=pltpu.CompilerParams(dimension_semantics=("arbitrary", "arbitrary")),
        name="retention",
    )(uc, dmat, zeta, xi, gch, bdn, rmask, qmask, vmask)


CONF_HALO = 32
CONF_ROWS = 64


def _conf_kernel(u_ref, w_ref, b_ref, lg_ref, lb_ref, o_ref, ext_ref):
    ts = SEQ_TILE

    @pl.when(pl.program_id(1) == 0)
    def _():
        ext_ref[0:CONF_HALO, :] = jnp.zeros((CONF_HALO, 256), F32)
        ext_ref[CONF_HALO + ts:CONF_HALO + ts + 8, :] = jnp.zeros((8, 256), F32)

    ext_ref[CONF_HALO:CONF_HALO + ts, :] = u_ref[:, 0:256] * _sigmoid(u_ref[:, 256:512])

    nwin = CONF_ROWS + CONF_HALO + 8
    for c in range(ts // CONF_ROWS):
        r0 = c * CONF_ROWS
        win = ext_ref[pl.ds(r0, nwin), :]
        acc = jnp.zeros((CONF_ROWS, 256), F32) + b_ref[...]
        for off in range(8):
            shifted = win if off == 0 else pltpu.roll(win, nwin - off, axis=0)
            for sh in range(CONF_HALO - CONF_KERNEL + 1, CONF_HALO + 1):
                if sh % 8 == off:
                    k = sh - (CONF_HALO - CONF_KERNEL + 1)
                    acc = acc + w_ref[k:k + 1, :] * shifted[sh - off:sh - off + CONF_ROWS, :]
        mu = jnp.mean(acc, axis=-1, keepdims=True)
        xc = acc - mu
        var = jnp.mean(xc * xc, axis=-1, keepdims=True)
        o_ref[pl.ds(r0, CONF_ROWS), :] = _silu(xc * lax.rsqrt(var + EPS) * lg_ref[...] + lb_ref[...])
    ext_ref[0:CONF_HALO, :] = ext_ref[ts:ts + CONF_HALO, :]


def _conformer(ud, w, b, lg, lb, bsz, seq, layer):
    ts = SEQ_TILE
    nt = seq // ts
    lay = lambda b_, n: (layer, 0, 0)
    tok = lambda b_, n: (b_ * nt + n, 0)
    return pl.pallas_call(
        _conf_kernel,
        grid=(bsz, nt),
        in_specs=[
            pl.BlockSpec((ts, 512), tok),
            pl.BlockSpec((None, 32, 256), lay), pl.BlockSpec((None, 1, 256), lay),
            pl.BlockSpec((None, 1, 256), lay), pl.BlockSpec((None, 1, 256), lay),
        ],
        out_specs=pl.BlockSpec((ts, 256), tok),
        out_shape=jax.ShapeDtypeStruct((bsz * seq, GROUP_WIDTH), F32),
        scratch_shapes=[pltpu.VMEM((ts + CONF_HALO + 8, 256), F32)],
        compiler_params=pltpu.CompilerParams(dimension_semantics=("arbitrary", "arbitrary")),
        name="conformer",
    )(ud, w, b, lg, lb)


FF_CHUNK = 1024


def _rms_rows(x, g):
    return x * lax.rsqrt(jnp.mean(x * x, axis=-1, keepdims=True) + EPS) * g


def _ffn_kernel(x_ref, ya_ref, yb_ref, yc_ref, yd_ref, p_ref, wout_ref, gf_ref, wup_ref, wdn_ref,
                gp_ref, wple_ref, wgate_ref, o_ref, hb_ref):
    tm = x_ref.shape[0]
    halves = [pl.ds(0, tm // 2), pl.ds(tm // 2, tm // 2)]
    x1 = []
    for rr in halves:
        acc = x_ref[rr, :]
        for m, y_ref in enumerate((ya_ref, yb_ref, yc_ref, yd_ref)):
            acc = acc + _dot(y_ref[rr, :].astype(BF16), wout_ref[m * 256:(m + 1) * 256, :])
        hb_ref[rr, :] = _rms_rows(acc, gf_ref[...]).astype(BF16)
        x1.append(acc)

    def down(rr, c):
        up = jnp.maximum(_dot(hb_ref[rr, :], wup_ref[:, c * FF_CHUNK:(c + 1) * FF_CHUNK]), 0.0)
        return _dot((up * up).astype(BF16), wdn_ref[c * FF_CHUNK:(c + 1) * FF_CHUNK, :])

    ffs = [down(rr, 0) for rr in halves]
    for c in range(1, D_FF // FF_CHUNK):
        ffs = [ff + down(rr, c) for ff, rr in zip(ffs, halves)]
    for rr, acc, ff in zip(halves, x1, ffs):
        x2 = acc + ff
        gate = _sigmoid(_dot(_rms_rows(x2, gp_ref[...]).astype(BF16), wgate_ref[...]))
        o_ref[rr, :] = x2 + _dot(p_ref[rr, :].astype(BF16), wple_ref[...]) * gate


def _ffn(x2d, ya, yb, yc, yd, p_all, wout, gf, wup, wdn, gp, wple, wgate, tm, layer):
    t = x2d.shape[0]
    tok = lambda i: (i, 0)
    lay = lambda i: (layer, 0, 0)
    once = dict(pipeline_mode=pl.Buffered(1))
    return pl.pallas_call(
        _ffn_kernel,
        grid=(t // tm,),
        in_specs=[
            pl.BlockSpec((tm, D_MODEL), tok),
            pl.BlockSpec((tm, 256), tok), pl.BlockSpec((tm, 256), tok),
            pl.BlockSpec((tm, 256), tok), pl.BlockSpec((tm, 256), tok),
            pl.BlockSpec((None, tm, PLE_DIM), lambda i: (layer, i, 0)),
            pl.BlockSpec((None, D_MODEL, D_MODEL), lay, **once),
            pl.BlockSpec((None, 1, D_MODEL), lay),
            pl.BlockSpec((None, D_MODEL, D_FF), lay, **once),
            pl.BlockSpec((None, D_FF, D_MODEL), lay, **once),
            pl.BlockSpec((None, 1, D_MODEL), lay),
            pl.BlockSpec((None, PLE_DIM, D_MODEL), lay, **once),
            pl.BlockSpec((None, D_MODEL, D_MODEL), lay, **once),
        ],
        out_specs=pl.BlockSpec((tm, D_MODEL), tok),
        out_shape=jax.ShapeDtypeStruct((t, D_MODEL), F32),
        scratch_shapes=[pltpu.VMEM((tm, D_MODEL), BF16)],
        compiler_params=pltpu.CompilerParams(
            dimension_semantics=("arbitrary",), vmem_limit_bytes=56 * 1024 * 1024),
        name="ffn",
    )(x2d, ya, yb, yc, yd, p_all, wout, gf, wup, wdn, gp, wple, wgate)


def _head_mask(head_of_lane):
    m = np.zeros((8, GROUP_WIDTH), np.float32)
    for h in range(GROUP_HEADS):
        m[h] = head_of_lane == h
    return m


def _tables(seq):
    pos = jnp.arange(seq, dtype=F32)
    ang_a = ROPE_THETA ** (-jnp.arange(0, HEAD_DIM, 2, dtype=F32) / HEAD_DIM)
    ang_c = 1.0 / (10000.0 ** jnp.linspace(0.0, 1.0, HEAD_DIM // 2, dtype=F32))

    def cs(ang):
        a = pos[:, None] * ang[None, :]
        return jnp.tile(jnp.cos(a), (1, 4)), jnp.tile(jnp.sin(a), (1, 4))

    cosa, sina = cs(ang_a)
    cosc, sinc = cs(ang_c)
    log_g = jnp.log(1.0 - 2.0 ** (-5.0 - jnp.arange(GROUP_HEADS, dtype=F32)))
    idx = jnp.arange(CHUNK, dtype=F32)
    dist = idx[:, None] - idx[None, :]
    dmat = jnp.where((dist >= 0)[None], jnp.exp(jnp.maximum(dist, 0.0)[None] * log_g[:, None, None]), 0.0)
    zeta = jnp.exp((CHUNK - 1 - idx)[:, None] * log_g)[:, _HEAD_PERM]
    xi = jnp.exp((idx + 1.0)[:, None] * log_g)[:, _HEAD_NAT]
    gch = jnp.exp(CHUNK * log_g)[_HEAD_NAT][None, :]
    return dict(
        cosa=cosa, sina=sina, cosc=cosc, sinc=sinc, dmat=dmat, zeta=zeta, xi=xi, gch=gch,
        bd_perm=jnp.asarray(_HEAD_PERM[:, None] == _HEAD_PERM[None, :], BF16),
        bd_nat=jnp.asarray(_HEAD_NAT[:, None] == _HEAD_NAT[None, :], BF16),
        rmask=jnp.asarray(_HEAD_PERM[:, None] == _HEAD_NAT[None, :], F32),
        smask=jnp.asarray((_LANE[:, None] // SSM_STATE) == (_HEAD_NAT[None, :] // (GROUP_HEADS // SSM_GROUPS)), F32),
        qmask=jnp.asarray(_head_mask(_HEAD_PERM)),
        vmask=jnp.asarray(_head_mask(_HEAD_NAT)),
        ltri=jnp.asarray(np.tril(np.ones((CHUNK, CHUNK), np.float32))),
    )


def _build_w_cat(w_in):
    w = w_in
    depth = w.shape[0]

    def nat(c0):
        return w[:, :, c0:c0 + 256]

    def perm(c0):
        blk = w[:, :, c0:c0 + 256].reshape(depth, D_MODEL, GROUP_HEADS, 2, 32)
        return blk.transpose(0, 1, 3, 2, 4).reshape(depth, D_MODEL, 256)

    dt = jnp.repeat(w[:, :, _BDT:_BDT + GROUP_HEADS], HEAD_DIM, axis=2)
    return jnp.concatenate([
        perm(_A0), perm(_A0 + 256), nat(_A0 + 512),
        nat(_BZ), nat(_BX), nat(_BB), nat(_BC), dt,
        perm(_C0), perm(_C0 + 256), nat(_C0 + 512), nat(_C0 + 768),
        nat(_D0), nat(_D0 + 256)], axis=2).astype(BF16)


def kernel(x, p, norm_mix, w_in, attn_q_norm, attn_k_norm, ssm_conv_w, ssm_conv_b, ssm_dt_bias,
           ssm_a_log, ssm_d, ssm_norm, conv_dw_w, conv_dw_b, conv_ln_g, conv_ln_b, w_out,
           norm_ffn, w_up, w_down, norm_ple, w_ple, w_ple_gate):
    bsz, seq, _ = x.shape
    depth = w_in.shape[0]
    tm = 512
    tb = _tables(seq)
    row = lambda a: a[:, None, :]
    w_cat = _build_w_cat(w_in)
    g_mix, g_ffn, g_ple = row(norm_mix), row(norm_ffn), row(norm_ple)
    gq, gk = row(attn_q_norm[:, _PERM % HEAD_DIM]), row(attn_k_norm[:, _PERM % HEAD_DIM])
    cw = jnp.pad(ssm_conv_w, ((0, 0), (0, 8 - SSM_CONV), (0, 0)))
    dw = jnp.pad(conv_dw_w, ((0, 0), (0, 32 - CONF_KERNEL), (0, 0)))
    dtb, alog, dsk = row(ssm_dt_bias[:, _HEAD_NAT]), row(ssm_a_log[:, _HEAD_NAT]), row(ssm_d[:, _HEAD_NAT])
    wout, wup, wdn = w_out.astype(BF16), w_up.astype(BF16), w_down.astype(BF16)
    wple, wgate = w_ple.astype(BF16), w_ple_gate.astype(BF16)
    p_all = p.reshape(depth, bsz * seq, PLE_DIM)
    x2d = x.reshape(bsz * seq, D_MODEL)
    for i in range(depth):
        oa, ob, oc, od = _inproj(x2d, g_mix, w_cat, tb["cosa"], tb["sina"], tb["cosc"], tb["sinc"],
                                 gq, gk, tb["bd_perm"], seq, tm, i)
        ya = _attention(oa, tb["qmask"], bsz, seq)
        yb = _ssd(ob, cw, row(ssm_conv_b), dtb, alog, dsk, row(ssm_norm), tb["ltri"], tb["smask"],
                  tb["vmask"], bsz, seq, i)
        yc = _retention(oc, tb["dmat"], tb["zeta"], tb["xi"], tb["gch"], tb["bd_nat"], tb["rmask"],
                        tb["qmask"], tb["vmask"], bsz, seq)
        yd = _conformer(od, dw, row(conv_dw_b), row(conv_ln_g), row(conv_ln_b), bsz, seq, i)
        x2d = _ffn(x2d, ya, yb, yc, yd, p_all, wout, g_ffn, wup, wdn, g_ple, wple, wgate, tm, i)
    return x2d.reshape(bsz, seq, D_MODEL)
```

```python
import functools
import math

import numpy as np
import jax
import jax.numpy as jnp
from jax import lax
from jax.experimental import pallas as pl
from jax.experimental.pallas import tpu as pltpu

F32 = jnp.float32
BF16 = jnp.bfloat16

D_MODEL = 1024
GROUP_WIDTH = 256
GROUP_HEADS = 4
HEAD_DIM = 64
EPS = 1e-6
ATTN_PATTERNS = ((128, 1), (512, 4), (2048, 16))
ATTN_BLOCK = 128
ROPE_THETA = 10000.0
SSM_STATE = 128
SSM_GROUPS = 2
SSM_CONV = 4
CHUNK = 128
CONF_KERNEL = 31
D_FF = 4 * D_MODEL
PLE_DIM = 256
NEG = -1e30
LOG2E = 1.4426950408889634

_A0 = 0
_B0 = 3 * GROUP_WIDTH
_BZ, _BX, _BB, _BC = _B0, _B0 + 256, _B0 + 512, _B0 + 768
_BDT = _B0 + 1024
_C0 = _BDT + GROUP_HEADS
_D0 = _C0 + 4 * GROUP_WIDTH
IN_COLS = _D0 + 2 * GROUP_WIDTH

_LANE = np.arange(GROUP_WIDTH)
_PERM = ((_LANE % 128) // 32) * HEAD_DIM + (_LANE // 128) * 32 + (_LANE % 32)
_HEAD_PERM = (_LANE % 128) // 32
_HEAD_NAT = _LANE // HEAD_DIM

N_PROJ = 14 * GROUP_WIDTH


def _proj_columns():
    nat = np.arange(GROUP_WIDTH)
    cols = [
        _A0 + _PERM, _A0 + 256 + _PERM, _A0 + 512 + nat,
        _BZ + nat, _BX + nat, _BB + nat, _BC + nat, _BDT + _HEAD_NAT,
        _C0 + _PERM, _C0 + 256 + _PERM, _C0 + 512 + nat, _C0 + 768 + nat,
        _D0 + nat, _D0 + 256 + nat,
    ]
    return np.concatenate(cols).astype(np.int32)


_PROJ_COLS = _proj_columns()


def _sigmoid(x):
    return 1.0 / (1.0 + jnp.exp(-x))


def _silu(x):
    return x * _sigmoid(x)


def _softplus(x):
    return jnp.maximum(x, 0.0) + jnp.log(1.0 + jnp.exp(-jnp.abs(x)))


def _dot(a, b):
    return jnp.dot(a, b, preferred_element_type=F32)


def _dot_nt(a, b):
    return lax.dot_general(a, b, (((1,), (1,)), ((), ())), preferred_element_type=F32)


def _dot_tn(a, b):
    return lax.dot_general(a, b, (((0,), (0,)), ((), ())), preferred_element_type=F32)


def _inproj_kernel(x0_ref, xn_ref, g_ref, w_ref, cosa_ref, sina_ref, cosc_ref, sinc_ref,
                   gq_ref, gk_ref, bd_ref, oa_ref, ob_ref, oc_ref, od_ref, hb0_ref, hb1_ref):
    step = pl.program_id(0)

    def norm_into(src_ref, dst_ref):
        x = src_ref[...]
        h = x * lax.rsqrt(jnp.mean(x * x, axis=-1, keepdims=True) + EPS) * g_ref[...]
        dst_ref[...] = h.astype(BF16)

    @pl.when(step == 0)
    def _():
        norm_into(x0_ref, hb0_ref)

    @pl.when(step % 2 == 0)
    def _():
        norm_into(xn_ref, hb1_ref)
        _inproj_body(hb0_ref, w_ref, cosa_ref, sina_ref, cosc_ref, sinc_ref, gq_ref, gk_ref, bd_ref,
                     oa_ref, ob_ref, oc_ref, od_ref)

    @pl.when(step % 2 == 1)
    def _():
        norm_into(xn_ref, hb0_ref)
        _inproj_body(hb1_ref, w_ref, cosa_ref, sina_ref, cosc_ref, sinc_ref, gq_ref, gk_ref, bd_ref,
                     oa_ref, ob_ref, oc_ref, od_ref)


def _inproj_body(hb_ref, w_ref, cosa_ref, sina_ref, cosc_ref, sinc_ref, gq_ref, gk_ref, bd_ref,
                 oa_ref, ob_ref, oc_ref, od_ref):
    def mm(j):
        return _dot(hb_ref[...], w_ref[:, j * 256:(j + 1) * 256])

    def rot(t, cos, sin):
        t1, t2 = t[:, :128], t[:, 128:]
        return jnp.concatenate([t1 * cos - t2 * sin, t2 * cos + t1 * sin], axis=1)

    def headnorm(t, gain):
        ss = _dot((t * t).astype(BF16), bd_ref[...])
        return t * lax.rsqrt(ss * (1.0 / HEAD_DIM) + EPS) * gain

    cosa, sina = cosa_ref[...], sina_ref[...]
    cosc, sinc = cosc_ref[...], sinc_ref[...]
    scale = HEAD_DIM ** -0.5
    oa_ref[:, 0:256] = mm(0)
    oa_ref[:, 256:512] = mm(1)
    oa_ref[:, 512:768] = mm(2)
    for j in range(5):
        ob_ref[:, j * 256:(j + 1) * 256] = mm(3 + j)
    oa_ref[:, 0:256] = rot(headnorm(oa_ref[:, 0:256], gq_ref[...]), cosa, sina) * (scale * LOG2E)
    oa_ref[:, 256:512] = rot(headnorm(oa_ref[:, 256:512], gk_ref[...]), cosa, sina)
    oc_ref[:, 0:256] = rot(mm(8), cosc, sinc)
    oc_ref[:, 256:512] = rot(mm(9), cosc, sinc) * scale
    oc_ref[:, 512:768] = mm(10)
    oc_ref[:, 768:1024] = mm(11)
    od_ref[:, 0:256] = mm(12)
    od_ref[:, 256:512] = mm(13)


def _inproj(x2d, g, w_cat, cosa, sina, cosc, sinc, gq, gk, bd, seq, tm, layer):
    t = x2d.shape[0]
    nseq = seq // tm
    const = lambda i: (0, 0)
    lay = lambda i: (layer, 0, 0)
    tok = lambda i: (i, 0)
    pos = lambda i: (i % nseq, 0)
    nxt = lambda i: (jnp.minimum(i + 1, t // tm - 1), 0)
    return pl.pallas_call(
        _inproj_kernel,
        grid=(t // tm,),
        in_specs=[
            pl.BlockSpec((tm, D_MODEL), const), pl.BlockSpec((tm, D_MODEL), nxt),
            pl.BlockSpec((None, 1, D_MODEL), lay),
            pl.BlockSpec((None, D_MODEL, N_PROJ), lay),
            pl.BlockSpec((tm, 128), pos), pl.BlockSpec((tm, 128), pos),
            pl.BlockSpec((tm, 128), pos), pl.BlockSpec((tm, 128), pos),
            pl.BlockSpec((None, 1, 256), lay), pl.BlockSpec((None, 1, 256), lay),
            pl.BlockSpec((256, 256), const),
        ],
        out_specs=[
            pl.BlockSpec((tm, 768), tok), pl.BlockSpec((tm, 1280), tok),
            pl.BlockSpec((tm, 1024), tok), pl.BlockSpec((tm, 512), tok),
        ],
        out_shape=[
            jax.ShapeDtypeStruct((t, 768), F32), jax.ShapeDtypeStruct((t, 1280), F32),
            jax.ShapeDtypeStruct((t, 1024), F32), jax.ShapeDtypeStruct((t, 512), F32),
        ],
        scratch_shapes=[pltpu.VMEM((tm, D_MODEL), BF16), pltpu.VMEM((tm, D_MODEL), BF16)],
        compiler_params=pltpu.CompilerParams(
            dimension_semantics=("arbitrary",), vmem_limit_bytes=56 * 1024 * 1024),
        name="inproj",
    )(x2d, x2d, g, w_cat, cosa, sina, cosc, sinc, gq, gk, bd)


ATTN_TILE = 2048


def _attn_kernel(*refs):
    q_refs, kp_refs, kc_refs, vp_refs, vc_refs = (refs[2 * i:2 * i + 2] for i in range(5))
    qm_ref, o_ref = refs[10], refs[11]
    m_refs, l_refs, u_refs = refs[12:16], refs[16:20], refs[20:22]
    tile = pl.program_id(1)

    ii = lax.broadcasted_iota(jnp.int32, (128, 256), 0)
    jj = lax.broadcasted_iota(jnp.int32, (128, 256), 1)
    band_bias = jnp.where((jj >= ii) & (jj <= ii + ATTN_BLOCK), 0.0, NEG).astype(F32)
    noprev_bias = jnp.where(jj < ATTN_BLOCK, NEG, 0.0).astype(F32)
    first_bias = band_bias + jnp.where(tile == 0, 1.0, 0.0).astype(F32) * noprev_bias
    lo64 = lax.broadcasted_iota(jnp.int32, (128, 128), 1) < HEAD_DIM

    def rows(start, dil):
        if dil == 1:
            return pl.ds(pl.multiple_of(start, 128), 128)
        return pl.ds(start, 128, stride=dil)

    def wide(pair, rr):
        return jnp.concatenate([pair[0][rr, :], pair[1][rr, :]], axis=1)

    def blocks(descs, dil, init):
        st = []
        for qs, klo_refs, vlo_refs, lo_s, hi_s, bias in descs:
            qr, lo, hi = rows(qs, dil), rows(lo_s, dil), rows(hi_s, dil)
            q = wide(q_refs, qr)
            k = jnp.concatenate([wide(klo_refs, lo), wide(kc_refs, hi)], axis=0).astype(BF16)
            v = [jnp.concatenate([vlo_refs[c][lo, :], vc_refs[c][hi, :]], axis=0).astype(BF16) for c in range(2)]
            q4 = jnp.concatenate([(q * qm_ref[h:h + 1, :]).astype(BF16) for h in range(GROUP_HEADS)], axis=0)
            st.append(dict(qr=qr, v=v, bias=bias, s_all=_dot_nt(q4, k)))
        if not init:
            for d in st:
                d["m_old"] = [m_refs[h][d["qr"], :] for h in range(GROUP_HEADS)]
                d["l_old"] = [l_refs[h][d["qr"], :] for h in range(GROUP_HEADS)]
        for d in st:
            d["m_new"], d["l_new"], d["alpha"], ps = [], [], [], []
            for h in range(GROUP_HEADS):
                s = d["s_all"][h * 128:(h + 1) * 128, :] + d["bias"]
                rmax = jnp.max(s, axis=-1, keepdims=True)
                m_new = jnp.broadcast_to(rmax, (128, 128)) if init else jnp.maximum(d["m_old"][h], rmax)
                p = jnp.exp2(s - jnp.concatenate([m_new, m_new], axis=1))
                rsum = jnp.sum(p, axis=-1, keepdims=True)
                if init:
                    d["l_new"].append(jnp.broadcast_to(rsum, (128, 128)))
                else:
                    alpha = jnp.exp2(d["m_old"][h] - m_new)
                    d["l_new"].append(alpha * d["l_old"][h] + rsum)
                    d["alpha"].append(alpha)
                d["m_new"].append(m_new)
                ps.append(p.astype(BF16))
            d["pv"] = []
            for c in range(2):
                pv = _dot(jnp.concatenate(ps[2 * c:2 * c + 2], axis=0), d["v"][c])
                d["pv"].append(jnp.where(lo64, pv[0:128, :], pv[128:256, :]))
        if not init:
            for d in st:
                d["u_old"] = [u_refs[c][d["qr"], :] for c in range(2)]
        for d in st:
            for h in range(GROUP_HEADS):
                m_refs[h][d["qr"], :] = d["m_new"][h]
                l_refs[h][d["qr"], :] = d["l_new"][h]
            for c in range(2):
                if init:
                    u_refs[c][d["qr"], :] = d["pv"][c]
                else:
                    a_c = jnp.where(lo64, d["alpha"][2 * c], d["alpha"][2 * c + 1])
                    u_refs[c][d["qr"], :] = d["u_old"][c] * a_c + d["pv"][c]

    def first_desc(r, span):
        return (r, kp_refs, vp_refs, ATTN_TILE - span + r, r, first_bias)

    def later_desc(r, m, span):
        base = r + (m - 1) * span
        return (base + span, kc_refs, vc_refs, base, base + span, band_bias)

    for pi, (_, dil) in enumerate(reversed(ATTN_PATTERNS)):
        span = ATTN_BLOCK * dil
        nblk = ATTN_TILE // span
        init = pi == 0
        if nblk == 1:
            def class_pair(i, carry, dil=dil, span=span, init=init):
                blocks([first_desc(2 * i, span), first_desc(2 * i + 1, span)], dil, init)
                return carry

            lax.fori_loop(0, dil // 2, class_pair, 0)
        else:
            def per_class(r, carry, dil=dil, span=span, nblk=nblk, init=init):
                blocks([first_desc(r, span), later_desc(r, 1, span)], dil, init)

                def later_pair(i, c):
                    blocks([later_desc(r, 2 * i + 2, span), later_desc(r, 2 * i + 3, span)], dil, init)
                    return c

                lax.fori_loop(0, (nblk - 2) // 2, later_pair, 0)
                return carry

            lax.fori_loop(0, dil, per_class, 0)

    lo64w = lax.broadcasted_iota(jnp.int32, (256, 128), 1) < HEAD_DIM

    def finish(c, carry):
        rr = pl.ds(pl.multiple_of(c * 256, 256), 256)
        for hlf in range(2):
            den = jnp.where(lo64w, l_refs[2 * hlf][rr, :], l_refs[2 * hlf + 1][rr, :])
            o_ref[rr, hlf * 128:(hlf + 1) * 128] = u_refs[hlf][rr, :] / den
        return carry

    lax.fori_loop(0, ATTN_TILE // 256, finish, 0)


def _attention(qkv, qmask, bsz, seq):
    nt = seq // ATTN_TILE
    cur = lambda c: (lambda b, n: (b * nt + n, c))
    prev = lambda c: (lambda b, n: (b * nt + jnp.maximum(n - 1, 0), c))
    blk = (ATTN_TILE, 128)
    col_maps = [cur(0), cur(1),
                prev(2), prev(3), cur(2), cur(3),
                prev(4), prev(5), cur(4), cur(5)]
    return pl.pallas_call(
        _attn_kernel,
        grid=(bsz, nt),
        in_specs=[pl.BlockSpec(blk, m) for m in col_maps] + [pl.BlockSpec((8, 256), lambda b, n: (0, 0))],
        out_specs=pl.BlockSpec((ATTN_TILE, 256), lambda b, n: (b * nt + n, 0)),
        out_shape=jax.ShapeDtypeStruct((bsz * seq, GROUP_WIDTH), F32),
        scratch_shapes=[pltpu.VMEM((ATTN_TILE, 128), F32)] * 10,
        compiler_params=pltpu.CompilerParams(
            dimension_semantics=("arbitrary", "arbitrary"), vmem_limit_bytes=56 * 1024 * 1024),
        name="attn",
    )(*([qkv] * 10), qmask)


SEQ_TILE = 1024
SSM_ROWS = 64


def _ssd_kernel(u_ref, cw_ref, cb_ref, dtb_ref, a_ref, dsk_ref, ng_ref, ltri_ref, smask_ref,
                vm_ref, o_ref, ext_ref, act_ref, hs_ref):
    ts = SEQ_TILE

    @pl.when(pl.program_id(1) == 0)
    def _():
        ext_ref[0:8, :] = jnp.zeros((8, 768), F32)
        hs_ref[...] = jnp.zeros(hs_ref.shape, F32)

    ext_ref[8:8 + ts, :] = u_ref[:, 256:1024]
    for c in range(ts // SSM_ROWS):
        for gc in range(3):
            cols = slice(gc * 256, (gc + 1) * 256)
            win = ext_ref[pl.ds(c * SSM_ROWS, SSM_ROWS + 8), cols]
            acc = cb_ref[:, cols] + cw_ref[SSM_CONV - 1:SSM_CONV, cols] * win[8:8 + SSM_ROWS, :]
            for sh in range(1, SSM_CONV):
                k = SSM_CONV - 1 - sh
                acc = acc + cw_ref[k:k + 1, cols] * pltpu.roll(win, sh, axis=0)[8:8 + SSM_ROWS, :]
            act_ref[pl.ds(c * SSM_ROWS, SSM_ROWS), cols] = _silu(acc)
    ext_ref[0:8, :] = ext_ref[ts:ts + 8, :]

    ii = lax.broadcasted_iota(jnp.int32, (CHUNK, CHUNK), 0)
    jj = lax.broadcasted_iota(jnp.int32, (CHUNK, CHUNK), 1)
    causal = jj <= ii
    ltri = ltri_ref[...]

    hs = hs_ref[...]
    for c in range(ts // CHUNK):
        rr = pl.ds(c * CHUNK, CHUNK)
        z = u_ref[rr, 0:256]
        xs = act_ref[rr, 0:256]
        bm = act_ref[rr, 256:512]
        cm = act_ref[rr, 512:768]
        dt = _softplus(u_ref[rr, 1024:1280] + dtb_ref[...])
        da = dt * (-jnp.exp(a_ref[...]))
        da_hi = da.astype(BF16)
        rem = da - da_hi.astype(F32)
        da_mid = rem.astype(BF16)
        da_lo = (rem - da_mid.astype(F32)).astype(BF16)
        acum = _dot(ltri, da_hi) + _dot(ltri, da_mid) + _dot(ltri, da_lo)
        total = acum[CHUNK - 1:CHUNK, :]
        bmb = bm.astype(BF16)
        cmb = cm.astype(BF16)
        xdt = xs * dt
        acum_t = (jnp.transpose(acum[:, 0:128]), jnp.transpose(acum[:, 128:256]))
        cbs = [_dot_nt(cmb[:, g * 128:(g + 1) * 128], bmb[:, g * 128:(g + 1) * 128]) for g in range(SSM_GROUPS)]
        mhs = []
        for h in range(GROUP_HEADS):
            col = acum[:, h * HEAD_DIM:h * HEAD_DIM + 1]
            rsel = (h % 2) * HEAD_DIM
            row = acum_t[h // 2][rsel:rsel + 1, :]
            lmat = jnp.exp(jnp.where(causal, col - row, NEG))
            mhs.append((cbs[h // 2] * lmat).astype(BF16))
        xdt4 = jnp.concatenate([(xdt * vm_ref[h:h + 1, :]).astype(BF16) for h in range(GROUP_HEADS)], axis=0)
        y = _dot(cmb, hs.astype(BF16)) * jnp.exp(acum) + _dot(jnp.concatenate(mhs, axis=1), xdt4)
        xw = (xs * (jnp.exp(total - acum) * dt)).astype(BF16)
        hs = jnp.exp(total) * hs + _dot_tn(bmb, xw) * smask_ref[...]
        y = (y + dsk_ref[...] * xs) * _silu(z)
        halves = []
        for g in range(SSM_GROUPS):
            yg = y[:, g * 128:(g + 1) * 128]
            halves.append(yg * lax.rsqrt(jnp.mean(yg * yg, axis=-1, keepdims=True) + EPS))
        o_ref[rr, :] = jnp.concatenate(halves, axis=1) * ng_ref[...]
    hs_ref[...] = hs


def _ssd(ub, cw, cb, dtb, a_exp, dsk, ng, ltri, smask, vmask, bsz, seq, layer):
    ts = SEQ_TILE
    nt = seq // ts
    const = lambda b, n: (0, 0)
    lay = lambda b, n: (layer, 0, 0)
    tok = lambda b, n: (b * nt + n, 0)
    return pl.pallas_call(
        _ssd_kernel,
        grid=(bsz, nt),
        in_specs=[
            pl.BlockSpec((ts, 1280), tok),
            pl.BlockSpec((None, 8, 768), lay), pl.BlockSpec((None, 1, 768), lay),
            pl.BlockSpec((None, 1, 256), lay), pl.BlockSpec((None, 1, 256), lay),
            pl.BlockSpec((None, 1, 256), lay), pl.BlockSpec((None, 1, 256), lay),
            pl.BlockSpec((CHUNK, CHUNK), const), pl.BlockSpec((256, 256), const),
            pl.BlockSpec((8, 256), const),
        ],
        out_specs=pl.BlockSpec((ts, 256), tok),
        out_shape=jax.ShapeDtypeStruct((bsz * seq, GROUP_WIDTH), F32),
        scratch_shapes=[pltpu.VMEM((ts + 8, 768), F32), pltpu.VMEM((ts, 768), F32), pltpu.VMEM((256, 256), F32)],
        compiler_params=pltpu.CompilerParams(dimension_semantics=("arbitrary", "arbitrary")),
        name="ssd",
    )(ub, cw, cb, dtb, a_exp, dsk, ng, ltri, smask, vmask)


def _ret_kernel(u_ref, dmat_ref, zeta_ref, xi_ref, gch_ref, bd_ref, rmask_ref, qm_ref, vm_ref,
                o_ref, r_ref):
    @pl.when(pl.program_id(1) == 0)
    def _():
        r_ref[...] = jnp.zeros(r_ref.shape, F32)

    r = r_ref[...]
    for c in range(SEQ_TILE // CHUNK):
        rr = pl.ds(c * CHUNK, CHUNK)
        q = u_ref[rr, 0:256]
        k = u_ref[rr, 256:512]
        v = u_ref[rr, 512:768]
        g = u_ref[rr, 768:1024]
        kb = k.astype(BF16)
        q4 = jnp.concatenate([(q * qm_ref[h:h + 1, :]).astype(BF16) for h in range(GROUP_HEADS)], axis=0)
        s4 = _dot_nt(q4, kb) * dmat_ref[...]
        s_cat = jnp.concatenate([s4[h * CHUNK:(h + 1) * CHUNK, :] for h in range(GROUP_HEADS)], axis=1)
        v4 = jnp.concatenate([(v * vm_ref[h:h + 1, :]).astype(BF16) for h in range(GROUP_HEADS)], axis=0)
        y = _dot(q.astype(BF16), r.astype(BF16)) * xi_ref[...] + _dot(s_cat.astype(BF16), v4)
        ss = _dot((y * y).astype(BF16), bd_ref[...])
        o_ref[rr, :] = y * lax.rsqrt(ss * (1.0 / HEAD_DIM) + EPS) * _silu(g)
        kz = (k * zeta_ref[...]).astype(BF16)
        r = gch_ref[...] * r + _dot_tn(kz, v.astype(BF16)) * rmask_ref[...]
    r_ref[...] = r


def _retention(uc, dmat, zeta, xi, gch, bdn, rmask, qmask, vmask, bsz, seq):
    ts = SEQ_TILE
    nt = seq // ts
    const = lambda b, n: (0, 0)
    tok = lambda b, n: (b * nt + n, 0)
    return pl.pallas_call(
        _ret_kernel,
        grid=(bsz, nt),
        in_specs=[
            pl.BlockSpec((ts, 1024), tok),
            pl.BlockSpec((GROUP_HEADS * CHUNK, CHUNK), const),
            pl.BlockSpec((CHUNK, 256), const), pl.BlockSpec((CHUNK, 256), const),
            pl.BlockSpec((1, 256), const), pl.BlockSpec((256, 256), const),
            pl.BlockSpec((256, 256), const),
            pl.BlockSpec((8, 256), const), pl.BlockSpec((8, 256), const),
        ],
        out_specs=pl.BlockSpec((ts, 256), tok),
        out_shape=jax.ShapeDtypeStruct((bsz * seq, GROUP_WIDTH), F32),
        scratch_shapes=[pltpu.VMEM((256, 256), F32)],
        compiler_params=pltpu.CompilerParams(dimension_semantics=("arbitrary", "arbitrary")),
        name="retention",
    )(uc, dmat, zeta, xi, gch, bdn, rmask, qmask, vmask)


CONF_HALO = 32
CONF_ROWS = 64


def _conf_kernel(u_ref, w_ref, b_ref, lg_ref, lb_ref, o_ref, ext_ref):
    ts = SEQ_TILE

    @pl.when(pl.program_id(1) == 0)
    def _():
        ext_ref[0:CONF_HALO, :] = jnp.zeros((CONF_HALO, 256), F32)
        ext_ref[CONF_HALO + ts:CONF_HALO + ts + 8, :] = jnp.zeros((8, 256), F32)

    ext_ref[CONF_HALO:CONF_HALO + ts, :] = u_ref[:, 0:256] * _sigmoid(u_ref[:, 256:512])

    nwin = CONF_ROWS + CONF_HALO + 8
    for c in range(ts // CONF_ROWS):
        r0 = c * CONF_ROWS
        win = ext_ref[pl.ds(r0, nwin), :]
        acc = jnp.zeros((CONF_ROWS, 256), F32) + b_ref[...]
        for off in range(8):
            shifted = win if off == 0 else pltpu.roll(win, nwin - off, axis=0)
            for sh in range(CONF_HALO - CONF_KERNEL + 1, CONF_HALO + 1):
                if sh % 8 == off:
                    k = sh - (CONF_HALO - CONF_KERNEL + 1)
                    acc = acc + w_ref[k:k + 1, :] * shifted[sh - off:sh - off + CONF_ROWS, :]
        mu = jnp.mean(acc, axis=-1, keepdims=True)
        xc = acc - mu
        var = jnp.mean(xc * xc, axis=-1, keepdims=True)
        o_ref[pl.ds(r0, CONF_ROWS), :] = _silu(xc * lax.rsqrt(var + EPS) * lg_ref[...] + lb_ref[...])
    ext_ref[0:CONF_HALO, :] = ext_ref[ts:ts + CONF_HALO, :]


def _conformer(ud, w, b, lg, lb, bsz, seq, layer):
    ts = SEQ_TILE
    nt = seq // ts
    lay = lambda b_, n: (layer, 0, 0)
    tok = lambda b_, n: (b_ * nt + n, 0)
    return pl.pallas_call(
        _conf_kernel,
        grid=(bsz, nt),
        in_specs=[
            pl.BlockSpec((ts, 512), tok),
            pl.BlockSpec((None, 32, 256), lay), pl.BlockSpec((None, 1, 256), lay),
            pl.BlockSpec((None, 1, 256), lay), pl.BlockSpec((None, 1, 256), lay),
        ],
        out_specs=pl.BlockSpec((ts, 256), tok),
        out_shape=jax.ShapeDtypeStruct((bsz * seq, GROUP_WIDTH), F32),
        scratch_shapes=[pltpu.VMEM((ts + CONF_HALO + 8, 256), F32)],
        compiler_params=pltpu.CompilerParams(dimension_semantics=("arbitrary", "arbitrary")),
        name="conformer",
    )(ud, w, b, lg, lb)


FF_CHUNK = 1024


def _rms_rows(x, g):
    return x * lax.rsqrt(jnp.mean(x * x, axis=-1, keepdims=True) + EPS) * g


def _ffn_kernel(x_ref, ya_ref, yb_ref, yc_ref, yd_ref, p_ref, wout_ref, gf_ref, wup_ref, wdn_ref,
                gp_ref, wple_ref, wgate_ref, o_ref, hb_ref):
    tm = x_ref.shape[0]
    halves = [pl.ds(0, tm // 2), pl.ds(tm // 2, tm // 2)]
    x1 = []
    for rr in halves:
        acc = x_ref[rr, :]
        for m, y_ref in enumerate((ya_ref, yb_ref, yc_ref, yd_ref)):
            acc = acc + _dot(y_ref[rr, :].astype(BF16), wout_ref[m * 256:(m + 1) * 256, :])
        hb_ref[rr, :] = _rms_rows(acc, gf_ref[...]).astype(BF16)
        x1.append(acc)

    def down(rr, c):
        up = jnp.maximum(_dot(hb_ref[rr, :], wup_ref[:, c * FF_CHUNK:(c + 1) * FF_CHUNK]), 0.0)
        return _dot((up * up).astype(BF16), wdn_ref[c * FF_CHUNK:(c + 1) * FF_CHUNK, :])

    ffs = [down(rr, 0) for rr in halves]
    for c in range(1, D_FF // FF_CHUNK):
        ffs = [ff + down(rr, c) for ff, rr in zip(ffs, halves)]
    for rr, acc, ff in zip(halves, x1, ffs):
        x2 = acc + ff
        gate = _sigmoid(_dot(_rms_rows(x2, gp_ref[...]).astype(BF16), wgate_ref[...]))
        o_ref[rr, :] = x2 + _dot(p_ref[rr, :].astype(BF16), wple_ref[...]) * gate


def _ffn(x2d, ya, yb, yc, yd, p_all, wout, gf, wup, wdn, gp, wple, wgate, tm, layer):
    t = x2d.shape[0]
    tok = lambda i: (i, 0)
    lay = lambda i: (layer, 0, 0)
    once = dict(pipeline_mode=pl.Buffered(1))
    return pl.pallas_call(
        _ffn_kernel,
        grid=(t // tm,),
        in_specs=[
            pl.BlockSpec((tm, D_MODEL), tok),
            pl.BlockSpec((tm, 256), tok), pl.BlockSpec((tm, 256), tok),
            pl.BlockSpec((tm, 256), tok), pl.BlockSpec((tm, 256), tok),
            pl.BlockSpec((None, tm, PLE_DIM), lambda i: (layer, i, 0)),
            pl.BlockSpec((None, D_MODEL, D_MODEL), lay, **once),
            pl.BlockSpec((None, 1, D_MODEL), lay),
            pl.BlockSpec((None, D_MODEL, D_FF), lay, **once),
            pl.BlockSpec((None, D_FF, D_MODEL), lay, **once),
            pl.BlockSpec((None, 1, D_MODEL), lay),
            pl.BlockSpec((None, PLE_DIM, D_MODEL), lay, **once),
            pl.BlockSpec((None, D_MODEL, D_MODEL), lay, **once),
        ],
        out_specs=pl.BlockSpec((tm, D_MODEL), tok),
        out_shape=jax.ShapeDtypeStruct((t, D_MODEL), F32),
        scratch_shapes=[pltpu.VMEM((tm, D_MODEL), BF16)],
        compiler_params=pltpu.CompilerParams(
            dimension_semantics=("arbitrary",), vmem_limit_bytes=56 * 1024 * 1024),
        name="ffn",
    )(x2d, ya, yb, yc, yd, p_all, wout, gf, wup, wdn, gp, wple, wgate)


def _head_mask(head_of_lane):
    m = np.zeros((8, GROUP_WIDTH), np.float32)
    for h in range(GROUP_HEADS):
        m[h] = head_of_lane == h
    return m


def _tables(seq):
    pos = jnp.arange(seq, dtype=F32)
    ang_a = ROPE_THETA ** (-jnp.arange(0, HEAD_DIM, 2, dtype=F32) / HEAD_DIM)
    ang_c = 1.0 / (10000.0 ** jnp.linspace(0.0, 1.0, HEAD_DIM // 2, dtype=F32))

    def cs(ang):
        a = pos[:, None] * ang[None, :]
        return jnp.tile(jnp.cos(a), (1, 4)), jnp.tile(jnp.sin(a), (1, 4))

    cosa, sina = cs(ang_a)
    cosc, sinc = cs(ang_c)
    log_g = jnp.log(1.0 - 2.0 ** (-5.0 - jnp.arange(GROUP_HEADS, dtype=F32)))
    idx = jnp.arange(CHUNK, dtype=F32)
    dist = idx[:, None] - idx[None, :]
    dmat = jnp.where((dist >= 0)[None], jnp.exp(jnp.maximum(dist, 0.0)[None] * log_g[:, None, None]), 0.0)
    zeta = jnp.exp((CHUNK - 1 - idx)[:, None] * log_g)[:, _HEAD_PERM]
    xi = jnp.exp((idx + 1.0)[:, None] * log_g)[:, _HEAD_NAT]
    gch = jnp.exp(CHUNK * log_g)[_HEAD_NAT][None, :]
    return dict(
        cosa=cosa, sina=sina, cosc=cosc, sinc=sinc, dmat=dmat, zeta=zeta, xi=xi, gch=gch,
        bd_perm=jnp.asarray(_HEAD_PERM[:, None] == _HEAD_PERM[None, :], BF16),
        bd_nat=jnp.asarray(_HEAD_NAT[:, None] == _HEAD_NAT[None, :], BF16),
        rmask=jnp.asarray(_HEAD_PERM[:, None] == _HEAD_NAT[None, :], F32),
        smask=jnp.asarray((_LANE[:, None] // SSM_STATE) == (_HEAD_NAT[None, :] // (GROUP_HEADS // SSM_GROUPS)), F32),
        qmask=jnp.asarray(_head_mask(_HEAD_PERM)),
        vmask=jnp.asarray(_head_mask(_HEAD_NAT)),
        ltri=jnp.asarray(np.tril(np.ones((CHUNK, CHUNK), np.float32)), BF16),
    )


def _build_w_cat(w_in):
    w = w_in
    depth = w.shape[0]

    def nat(c0):
        return w[:, :, c0:c0 + 256]

    def perm(c0):
        blk = w[:, :, c0:c0 + 256].reshape(depth, D_MODEL, GROUP_HEADS, 2, 32)
        return blk.transpose(0, 1, 3, 2, 4).reshape(depth, D_MODEL, 256)

    dt = jnp.repeat(w[:, :, _BDT:_BDT + GROUP_HEADS], HEAD_DIM, axis=2)
    return jnp.concatenate([
        perm(_A0), perm(_A0 + 256), nat(_A0 + 512),
        nat(_BZ), nat(_BX), nat(_BB), nat(_BC), dt,
        perm(_C0), perm(_C0 + 256), nat(_C0 + 512), nat(_C0 + 768),
        nat(_D0), nat(_D0 + 256)], axis=2).astype(BF16)


def kernel(x, p, norm_mix, w_in, attn_q_norm, attn_k_norm, ssm_conv_w, ssm_conv_b, ssm_dt_bias,
           ssm_a_log, ssm_d, ssm_norm, conv_dw_w, conv_dw_b, conv_ln_g, conv_ln_b, w_out,
           norm_ffn, w_up, w_down, norm_ple, w_ple, w_ple_gate):
    bsz, seq, _ = x.shape
    depth = w_in.shape[0]
    tm = 512
    tb = _tables(seq)
    row = lambda a: a[:, None, :]
    w_cat = _build_w_cat(w_in)
    g_mix, g_ffn, g_ple = row(norm_mix), row(norm_ffn), row(norm_ple)
    gq, gk = row(attn_q_norm[:, _PERM % HEAD_DIM]), row(attn_k_norm[:, _PERM % HEAD_DIM])
    cw = jnp.pad(ssm_conv_w, ((0, 0), (0, 8 - SSM_CONV), (0, 0)))
    dw = jnp.pad(conv_dw_w, ((0, 0), (0, 32 - CONF_KERNEL), (0, 0)))
    dtb, alog, dsk = row(ssm_dt_bias[:, _HEAD_NAT]), row(ssm_a_log[:, _HEAD_NAT]), row(ssm_d[:, _HEAD_NAT])
    wout, wup, wdn = w_out.astype(BF16), w_up.astype(BF16), w_down.astype(BF16)
    wple, wgate = w_ple.astype(BF16), w_ple_gate.astype(BF16)
    p_all = p.reshape(depth, bsz * seq, PLE_DIM)
    x2d = x.reshape(bsz * seq, D_MODEL)
    for i in range(depth):
        oa, ob, oc, od = _inproj(x2d, g_mix, w_cat, tb["cosa"], tb["sina"], tb["cosc"], tb["sinc"],
                                 gq, gk, tb["bd_perm"], seq, tm, i)
        ya = _attention(oa, tb["qmask"], bsz, seq)
        yb = _ssd(ob, cw, row(ssm_conv_b), dtb, alog, dsk, row(ssm_norm), tb["ltri"], tb["smask"],
                  tb["vmask"], bsz, seq, i)
        yc = _retention(oc, tb["dmat"].reshape(GROUP_HEADS * CHUNK, CHUNK), tb["zeta"], tb["xi"], tb["gch"], tb["bd_nat"], tb["rmask"],
                        tb["qmask"], tb["vmask"], bsz, seq)
        yd = _conformer(od, dw, row(conv_dw_b), row(conv_ln_g), row(conv_ln_b), bsz, seq, i)
        x2d = _ffn(x2d, ya, yb, yc, yd, p_all, wout, g_ffn, wup, wdn, g_ple, wple, wgate, tm, i)
    return x2d.reshape(bsz, seq, D_MODEL)
```

```python
import functools

import numpy as np
import jax
import jax.numpy as jnp
from jax import lax
from jax.experimental import pallas as pl
from jax.experimental.pallas import tpu as pltpu

F32 = jnp.float32
BF16 = jnp.bfloat16

D_MODEL = 1024
GROUP_WIDTH = 256
GROUP_HEADS = 4
HEAD_DIM = 64
EPS = 1e-6
ATTN_PATTERNS = ((128, 1), (512, 4), (2048, 16))
ATTN_BLOCK = 128
ROPE_THETA = 10000.0
SSM_STATE = 128
SSM_GROUPS = 2
SSM_CONV = 4
CHUNK = 128
CONF_KERNEL = 31
D_FF = 4 * D_MODEL
PLE_DIM = 256
NEG = -1e30
LOG2E = 1.4426950408889634
CONF_HALO = 32
CONF_ROWS = 64

_A0 = 0
_B0 = 3 * GROUP_WIDTH
_BZ, _BX, _BB, _BC = _B0, _B0 + 256, _B0 + 512, _B0 + 768
_BDT = _B0 + 1024
_C0 = _BDT + GROUP_HEADS
_D0 = _C0 + 4 * GROUP_WIDTH
IN_COLS = _D0 + 2 * GROUP_WIDTH

_LANE = np.arange(GROUP_WIDTH)
_PERM = ((_LANE % 128) // 32) * HEAD_DIM + (_LANE // 128) * 32 + (_LANE % 32)
_HEAD_PERM = (_LANE % 128) // 32
_HEAD_NAT = _LANE // HEAD_DIM

N_PROJ = 14 * GROUP_WIDTH


def _sigmoid(x):
    return 1.0 / (1.0 + jnp.exp(-x))


def _silu(x):
    return x * _sigmoid(x)


def _softplus(x):
    return jnp.maximum(x, 0.0) + jnp.log(1.0 + jnp.exp(-jnp.abs(x)))


def _dot(a, b):
    return jnp.dot(a, b, preferred_element_type=F32)


def _dot_nt(a, b):
    return lax.dot_general(a, b, (((1,), (1,)), ((), ())), preferred_element_type=F32)


def _dot_tn(a, b):
    return lax.dot_general(a, b, (((0,), (0,)), ((), ())), preferred_element_type=F32)


def _inproj_kernel(x0_ref, xn_ref, g_ref, w_ref, cosa_ref, sina_ref, cosc_ref, sinc_ref,
                   gq_ref, gk_ref, bd_ref, dw_ref, db_ref, lg_ref, lb_ref,
                   oa_ref, ob_ref, oc_ref, yd_ref, hb0_ref, hb1_ref, ext_ref, *, nseq):
    step = pl.program_id(0)
    tm = oa_ref.shape[0]

    def norm_into(src_ref, dst_ref):
        x = src_ref[...]
        h = x * lax.rsqrt(jnp.mean(x * x, axis=-1, keepdims=True) + EPS) * g_ref[...]
        dst_ref[...] = h.astype(BF16)

    @pl.when(step == 0)
    def _():
        norm_into(x0_ref, hb0_ref)
        ext_ref[CONF_HALO + tm:CONF_HALO + tm + 8, :] = jnp.zeros((8, 256), F32)

    @pl.when(step % nseq == 0)
    def _():
        ext_ref[0:CONF_HALO, :] = jnp.zeros((CONF_HALO, 256), F32)

    rest = (w_ref, cosa_ref, sina_ref, cosc_ref, sinc_ref, gq_ref, gk_ref, bd_ref, dw_ref, db_ref, lg_ref,
            lb_ref, oa_ref, ob_ref, oc_ref, yd_ref, ext_ref)

    @pl.when(step % 2 == 0)
    def _():
        norm_into(xn_ref, hb1_ref)
        _inproj_body(hb0_ref, *rest)

    @pl.when(step % 2 == 1)
    def _():
        norm_into(xn_ref, hb0_ref)
        _inproj_body(hb1_ref, *rest)


def _inproj_body(hb_ref, w_ref, cosa_ref, sina_ref, cosc_ref, sinc_ref, gq_ref, gk_ref, bd_ref,
                 dw_ref, db_ref, lg_ref, lb_ref, oa_ref, ob_ref, oc_ref, yd_ref, ext_ref):
    tm = oa_ref.shape[0]

    def mm(j):
        return _dot(hb_ref[...], w_ref[:, j * 256:(j + 1) * 256])

    def rot(t, cos, sin):
        t1, t2 = t[:, :128], t[:, 128:]
        return jnp.concatenate([t1 * cos - t2 * sin, t2 * cos + t1 * sin], axis=1)

    def headnorm(t, gain):
        ss = _dot((t * t).astype(BF16), bd_ref[...])
        return t * lax.rsqrt(ss * (1.0 / HEAD_DIM) + EPS) * gain

    cosa, sina = cosa_ref[...], sina_ref[...]
    cosc, sinc = cosc_ref[...], sinc_ref[...]
    scale = HEAD_DIM ** -0.5
    ext_ref[CONF_HALO:CONF_HALO + tm, :] = mm(12) * _sigmoid(mm(13))
    conv_chunks = iter(range(tm // CONF_ROWS))

    def conv_some(n):
        for _ in range(n):
            c = next(conv_chunks, None)
            if c is not None:
                _conformer_rows(ext_ref, dw_ref, db_ref, lg_ref, lb_ref, yd_ref, c)

    oa_ref[:, 0:256] = mm(0)
    conv_some(1)
    oa_ref[:, 256:512] = mm(1)
    conv_some(1)
    oa_ref[:, 512:768] = mm(2)
    conv_some(1)
    for j in range(5):
        ob_ref[:, j * 256:(j + 1) * 256] = mm(3 + j)
        conv_some(1)
    oa_ref[:, 0:256] = rot(headnorm(oa_ref[:, 0:256], gq_ref[...]), cosa, sina) * (scale * LOG2E)
    oa_ref[:, 256:512] = rot(headnorm(oa_ref[:, 256:512], gk_ref[...]), cosa, sina)
    oc_ref[:, 0:256] = rot(mm(8), cosc, sinc)
    oc_ref[:, 256:512] = rot(mm(9), cosc, sinc) * scale
    oc_ref[:, 512:768] = mm(10)
    oc_ref[:, 768:1024] = mm(11)
    conv_some(tm // CONF_ROWS)
    ext_ref[0:CONF_HALO, :] = ext_ref[tm:tm + CONF_HALO, :]


def _conformer_rows(ext_ref, w_ref, b_ref, lg_ref, lb_ref, o_ref, c):
    nwin = CONF_ROWS + CONF_HALO + 8
    r0 = c * CONF_ROWS
    win = ext_ref[pl.ds(r0, nwin), :]
    acc = jnp.zeros((CONF_ROWS, 256), F32) + b_ref[...]
    for off in range(8):
        shifted = win if off == 0 else pltpu.roll(win, nwin - off, axis=0)
        for sh in range(CONF_HALO - CONF_KERNEL + 1, CONF_HALO + 1):
            if sh % 8 == off:
                k = sh - (CONF_HALO - CONF_KERNEL + 1)
                acc = acc + w_ref[k:k + 1, :] * shifted[sh - off:sh - off + CONF_ROWS, :]
    mu = jnp.mean(acc, axis=-1, keepdims=True)
    xc = acc - mu
    var = jnp.mean(xc * xc, axis=-1, keepdims=True)
    o_ref[pl.ds(r0, CONF_ROWS), :] = _silu(xc * lax.rsqrt(var + EPS) * lg_ref[...] + lb_ref[...])


def _inproj(x2d, g, w_cat, cosa, sina, cosc, sinc, gq, gk, bd, dw, db, lg, lb, seq, tm, layer):
    t = x2d.shape[0]
    nseq = seq // tm
    const = lambda i: (0, 0)
    lay = lambda i: (layer, 0, 0)
    tok = lambda i: (i, 0)
    pos = lambda i: (i % nseq, 0)
    nxt = lambda i: (jnp.minimum(i + 1, t // tm - 1), 0)
    return pl.pallas_call(
        functools.partial(_inproj_kernel, nseq=nseq),
        grid=(t // tm,),
        in_specs=[
            pl.BlockSpec((tm, D_MODEL), const), pl.BlockSpec((tm, D_MODEL), nxt),
            pl.BlockSpec((None, 1, D_MODEL), lay),
            pl.BlockSpec((None, D_MODEL, N_PROJ), lay),
            pl.BlockSpec((tm, 128), pos), pl.BlockSpec((tm, 128), pos),
            pl.BlockSpec((tm, 128), pos), pl.BlockSpec((tm, 128), pos),
            pl.BlockSpec((None, 1, 256), lay), pl.BlockSpec((None, 1, 256), lay),
            pl.BlockSpec((256, 256), const),
            pl.BlockSpec((None, 32, 256), lay), pl.BlockSpec((None, 1, 256), lay),
            pl.BlockSpec((None, 1, 256), lay), pl.BlockSpec((None, 1, 256), lay),
        ],
        out_specs=[
            pl.BlockSpec((tm, 768), tok), pl.BlockSpec((tm, 1280), tok),
            pl.BlockSpec((tm, 1024), tok), pl.BlockSpec((tm, 256), tok),
        ],
        out_shape=[
            jax.ShapeDtypeStruct((t, 768), F32), jax.ShapeDtypeStruct((t, 1280), F32),
            jax.ShapeDtypeStruct((t, 1024), F32), jax.ShapeDtypeStruct((t, 256), F32),
        ],
        scratch_shapes=[pltpu.VMEM((tm, D_MODEL), BF16), pltpu.VMEM((tm, D_MODEL), BF16),
                        pltpu.VMEM((tm + CONF_HALO + 8, 256), F32)],
        compiler_params=pltpu.CompilerParams(
            dimension_semantics=("arbitrary",), vmem_limit_bytes=56 * 1024 * 1024),
        name="inproj",
    )(x2d, x2d, g, w_cat, cosa, sina, cosc, sinc, gq, gk, bd, dw, db, lg, lb)


ATTN_TILE = 2048


def _attn_kernel(*refs):
    q_refs, kp_refs, kc_refs, vp_refs, vc_refs = (refs[2 * i:2 * i + 2] for i in range(5))
    qm_ref, o_ref = refs[10], refs[11]
    m_refs, l_refs, u_refs = refs[12:16], refs[16:20], refs[20:22]
    tile = pl.program_id(1)

    ii = lax.broadcasted_iota(jnp.int32, (128, 256), 0)
    jj = lax.broadcasted_iota(jnp.int32, (128, 256), 1)
    band_bias = jnp.where((jj >= ii) & (jj <= ii + ATTN_BLOCK), 0.0, NEG).astype(F32)
    noprev_bias = jnp.where(jj < ATTN_BLOCK, NEG, 0.0).astype(F32)
    first_bias = band_bias + jnp.where(tile == 0, 1.0, 0.0).astype(F32) * noprev_bias
    lo64 = lax.broadcasted_iota(jnp.int32, (128, 128), 1) < HEAD_DIM

    def rows(start, dil):
        if dil == 1:
            return pl.ds(pl.multiple_of(start, 128), 128)
        return pl.ds(start, 128, stride=dil)

    def wide(pair, rr):
        return jnp.concatenate([pair[0][rr, :], pair[1][rr, :]], axis=1)

    def blocks(descs, dil, init):
        st = []
        for qs, klo_refs, vlo_refs, lo_s, hi_s, bias in descs:
            qr, lo, hi = rows(qs, dil), rows(lo_s, dil), rows(hi_s, dil)
            q = wide(q_refs, qr)
            k = jnp.concatenate([wide(klo_refs, lo), wide(kc_refs, hi)], axis=0).astype(BF16)
            v = [jnp.concatenate([vlo_refs[c][lo, :], vc_refs[c][hi, :]], axis=0).astype(BF16) for c in range(2)]
            q4 = jnp.concatenate([(q * qm_ref[h:h + 1, :]).astype(BF16) for h in range(GROUP_HEADS)], axis=0)
            st.append(dict(qr=qr, v=v, bias=bias, s_all=_dot_nt(q4, k)))
        if not init:
            for d in st:
                d["m_old"] = [m_refs[h][d["qr"], :] for h in range(GROUP_HEADS)]
                d["l_old"] = [l_refs[h][d["qr"], :] for h in range(GROUP_HEADS)]
        for d in st:
            d["m_new"], d["l_new"], d["alpha"], ps = [], [], [], []
            for h in range(GROUP_HEADS):
                s = d["s_all"][h * 128:(h + 1) * 128, :] + d["bias"]
                rmax = jnp.max(s, axis=-1, keepdims=True)
                m_new = jnp.broadcast_to(rmax, (128, 128)) if init else jnp.maximum(d["m_old"][h], rmax)
                p = jnp.exp2(s - jnp.concatenate([m_new, m_new], axis=1))
                rsum = jnp.sum(p, axis=-1, keepdims=True)
                if init:
                    d["l_new"].append(jnp.broadcast_to(rsum, (128, 128)))
                else:
                    alpha = jnp.exp2(d["m_old"][h] - m_new)
                    d["l_new"].append(alpha * d["l_old"][h] + rsum)
                    d["alpha"].append(alpha)
                d["m_new"].append(m_new)
                ps.append(p.astype(BF16))
            d["pv"] = []
            for c in range(2):
                pv = _dot(jnp.concatenate(ps[2 * c:2 * c + 2], axis=0), d["v"][c])
                d["pv"].append(jnp.where(lo64, pv[0:128, :], pv[128:256, :]))
        if not init:
            for d in st:
                d["u_old"] = [u_refs[c][d["qr"], :] for c in range(2)]
        for d in st:
            for h in range(GROUP_HEADS):
                m_refs[h][d["qr"], :] = d["m_new"][h]
                l_refs[h][d["qr"], :] = d["l_new"][h]
            for c in range(2):
                if init:
                    u_refs[c][d["qr"], :] = d["pv"][c]
                else:
                    a_c = jnp.where(lo64, d["alpha"][2 * c], d["alpha"][2 * c + 1])
                    u_refs[c][d["qr"], :] = d["u_old"][c] * a_c + d["pv"][c]

    def first_desc(r, span):
        return (r, kp_refs, vp_refs, ATTN_TILE - span + r, r, first_bias)

    def later_desc(r, m, span):
        base = r + (m - 1) * span
        return (base + span, kc_refs, vc_refs, base, base + span, band_bias)

    for pi, (_, dil) in enumerate(reversed(ATTN_PATTERNS)):
        span = ATTN_BLOCK * dil
        nblk = ATTN_TILE // span
        init = pi == 0
        if nblk == 1:
            def class_pair(i, carry, dil=dil, span=span, init=init):
                blocks([first_desc(2 * i, span), first_desc(2 * i + 1, span)], dil, init)
                return carry

            lax.fori_loop(0, dil // 2, class_pair, 0)
        else:
            def per_class(r, carry, dil=dil, span=span, nblk=nblk, init=init):
                blocks([first_desc(r, span), later_desc(r, 1, span)], dil, init)

                def later_pair(i, c):
                    blocks([later_desc(r, 2 * i + 2, span), later_desc(r, 2 * i + 3, span)], dil, init)
                    return c

                lax.fori_loop(0, (nblk - 2) // 2, later_pair, 0)
                return carry

            lax.fori_loop(0, dil, per_class, 0)

    lo64w = lax.broadcasted_iota(jnp.int32, (256, 128), 1) < HEAD_DIM

    def finish(c, carry):
        rr = pl.ds(pl.multiple_of(c * 256, 256), 256)
        for hlf in range(2):
            den = jnp.where(lo64w, l_refs[2 * hlf][rr, :], l_refs[2 * hlf + 1][rr, :])
            o_ref[rr, hlf * 128:(hlf + 1) * 128] = u_refs[hlf][rr, :] / den
        return carry

    lax.fori_loop(0, ATTN_TILE // 256, finish, 0)


def _attention(qkv, qmask, bsz, seq):
    nt = seq // ATTN_TILE
    cur = lambda c: (lambda b, n: (b * nt + n, c))
    prev = lambda c: (lambda b, n: (b * nt + jnp.maximum(n - 1, 0), c))
    blk = (ATTN_TILE, 128)
    col_maps = [cur(0), cur(1),
                prev(2), prev(3), cur(2), cur(3),
                prev(4), prev(5), cur(4), cur(5)]
    return pl.pallas_call(
        _attn_kernel,
        grid=(bsz, nt),
        in_specs=[pl.BlockSpec(blk, m) for m in col_maps] + [pl.BlockSpec((8, 256), lambda b, n: (0, 0))],
        out_specs=pl.BlockSpec((ATTN_TILE, 256), lambda b, n: (b * nt + n, 0)),
        out_shape=jax.ShapeDtypeStruct((bsz * seq, GROUP_WIDTH), F32),
        scratch_shapes=[pltpu.VMEM((ATTN_TILE, 128), F32)] * 10,
        compiler_params=pltpu.CompilerParams(
            dimension_semantics=("arbitrary", "arbitrary"), vmem_limit_bytes=56 * 1024 * 1024),
        name="attn",
    )(*([qkv] * 10), qmask)


SEQ_TILE = 1024
SSM_ROWS = 64


def _ssd_kernel(u_ref, cw_ref, cb_ref, dtb_ref, a_ref, dsk_ref, ng_ref, ltri_ref, smask_ref,
                vm_ref, o_ref, ext_ref, act_ref, hs_ref):
    ts = SEQ_TILE

    @pl.when(pl.program_id(1) == 0)
    def _():
        ext_ref[0:8, :] = jnp.zeros((8, 768), F32)
        hs_ref[...] = jnp.zeros(hs_ref.shape, F32)

    ext_ref[8:8 + ts, :] = u_ref[:, 256:1024]
    for c in range(ts // SSM_ROWS):
        for gc in range(3):
            cols = slice(gc * 256, (gc + 1) * 256)
            win = ext_ref[pl.ds(c * SSM_ROWS, SSM_ROWS + 8), cols]
            acc = cb_ref[:, cols] + cw_ref[SSM_CONV - 1:SSM_CONV, cols] * win[8:8 + SSM_ROWS, :]
            for sh in range(1, SSM_CONV):
                k = SSM_CONV - 1 - sh
                acc = acc + cw_ref[k:k + 1, cols] * pltpu.roll(win, sh, axis=0)[8:8 + SSM_ROWS, :]
            act_ref[pl.ds(c * SSM_ROWS, SSM_ROWS), cols] = _silu(acc)
    ext_ref[0:8, :] = ext_ref[ts:ts + 8, :]

    ii = lax.broadcasted_iota(jnp.int32, (CHUNK, CHUNK), 0)
    jj = lax.broadcasted_iota(jnp.int32, (CHUNK, CHUNK), 1)
    causal = jj <= ii
    ltri = ltri_ref[...]

    hs = hs_ref[...]
    for c in range(ts // CHUNK):
        rr = pl.ds(c * CHUNK, CHUNK)
        z = u_ref[rr, 0:256]
        xs = act_ref[rr, 0:256]
        bm = act_ref[rr, 256:512]
        cm = act_ref[rr, 512:768]
        dt = _softplus(u_ref[rr, 1024:1280] + dtb_ref[...])
        da = dt * (-jnp.exp(a_ref[...]))
        da_hi = da.astype(BF16)
        rem = da - da_hi.astype(F32)
        da_mid = rem.astype(BF16)
        da_lo = (rem - da_mid.astype(F32)).astype(BF16)
        acum = _dot(ltri, da_hi) + _dot(ltri, da_mid) + _dot(ltri, da_lo)
        total = acum[CHUNK - 1:CHUNK, :]
        bmb = bm.astype(BF16)
        cmb = cm.astype(BF16)
        xdt = xs * dt
        acum_t = (jnp.transpose(acum[:, 0:128]), jnp.transpose(acum[:, 128:256]))
        cbs = [_dot_nt(cmb[:, g * 128:(g + 1) * 128], bmb[:, g * 128:(g + 1) * 128]) for g in range(SSM_GROUPS)]
        mhs = []
        for h in range(GROUP_HEADS):
            col = acum[:, h * HEAD_DIM:h * HEAD_DIM + 1]
            rsel = (h % 2) * HEAD_DIM
            row = acum_t[h // 2][rsel:rsel + 1, :]
            lmat = jnp.exp(jnp.where(causal, col - row, NEG))
            mhs.append((cbs[h // 2] * lmat).astype(BF16))
        xdt4 = jnp.concatenate([(xdt * vm_ref[h:h + 1, :]).astype(BF16) for h in range(GROUP_HEADS)], axis=0)
        y = _dot(cmb, hs.astype(BF16)) * jnp.exp(acum) + _dot(jnp.concatenate(mhs, axis=1), xdt4)
        xw = (xs * (jnp.exp(total - acum) * dt)).astype(BF16)
        hs = jnp.exp(total) * hs + _dot_tn(bmb, xw) * smask_ref[...]
        y = (y + dsk_ref[...] * xs) * _silu(z)
        halves = []
        for g in range(SSM_GROUPS):
            yg = y[:, g * 128:(g + 1) * 128]
            halves.append(yg * lax.rsqrt(jnp.mean(yg * yg, axis=-1, keepdims=True) + EPS))
        o_ref[rr, :] = jnp.concatenate(halves, axis=1) * ng_ref[...]
    hs_ref[...] = hs


def _ssd(ub, cw, cb, dtb, a_exp, dsk, ng, ltri, smask, vmask, bsz, seq, layer):
    ts = SEQ_TILE
    nt = seq // ts
    const = lambda b, n: (0, 0)
    lay = lambda b, n: (layer, 0, 0)
    tok = lambda b, n: (b * nt + n, 0)
    return pl.pallas_call(
        _ssd_kernel,
        grid=(bsz, nt),
        in_specs=[
            pl.BlockSpec((ts, 1280), tok),
            pl.BlockSpec((None, 8, 768), lay), pl.BlockSpec((None, 1, 768), lay),
            pl.BlockSpec((None, 1, 256), lay), pl.BlockSpec((None, 1, 256), lay),
            pl.BlockSpec((None, 1, 256), lay), pl.BlockSpec((None, 1, 256), lay),
            pl.BlockSpec((CHUNK, CHUNK), const), pl.BlockSpec((256, 256), const),
            pl.BlockSpec((8, 256), const),
        ],
        out_specs=pl.BlockSpec((ts, 256), tok),
        out_shape=jax.ShapeDtypeStruct((bsz * seq, GROUP_WIDTH), F32),
        scratch_shapes=[pltpu.VMEM((ts + 8, 768), F32), pltpu.VMEM((ts, 768), F32), pltpu.VMEM((256, 256), F32)],
        compiler_params=pltpu.CompilerParams(dimension_semantics=("arbitrary", "arbitrary")),
        name="ssd",
    )(ub, cw, cb, dtb, a_exp, dsk, ng, ltri, smask, vmask)


def _ret_kernel(u_ref, dmat_ref, zeta_ref, xi_ref, gch_ref, bd_ref, rmask_ref, qm_ref, vm_ref,
                o_ref, r_ref):
    @pl.when(pl.program_id(1) == 0)
    def _():
        r_ref[...] = jnp.zeros(r_ref.shape, F32)

    r = r_ref[...]
    for c in range(SEQ_TILE // CHUNK):
        rr = pl.ds(c * CHUNK, CHUNK)
        q = u_ref[rr, 0:256]
        k = u_ref[rr, 256:512]
        v = u_ref[rr, 512:768]
        g = u_ref[rr, 768:1024]
        kb = k.astype(BF16)
        q4 = jnp.concatenate([(q * qm_ref[h:h + 1, :]).astype(BF16) for h in range(GROUP_HEADS)], axis=0)
        s4 = _dot_nt(q4, kb) * dmat_ref[...]
        s_cat = jnp.concatenate([s4[h * CHUNK:(h + 1) * CHUNK, :] for h in range(GROUP_HEADS)], axis=1)
        v4 = jnp.concatenate([(v * vm_ref[h:h + 1, :]).astype(BF16) for h in range(GROUP_HEADS)], axis=0)
        y = _dot(q.astype(BF16), r.astype(BF16)) * xi_ref[...] + _dot(s_cat.astype(BF16), v4)
        ss = _dot((y * y).astype(BF16), bd_ref[...])
        o_ref[rr, :] = y * lax.rsqrt(ss * (1.0 / HEAD_DIM) + EPS) * _silu(g)
        kz = (k * zeta_ref[...]).astype(BF16)
        r = gch_ref[...] * r + _dot_tn(kz, v.astype(BF16)) * rmask_ref[...]
    r_ref[...] = r


def _retention(uc, dmat, zeta, xi, gch, bdn, rmask, qmask, vmask, bsz, seq):
    ts = SEQ_TILE
    nt = seq // ts
    const = lambda b, n: (0, 0)
    tok = lambda b, n: (b * nt + n, 0)
    return pl.pallas_call(
        _ret_kernel,
        grid=(bsz, nt),
        in_specs=[
            pl.BlockSpec((ts, 1024), tok),
            pl.BlockSpec((GROUP_HEADS * CHUNK, CHUNK), const),
            pl.BlockSpec((CHUNK, 256), const), pl.BlockSpec((CHUNK, 256), const),
            pl.BlockSpec((1, 256), const), pl.BlockSpec((256, 256), const),
            pl.BlockSpec((256, 256), const),
            pl.BlockSpec((8, 256), const), pl.BlockSpec((8, 256), const),
        ],
        out_specs=pl.BlockSpec((ts, 256), tok),
        out_shape=jax.ShapeDtypeStruct((bsz * seq, GROUP_WIDTH), F32),
        scratch_shapes=[pltpu.VMEM((256, 256), F32)],
        compiler_params=pltpu.CompilerParams(dimension_semantics=("arbitrary", "arbitrary")),
        name="retention",
    )(uc, dmat, zeta, xi, gch, bdn, rmask, qmask, vmask)


FF_CHUNK = 1024


def _rms_rows(x, g):
    return x * lax.rsqrt(jnp.mean(x * x, axis=-1, keepdims=True) + EPS) * g


def _ffn_kernel(x_ref, ya_ref, yb_ref, yc_ref, yd_ref, p_ref, wout_ref, gf_ref, wup_ref, wdn_ref,
                gp_ref, wple_ref, wgate_ref, o_ref, hb_ref):
    tm = x_ref.shape[0]
    halves = [pl.ds(0, tm // 2), pl.ds(tm // 2, tm // 2)]
    x1 = []
    for rr in halves:
        acc = x_ref[rr, :]
        for m, y_ref in enumerate((ya_ref, yb_ref, yc_ref, yd_ref)):
            acc = acc + _dot(y_ref[rr, :].astype(BF16), wout_ref[m * 256:(m + 1) * 256, :])
        hb_ref[rr, :] = _rms_rows(acc, gf_ref[...]).astype(BF16)
        x1.append(acc)

    def down(rr, c):
        up = jnp.maximum(_dot(hb_ref[rr, :], wup_ref[:, c * FF_CHUNK:(c + 1) * FF_CHUNK]), 0.0)
        return _dot((up * up).astype(BF16), wdn_ref[c * FF_CHUNK:(c + 1) * FF_CHUNK, :])

    ffs = [down(rr, 0) for rr in halves]
    for c in range(1, D_FF // FF_CHUNK):
        ffs = [ff + down(rr, c) for ff, rr in zip(ffs, halves)]
    for rr, acc, ff in zip(halves, x1, ffs):
        x2 = acc + ff
        gate = _sigmoid(_dot(_rms_rows(x2, gp_ref[...]).astype(BF16), wgate_ref[...]))
        o_ref[rr, :] = x2 + _dot(p_ref[rr, :].astype(BF16), wple_ref[...]) * gate


def _ffn(x2d, ya, yb, yc, yd, p_all, wout, gf, wup, wdn, gp, wple, wgate, tm, layer):
    t = x2d.shape[0]
    tok = lambda i: (i, 0)
    lay = lambda i: (layer, 0, 0)
    once = dict(pipeline_mode=pl.Buffered(1))
    return pl.pallas_call(
        _ffn_kernel,
        grid=(t // tm,),
        in_specs=[
            pl.BlockSpec((tm, D_MODEL), tok),
            pl.BlockSpec((tm, 256), tok), pl.BlockSpec((tm, 256), tok),
            pl.BlockSpec((tm, 256), tok), pl.BlockSpec((tm, 256), tok),
            pl.BlockSpec((None, tm, PLE_DIM), lambda i: (layer, i, 0)),
            pl.BlockSpec((None, D_MODEL, D_MODEL), lay, **once),
            pl.BlockSpec((None, 1, D_MODEL), lay),
            pl.BlockSpec((None, D_MODEL, D_FF), lay, **once),
            pl.BlockSpec((None, D_FF, D_MODEL), lay, **once),
            pl.BlockSpec((None, 1, D_MODEL), lay),
            pl.BlockSpec((None, PLE_DIM, D_MODEL), lay, **once),
            pl.BlockSpec((None, D_MODEL, D_MODEL), lay, **once),
        ],
        out_specs=pl.BlockSpec((tm, D_MODEL), tok),
        out_shape=jax.ShapeDtypeStruct((t, D_MODEL), F32),
        scratch_shapes=[pltpu.VMEM((tm, D_MODEL), BF16)],
        compiler_params=pltpu.CompilerParams(
            dimension_semantics=("arbitrary",), vmem_limit_bytes=56 * 1024 * 1024),
        name="ffn",
    )(x2d, ya, yb, yc, yd, p_all, wout, gf, wup, wdn, gp, wple, wgate)


def _head_mask(head_of_lane):
    m = np.zeros((8, GROUP_WIDTH), np.float32)
    for h in range(GROUP_HEADS):
        m[h] = head_of_lane == h
    return m


def _tables(seq):
    pos = jnp.arange(seq, dtype=F32)
    ang_a = ROPE_THETA ** (-jnp.arange(0, HEAD_DIM, 2, dtype=F32) / HEAD_DIM)
    ang_c = 1.0 / (10000.0 ** jnp.linspace(0.0, 1.0, HEAD_DIM // 2, dtype=F32))

    def cs(ang):
        a = pos[:, None] * ang[None, :]
        return jnp.tile(jnp.cos(a), (1, 4)), jnp.tile(jnp.sin(a), (1, 4))

    cosa, sina = cs(ang_a)
    cosc, sinc = cs(ang_c)
    log_g = jnp.log(1.0 - 2.0 ** (-5.0 - jnp.arange(GROUP_HEADS, dtype=F32)))
    idx = jnp.arange(CHUNK, dtype=F32)
    dist = idx[:, None] - idx[None, :]
    dmat = jnp.where((dist >= 0)[None], jnp.exp(jnp.maximum(dist, 0.0)[None] * log_g[:, None, None]), 0.0)
    zeta = jnp.exp((CHUNK - 1 - idx)[:, None] * log_g)[:, _HEAD_PERM]
    xi = jnp.exp((idx + 1.0)[:, None] * log_g)[:, _HEAD_NAT]
    gch = jnp.exp(CHUNK * log_g)[_HEAD_NAT][None, :]
    return dict(
        cosa=cosa, sina=sina, cosc=cosc, sinc=sinc, dmat=dmat, zeta=zeta, xi=xi, gch=gch,
        bd_perm=jnp.asarray(_HEAD_PERM[:, None] == _HEAD_PERM[None, :], BF16),
        bd_nat=jnp.asarray(_HEAD_NAT[:, None] == _HEAD_NAT[None, :], BF16),
        rmask=jnp.asarray(_HEAD_PERM[:, None] == _HEAD_NAT[None, :], F32),
        smask=jnp.asarray((_LANE[:, None] // SSM_STATE) == (_HEAD_NAT[None, :] // (GROUP_HEADS // SSM_GROUPS)), F32),
        qmask=jnp.asarray(_head_mask(_HEAD_PERM)),
        vmask=jnp.asarray(_head_mask(_HEAD_NAT)),
        ltri=jnp.asarray(np.tril(np.ones((CHUNK, CHUNK), np.float32)), BF16),
    )


def _build_w_cat(w_in):
    w = w_in
    depth = w.shape[0]

    def nat(c0):
        return w[:, :, c0:c0 + 256]

    def perm(c0):
        blk = w[:, :, c0:c0 + 256].reshape(depth, D_MODEL, GROUP_HEADS, 2, 32)
        return blk.transpose(0, 1, 3, 2, 4).reshape(depth, D_MODEL, 256)

    dt = jnp.repeat(w[:, :, _BDT:_BDT + GROUP_HEADS], HEAD_DIM, axis=2)
    return jnp.concatenate([
        perm(_A0), perm(_A0 + 256), nat(_A0 + 512),
        nat(_BZ), nat(_BX), nat(_BB), nat(_BC), dt,
        perm(_C0), perm(_C0 + 256), nat(_C0 + 512), nat(_C0 + 768),
        nat(_D0), nat(_D0 + 256)], axis=2).astype(BF16)


def kernel(x, p, norm_mix, w_in, attn_q_norm, attn_k_norm, ssm_conv_w, ssm_conv_b, ssm_dt_bias,
           ssm_a_log, ssm_d, ssm_norm, conv_dw_w, conv_dw_b, conv_ln_g, conv_ln_b, w_out,
           norm_ffn, w_up, w_down, norm_ple, w_ple, w_ple_gate):
    bsz, seq, _ = x.shape
    depth = w_in.shape[0]
    tm = 512
    tb = _tables(seq)
    row = lambda a: a[:, None, :]
    w_cat = _build_w_cat(w_in)
    g_mix, g_ffn, g_ple = row(norm_mix), row(norm_ffn), row(norm_ple)
    gq, gk = row(attn_q_norm[:, _PERM % HEAD_DIM]), row(attn_k_norm[:, _PERM % HEAD_DIM])
    cw = jnp.pad(ssm_conv_w, ((0, 0), (0, 8 - SSM_CONV), (0, 0)))
    dw = jnp.pad(conv_dw_w, ((0, 0), (0, 32 - CONF_KERNEL), (0, 0)))
    dtb, alog, dsk = row(ssm_dt_bias[:, _HEAD_NAT]), row(ssm_a_log[:, _HEAD_NAT]), row(ssm_d[:, _HEAD_NAT])
    wout, wup, wdn = w_out.astype(BF16), w_up.astype(BF16), w_down.astype(BF16)
    wple, wgate = w_ple.astype(BF16), w_ple_gate.astype(BF16)
    p_all = p.reshape(depth, bsz * seq, PLE_DIM)
    x2d = x.reshape(bsz * seq, D_MODEL)
    for i in range(depth):
        oa, ob, oc, yd = _inproj(x2d, g_mix, w_cat, tb["cosa"], tb["sina"], tb["cosc"], tb["sinc"],
                                 gq, gk, tb["bd_perm"], dw, row(conv_dw_b), row(conv_ln_g), row(conv_ln_b),
                                 seq, tm, i)
        ya = _attention(oa, tb["qmask"], bsz, seq)
        yb = _ssd(ob, cw, row(ssm_conv_b), dtb, alog, dsk, row(ssm_norm), tb["ltri"], tb["smask"],
                  tb["vmask"], bsz, seq, i)
        yc = _retention(oc, tb["dmat"].reshape(GROUP_HEADS * CHUNK, CHUNK), tb["zeta"], tb["xi"], tb["gch"], tb["bd_nat"], tb["rmask"],
                        tb["qmask"], tb["vmask"], bsz, seq)
        x2d = _ffn(x2d, ya, yb, yc, yd, p_all, wout, g_ffn, wup, wdn, g_ple, wple, wgate, tm, i)
    return x2d.reshape(bsz, seq, D_MODEL)
```

```python
import functools

import numpy as np
import jax
import jax.numpy as jnp
from jax import lax
from jax.experimental import pallas as pl
from jax.experimental.pallas import tpu as pltpu

F32 = jnp.float32
BF16 = jnp.bfloat16

D_MODEL = 1024
GROUP_WIDTH = 256
GROUP_HEADS = 4
HEAD_DIM = 64
EPS = 1e-6
ATTN_PATTERNS = ((128, 1), (512, 4), (2048, 16))
ATTN_BLOCK = 128
ROPE_THETA = 10000.0
SSM_STATE = 128
SSM_GROUPS = 2
SSM_CONV = 4
CHUNK = 128
CONF_KERNEL = 31
D_FF = 4 * D_MODEL
PLE_DIM = 256
NEG = -1e30
LOG2E = 1.4426950408889634
CONF_HALO = 32
CONF_ROWS = 64

_A0 = 0
_B0 = 3 * GROUP_WIDTH
_BZ, _BX, _BB, _BC = _B0, _B0 + 256, _B0 + 512, _B0 + 768
_BDT = _B0 + 1024
_C0 = _BDT + GROUP_HEADS
_D0 = _C0 + 4 * GROUP_WIDTH
IN_COLS = _D0 + 2 * GROUP_WIDTH

_LANE = np.arange(GROUP_WIDTH)
_PERM = ((_LANE % 128) // 32) * HEAD_DIM + (_LANE // 128) * 32 + (_LANE % 32)
_HEAD_PERM = (_LANE % 128) // 32
_HEAD_NAT = _LANE // HEAD_DIM

N_PROJ = 14 * GROUP_WIDTH


def _sigmoid(x):
    return 1.0 / (1.0 + jnp.exp(-x))


def _silu(x):
    return x * _sigmoid(x)


def _softplus(x):
    return jnp.maximum(x, 0.0) + jnp.log(1.0 + jnp.exp(-jnp.abs(x)))


def _dot(a, b):
    return jnp.dot(a, b, preferred_element_type=F32)


def _dot_nt(a, b):
    return lax.dot_general(a, b, (((1,), (1,)), ((), ())), preferred_element_type=F32)


def _dot_tn(a, b):
    return lax.dot_general(a, b, (((0,), (0,)), ((), ())), preferred_element_type=F32)


def _inproj_kernel(x_ref, g_ref, w_ref, cosa_ref, sina_ref, cosc_ref, sinc_ref,
                   gq_ref, gk_ref, bd_ref, dw_ref, db_ref, lg_ref, lb_ref,
                   oa_ref, ob_ref, oc_ref, yd_ref, hb0_ref, hb1_ref, ext_ref, *, nseq):
    step = pl.program_id(0)
    tm = oa_ref.shape[0]

    def norm_into(dst_ref):
        x = x_ref[...]
        h = x * lax.rsqrt(jnp.mean(x * x, axis=-1, keepdims=True) + EPS) * g_ref[...]
        dst_ref[...] = h.astype(BF16)

    @pl.when(step == 0)
    def _():
        norm_into(hb0_ref)
        ext_ref[CONF_HALO + tm:CONF_HALO + tm + 8, :] = jnp.zeros((8, 256), F32)

    @pl.when(step % nseq == 1)
    def _():
        ext_ref[0:CONF_HALO, :] = jnp.zeros((CONF_HALO, 256), F32)

    rest = (w_ref, cosa_ref, sina_ref, cosc_ref, sinc_ref, gq_ref, gk_ref, bd_ref, dw_ref, db_ref, lg_ref,
            lb_ref, oa_ref, ob_ref, oc_ref, yd_ref, ext_ref)

    @pl.when((step > 0) & (step % 2 == 0))
    def _():
        norm_into(hb0_ref)
        _inproj_body(hb1_ref, *rest)

    @pl.when(step % 2 == 1)
    def _():
        norm_into(hb1_ref)
        _inproj_body(hb0_ref, *rest)


def _inproj_body(hb_ref, w_ref, cosa_ref, sina_ref, cosc_ref, sinc_ref, gq_ref, gk_ref, bd_ref,
                 dw_ref, db_ref, lg_ref, lb_ref, oa_ref, ob_ref, oc_ref, yd_ref, ext_ref):
    tm = oa_ref.shape[0]

    def mm(j):
        return _dot(hb_ref[...], w_ref[:, j * 256:(j + 1) * 256])

    def rot(t, cos, sin):
        t1, t2 = t[:, :128], t[:, 128:]
        return jnp.concatenate([t1 * cos - t2 * sin, t2 * cos + t1 * sin], axis=1)

    def headnorm(t, gain):
        ss = _dot((t * t).astype(BF16), bd_ref[...])
        return t * lax.rsqrt(ss * (1.0 / HEAD_DIM) + EPS) * gain

    cosa, sina = cosa_ref[...], sina_ref[...]
    cosc, sinc = cosc_ref[...], sinc_ref[...]
    scale = HEAD_DIM ** -0.5
    ext_ref[CONF_HALO:CONF_HALO + tm, :] = mm(12) * _sigmoid(mm(13))
    conv_chunks = iter(range(tm // CONF_ROWS))

    def conv_some(n):
        for _ in range(n):
            c = next(conv_chunks, None)
            if c is not None:
                _conformer_rows(ext_ref, dw_ref, db_ref, lg_ref, lb_ref, yd_ref, c)

    oa_ref[:, 0:256] = mm(0)
    conv_some(1)
    oa_ref[:, 256:512] = mm(1)
    conv_some(1)
    oa_ref[:, 512:768] = mm(2)
    conv_some(1)
    for j in range(5):
        ob_ref[:, j * 256:(j + 1) * 256] = mm(3 + j)
        conv_some(1)
    oa_ref[:, 0:256] = rot(headnorm(oa_ref[:, 0:256], gq_ref[...]), cosa, sina) * (scale * LOG2E)
    oa_ref[:, 256:512] = rot(headnorm(oa_ref[:, 256:512], gk_ref[...]), cosa, sina)
    oc_ref[:, 0:256] = rot(mm(8), cosc, sinc)
    oc_ref[:, 256:512] = rot(mm(9), cosc, sinc) * scale
    oc_ref[:, 512:768] = mm(10)
    oc_ref[:, 768:1024] = mm(11)
    conv_some(tm // CONF_ROWS)
    ext_ref[0:CONF_HALO, :] = ext_ref[tm:tm + CONF_HALO, :]


def _conformer_rows(ext_ref, w_ref, b_ref, lg_ref, lb_ref, o_ref, c):
    nwin = CONF_ROWS + CONF_HALO + 8
    r0 = c * CONF_ROWS
    win = ext_ref[pl.ds(r0, nwin), :]
    acc = jnp.zeros((CONF_ROWS, 256), F32) + b_ref[...]
    for off in range(8):
        shifted = win if off == 0 else pltpu.roll(win, nwin - off, axis=0)
        for sh in range(CONF_HALO - CONF_KERNEL + 1, CONF_HALO + 1):
            if sh % 8 == off:
                k = sh - (CONF_HALO - CONF_KERNEL + 1)
                acc = acc + w_ref[k:k + 1, :] * shifted[sh - off:sh - off + CONF_ROWS, :]
    mu = jnp.mean(acc, axis=-1, keepdims=True)
    xc = acc - mu
    var = jnp.mean(xc * xc, axis=-1, keepdims=True)
    o_ref[pl.ds(r0, CONF_ROWS), :] = _silu(xc * lax.rsqrt(var + EPS) * lg_ref[...] + lb_ref[...])


def _inproj(x2d, g, w_cat, cosa, sina, cosc, sinc, gq, gk, bd, dw, db, lg, lb, seq, tm, layer):
    t = x2d.shape[0]
    nseq = seq // tm
    const = lambda i: (0, 0)
    lay = lambda i: (layer, 0, 0)
    nt = t // tm
    src = lambda i: (jnp.minimum(i, nt - 1), 0)
    tok = lambda i: (jnp.maximum(i - 1, 0), 0)
    pos = lambda i: (jnp.maximum(i - 1, 0) % nseq, 0)
    return pl.pallas_call(
        functools.partial(_inproj_kernel, nseq=nseq),
        grid=(nt + 1,),
        in_specs=[
            pl.BlockSpec((tm, D_MODEL), src),
            pl.BlockSpec((None, 1, D_MODEL), lay),
            pl.BlockSpec((None, D_MODEL, N_PROJ), lay),
            pl.BlockSpec((tm, 128), pos), pl.BlockSpec((tm, 128), pos),
            pl.BlockSpec((tm, 128), pos), pl.BlockSpec((tm, 128), pos),
            pl.BlockSpec((None, 1, 256), lay), pl.BlockSpec((None, 1, 256), lay),
            pl.BlockSpec((256, 256), const),
            pl.BlockSpec((None, 32, 256), lay), pl.BlockSpec((None, 1, 256), lay),
            pl.BlockSpec((None, 1, 256), lay), pl.BlockSpec((None, 1, 256), lay),
        ],
        out_specs=[
            pl.BlockSpec((tm, 768), tok), pl.BlockSpec((tm, 1280), tok),
            pl.BlockSpec((tm, 1024), tok), pl.BlockSpec((tm, 256), tok),
        ],
        out_shape=[
            jax.ShapeDtypeStruct((t, 768), F32), jax.ShapeDtypeStruct((t, 1280), F32),
            jax.ShapeDtypeStruct((t, 1024), F32), jax.ShapeDtypeStruct((t, 256), F32),
        ],
        scratch_shapes=[pltpu.VMEM((tm, D_MODEL), BF16), pltpu.VMEM((tm, D_MODEL), BF16),
                        pltpu.VMEM((tm + CONF_HALO + 8, 256), F32)],
        compiler_params=pltpu.CompilerParams(
            dimension_semantics=("arbitrary",), vmem_limit_bytes=56 * 1024 * 1024),
        name="inproj",
    )(x2d, g, w_cat, cosa, sina, cosc, sinc, gq, gk, bd, dw, db, lg, lb)


ATTN_TILE = 2048


def _attn_kernel(*refs):
    q_refs, kp_refs, kc_refs, vp_refs, vc_refs = (refs[2 * i:2 * i + 2] for i in range(5))
    qm_ref, o_ref = refs[10], refs[11]
    m_refs, l_refs, u_refs = refs[12:16], refs[16:20], refs[20:22]
    tile = pl.program_id(1)

    ii = lax.broadcasted_iota(jnp.int32, (128, 256), 0)
    jj = lax.broadcasted_iota(jnp.int32, (128, 256), 1)
    band_bias = jnp.where((jj >= ii) & (jj <= ii + ATTN_BLOCK), 0.0, NEG).astype(F32)
    noprev_bias = jnp.where(jj < ATTN_BLOCK, NEG, 0.0).astype(F32)
    first_bias = band_bias + jnp.where(tile == 0, 1.0, 0.0).astype(F32) * noprev_bias
    lo64 = lax.broadcasted_iota(jnp.int32, (128, 128), 1) < HEAD_DIM

    def rows(start, dil):
        if dil == 1:
            return pl.ds(pl.multiple_of(start, 128), 128)
        return pl.ds(start, 128, stride=dil)

    def wide(pair, rr):
        return jnp.concatenate([pair[0][rr, :], pair[1][rr, :]], axis=1)

    def blocks(descs, dil, init):
        st = []
        for qs, klo_refs, vlo_refs, lo_s, hi_s, bias in descs:
            qr, lo, hi = rows(qs, dil), rows(lo_s, dil), rows(hi_s, dil)
            q = wide(q_refs, qr)
            k = jnp.concatenate([wide(klo_refs, lo), wide(kc_refs, hi)], axis=0).astype(BF16)
            v = [jnp.concatenate([vlo_refs[c][lo, :], vc_refs[c][hi, :]], axis=0).astype(BF16) for c in range(2)]
            q4 = jnp.concatenate([(q * qm_ref[h:h + 1, :]).astype(BF16) for h in range(GROUP_HEADS)], axis=0)
            st.append(dict(qr=qr, v=v, bias=bias, s_all=_dot_nt(q4, k)))
        if not init:
            for d in st:
                d["m_old"] = [m_refs[h][d["qr"], :] for h in range(GROUP_HEADS)]
                d["l_old"] = [l_refs[h][d["qr"], :] for h in range(GROUP_HEADS)]
        for d in st:
            d["m_new"], d["l_new"], d["alpha"], ps = [], [], [], []
            for h in range(GROUP_HEADS):
                s = d["s_all"][h * 128:(h + 1) * 128, :] + d["bias"]
                rmax = jnp.max(s, axis=-1, keepdims=True)
                m_new = jnp.broadcast_to(rmax, (128, 128)) if init else jnp.maximum(d["m_old"][h], rmax)
                p = jnp.exp2(s - jnp.concatenate([m_new, m_new], axis=1))
                rsum = jnp.sum(p, axis=-1, keepdims=True)
                if init:
                    d["l_new"].append(jnp.broadcast_to(rsum, (128, 128)))
                else:
                    alpha = jnp.exp2(d["m_old"][h] - m_new)
                    d["l_new"].append(alpha * d["l_old"][h] + rsum)
                    d["alpha"].append(alpha)
                d["m_new"].append(m_new)
                ps.append(p.astype(BF16))
            d["pv"] = []
            for c in range(2):
                pv = _dot(jnp.concatenate(ps[2 * c:2 * c + 2], axis=0), d["v"][c])
                d["pv"].append(jnp.where(lo64, pv[0:128, :], pv[128:256, :]))
        if not init:
            for d in st:
                d["u_old"] = [u_refs[c][d["qr"], :] for c in range(2)]
        for d in st:
            for h in range(GROUP_HEADS):
                m_refs[h][d["qr"], :] = d["m_new"][h]
                l_refs[h][d["qr"], :] = d["l_new"][h]
            for c in range(2):
                if init:
                    u_refs[c][d["qr"], :] = d["pv"][c]
                else:
                    a_c = jnp.where(lo64, d["alpha"][2 * c], d["alpha"][2 * c + 1])
                    u_refs[c][d["qr"], :] = d["u_old"][c] * a_c + d["pv"][c]

    def first_desc(r, span):
        return (r, kp_refs, vp_refs, ATTN_TILE - span + r, r, first_bias)

    def later_desc(r, m, span):
        base = r + (m - 1) * span
        return (base + span, kc_refs, vc_refs, base, base + span, band_bias)

    for pi, (_, dil) in enumerate(reversed(ATTN_PATTERNS)):
        span = ATTN_BLOCK * dil
        nblk = ATTN_TILE // span
        init = pi == 0
        if nblk == 1:
            def class_pair(i, carry, dil=dil, span=span, init=init):
                blocks([first_desc(2 * i, span), first_desc(2 * i + 1, span)], dil, init)
                return carry

            lax.fori_loop(0, dil // 2, class_pair, 0)
        else:
            def per_class(r, carry, dil=dil, span=span, nblk=nblk, init=init):
                blocks([first_desc(r, span), later_desc(r, 1, span)], dil, init)

                def later_pair(i, c):
                    blocks([later_desc(r, 2 * i + 2, span), later_desc(r, 2 * i + 3, span)], dil, init)
                    return c

                lax.fori_loop(0, (nblk - 2) // 2, later_pair, 0)
                return carry

            lax.fori_loop(0, dil, per_class, 0)

    lo64w = lax.broadcasted_iota(jnp.int32, (256, 128), 1) < HEAD_DIM

    def finish(c, carry):
        rr = pl.ds(pl.multiple_of(c * 256, 256), 256)
        for hlf in range(2):
            den = jnp.where(lo64w, l_refs[2 * hlf][rr, :], l_refs[2 * hlf + 1][rr, :])
            o_ref[rr, hlf * 128:(hlf + 1) * 128] = u_refs[hlf][rr, :] / den
        return carry

    lax.fori_loop(0, ATTN_TILE // 256, finish, 0)


def _attention(qkv, qmask, bsz, seq):
    nt = seq // ATTN_TILE
    cur = lambda c: (lambda b, n: (b * nt + n, c))
    prev = lambda c: (lambda b, n: (b * nt + jnp.maximum(n - 1, 0), c))
    blk = (ATTN_TILE, 128)
    col_maps = [cur(0), cur(1),
                prev(2), prev(3), cur(2), cur(3),
                prev(4), prev(5), cur(4), cur(5)]
    return pl.pallas_call(
        _attn_kernel,
        grid=(bsz, nt),
        in_specs=[pl.BlockSpec(blk, m) for m in col_maps] + [pl.BlockSpec((8, 256), lambda b, n: (0, 0))],
        out_specs=pl.BlockSpec((ATTN_TILE, 256), lambda b, n: (b * nt + n, 0)),
        out_shape=jax.ShapeDtypeStruct((bsz * seq, GROUP_WIDTH), F32),
        scratch_shapes=[pltpu.VMEM((ATTN_TILE, 128), F32)] * 10,
        compiler_params=pltpu.CompilerParams(
            dimension_semantics=("arbitrary", "arbitrary"), vmem_limit_bytes=56 * 1024 * 1024),
        name="attn",
    )(*([qkv] * 10), qmask)


SEQ_TILE = 1024
SSM_ROWS = 64


def _ssd_kernel(u_ref, cw_ref, cb_ref, dtb_ref, a_ref, dsk_ref, ng_ref, ltri_ref, smask_ref,
                vm_ref, o_ref, ext_ref, act_ref, hs_ref):
    ts = SEQ_TILE

    @pl.when(pl.program_id(1) == 0)
    def _():
        ext_ref[0:8, :] = jnp.zeros((8, 768), F32)
        hs_ref[...] = jnp.zeros(hs_ref.shape, F32)

    ext_ref[8:8 + ts, :] = u_ref[:, 256:1024]
    for c in range(ts // SSM_ROWS):
        for gc in range(3):
            cols = slice(gc * 256, (gc + 1) * 256)
            win = ext_ref[pl.ds(c * SSM_ROWS, SSM_ROWS + 8), cols]
            acc = cb_ref[:, cols] + cw_ref[SSM_CONV - 1:SSM_CONV, cols] * win[8:8 + SSM_ROWS, :]
            for sh in range(1, SSM_CONV):
                k = SSM_CONV - 1 - sh
                acc = acc + cw_ref[k:k + 1, cols] * pltpu.roll(win, sh, axis=0)[8:8 + SSM_ROWS, :]
            act_ref[pl.ds(c * SSM_ROWS, SSM_ROWS), cols] = _silu(acc)
    ext_ref[0:8, :] = ext_ref[ts:ts + 8, :]

    ii = lax.broadcasted_iota(jnp.int32, (CHUNK, CHUNK), 0)
    jj = lax.broadcasted_iota(jnp.int32, (CHUNK, CHUNK), 1)
    causal = jj <= ii
    ltri = ltri_ref[...]

    hs = hs_ref[...]
    for c in range(ts // CHUNK):
        rr = pl.ds(c * CHUNK, CHUNK)
        z = u_ref[rr, 0:256]
        xs = act_ref[rr, 0:256]
        bm = act_ref[rr, 256:512]
        cm = act_ref[rr, 512:768]
        dt = _softplus(u_ref[rr, 1024:1280] + dtb_ref[...])
        da = dt * (-jnp.exp(a_ref[...]))
        da_hi = da.astype(BF16)
        rem = da - da_hi.astype(F32)
        da_mid = rem.astype(BF16)
        da_lo = (rem - da_mid.astype(F32)).astype(BF16)
        acum = _dot(ltri, da_hi) + _dot(ltri, da_mid) + _dot(ltri, da_lo)
        total = acum[CHUNK - 1:CHUNK, :]
        bmb = bm.astype(BF16)
        cmb = cm.astype(BF16)
        xdt = xs * dt
        acum_t = (jnp.transpose(acum[:, 0:128]), jnp.transpose(acum[:, 128:256]))
        cbs = [_dot_nt(cmb[:, g * 128:(g + 1) * 128], bmb[:, g * 128:(g + 1) * 128]) for g in range(SSM_GROUPS)]
        mhs = []
        for h in range(GROUP_HEADS):
            col = acum[:, h * HEAD_DIM:h * HEAD_DIM + 1]
            rsel = (h % 2) * HEAD_DIM
            row = acum_t[h // 2][rsel:rsel + 1, :]
            lmat = jnp.exp(jnp.where(causal, col - row, NEG))
            mhs.append((cbs[h // 2] * lmat).astype(BF16))
        xdt4 = jnp.concatenate([(xdt * vm_ref[h:h + 1, :]).astype(BF16) for h in range(GROUP_HEADS)], axis=0)
        y = _dot(cmb, hs.astype(BF16)) * jnp.exp(acum) + _dot(jnp.concatenate(mhs, axis=1), xdt4)
        xw = (xs * (jnp.exp(total - acum) * dt)).astype(BF16)
        hs = jnp.exp(total) * hs + _dot_tn(bmb, xw) * smask_ref[...]
        y = (y + dsk_ref[...] * xs) * _silu(z)
        halves = []
        for g in range(SSM_GROUPS):
            yg = y[:, g * 128:(g + 1) * 128]
            halves.append(yg * lax.rsqrt(jnp.mean(yg * yg, axis=-1, keepdims=True) + EPS))
        o_ref[rr, :] = jnp.concatenate(halves, axis=1) * ng_ref[...]
    hs_ref[...] = hs


def _ssd(ub, cw, cb, dtb, a_exp, dsk, ng, ltri, smask, vmask, bsz, seq, layer):
    ts = SEQ_TILE
    nt = seq // ts
    const = lambda b, n: (0, 0)
    lay = lambda b, n: (layer, 0, 0)
    tok = lambda b, n: (b * nt + n, 0)
    return pl.pallas_call(
        _ssd_kernel,
        grid=(bsz, nt),
        in_specs=[
            pl.BlockSpec((ts, 1280), tok),
            pl.BlockSpec((None, 8, 768), lay), pl.BlockSpec((None, 1, 768), lay),
            pl.BlockSpec((None, 1, 256), lay), pl.BlockSpec((None, 1, 256), lay),
            pl.BlockSpec((None, 1, 256), lay), pl.BlockSpec((None, 1, 256), lay),
            pl.BlockSpec((CHUNK, CHUNK), const), pl.BlockSpec((256, 256), const),
            pl.BlockSpec((8, 256), const),
        ],
        out_specs=pl.BlockSpec((ts, 256), tok),
        out_shape=jax.ShapeDtypeStruct((bsz * seq, GROUP_WIDTH), F32),
        scratch_shapes=[pltpu.VMEM((ts + 8, 768), F32), pltpu.VMEM((ts, 768), F32), pltpu.VMEM((256, 256), F32)],
        compiler_params=pltpu.CompilerParams(dimension_semantics=("arbitrary", "arbitrary")),
        name="ssd",
    )(ub, cw, cb, dtb, a_exp, dsk, ng, ltri, smask, vmask)


def _ret_kernel(u_ref, dmat_ref, zeta_ref, xi_ref, gch_ref, bd_ref, rmask_ref, qm_ref, vm_ref,
                o_ref, r_ref):
    @pl.when(pl.program_id(1) == 0)
    def _():
        r_ref[...] = jnp.zeros(r_ref.shape, F32)

    r = r_ref[...]
    for c in range(SEQ_TILE // CHUNK):
        rr = pl.ds(c * CHUNK, CHUNK)
        q = u_ref[rr, 0:256]
        k = u_ref[rr, 256:512]
        v = u_ref[rr, 512:768]
        g = u_ref[rr, 768:1024]
        kb = k.astype(BF16)
        q4 = jnp.concatenate([(q * qm_ref[h:h + 1, :]).astype(BF16) for h in range(GROUP_HEADS)], axis=0)
        s4 = _dot_nt(q4, kb) * dmat_ref[...]
        s_cat = jnp.concatenate([s4[h * CHUNK:(h + 1) * CHUNK, :] for h in range(GROUP_HEADS)], axis=1)
        v4 = jnp.concatenate([(v * vm_ref[h:h + 1, :]).astype(BF16) for h in range(GROUP_HEADS)], axis=0)
        y = _dot(q.astype(BF16), r.astype(BF16)) * xi_ref[...] + _dot(s_cat.astype(BF16), v4)
        ss = _dot((y * y).astype(BF16), bd_ref[...])
        o_ref[rr, :] = y * lax.rsqrt(ss * (1.0 / HEAD_DIM) + EPS) * _silu(g)
        kz = (k * zeta_ref[...]).astype(BF16)
        r = gch_ref[...] * r + _dot_tn(kz, v.astype(BF16)) * rmask_ref[...]
    r_ref[...] = r


def _retention(uc, dmat, zeta, xi, gch, bdn, rmask, qmask, vmask, bsz, seq):
    ts = SEQ_TILE
    nt = seq // ts
    const = lambda b, n: (0, 0)
    tok = lambda b, n: (b * nt + n, 0)
    return pl.pallas_call(
        _ret_kernel,
        grid=(bsz, nt),
        in_specs=[
            pl.BlockSpec((ts, 1024), tok),
            pl.BlockSpec((GROUP_HEADS * CHUNK, CHUNK), const),
            pl.BlockSpec((CHUNK, 256), const), pl.BlockSpec((CHUNK, 256), const),
            pl.BlockSpec((1, 256), const), pl.BlockSpec((256, 256), const),
            pl.BlockSpec((256, 256), const),
            pl.BlockSpec((8, 256), const), pl.BlockSpec((8, 256), const),
        ],
        out_specs=pl.BlockSpec((ts, 256), tok),
        out_shape=jax.ShapeDtypeStruct((bsz * seq, GROUP_WIDTH), F32),
        scratch_shapes=[pltpu.VMEM((256, 256), F32)],
        compiler_params=pltpu.CompilerParams(dimension_semantics=("arbitrary", "arbitrary")),
        name="retention",
    )(uc, dmat, zeta, xi, gch, bdn, rmask, qmask, vmask)


FF_CHUNK = 1024


def _rms_rows(x, g):
    return x * lax.rsqrt(jnp.mean(x * x, axis=-1, keepdims=True) + EPS) * g


def _ffn_kernel(x_ref, ya_ref, yb_ref, yc_ref, yd_ref, p_ref, wout_ref, gf_ref, wup_ref, wdn_ref,
                gp_ref, wple_ref, wgate_ref, o_ref, hb_ref):
    tm = x_ref.shape[0]
    halves = [pl.ds(0, tm // 2), pl.ds(tm // 2, tm // 2)]
    x1 = []
    for rr in halves:
        acc = x_ref[rr, :]
        for m, y_ref in enumerate((ya_ref, yb_ref, yc_ref, yd_ref)):
            acc = acc + _dot(y_ref[rr, :].astype(BF16), wout_ref[m * 256:(m + 1) * 256, :])
        hb_ref[rr, :] = _rms_rows(acc, gf_ref[...]).astype(BF16)
        x1.append(acc)

    def down(rr, c):
        up = jnp.maximum(_dot(hb_ref[rr, :], wup_ref[:, c * FF_CHUNK:(c + 1) * FF_CHUNK]), 0.0)
        return _dot((up * up).astype(BF16), wdn_ref[c * FF_CHUNK:(c + 1) * FF_CHUNK, :])

    ffs = [down(rr, 0) for rr in halves]
    for c in range(1, D_FF // FF_CHUNK):
        ffs = [ff + down(rr, c) for ff, rr in zip(ffs, halves)]
    for rr, acc, ff in zip(halves, x1, ffs):
        x2 = acc + ff
        gate = _sigmoid(_dot(_rms_rows(x2, gp_ref[...]).astype(BF16), wgate_ref[...]))
        o_ref[rr, :] = x2 + _dot(p_ref[rr, :].astype(BF16), wple_ref[...]) * gate


def _ffn(x2d, ya, yb, yc, yd, p_all, wout, gf, wup, wdn, gp, wple, wgate, tm, layer):
    t = x2d.shape[0]
    tok = lambda i: (i, 0)
    lay = lambda i: (layer, 0, 0)
    once = dict(pipeline_mode=pl.Buffered(1))
    return pl.pallas_call(
        _ffn_kernel,
        grid=(t // tm,),
        in_specs=[
            pl.BlockSpec((tm, D_MODEL), tok),
            pl.BlockSpec((tm, 256), tok), pl.BlockSpec((tm, 256), tok),
            pl.BlockSpec((tm, 256), tok), pl.BlockSpec((tm, 256), tok),
            pl.BlockSpec((None, tm, PLE_DIM), lambda i: (layer, i, 0)),
            pl.BlockSpec((None, D_MODEL, D_MODEL), lay, **once),
            pl.BlockSpec((None, 1, D_MODEL), lay),
            pl.BlockSpec((None, D_MODEL, D_FF), lay, **once),
            pl.BlockSpec((None, D_FF, D_MODEL), lay, **once),
            pl.BlockSpec((None, 1, D_MODEL), lay),
            pl.BlockSpec((None, PLE_DIM, D_MODEL), lay, **once),
            pl.BlockSpec((None, D_MODEL, D_MODEL), lay, **once),
        ],
        out_specs=pl.BlockSpec((tm, D_MODEL), tok),
        out_shape=jax.ShapeDtypeStruct((t, D_MODEL), F32),
        scratch_shapes=[pltpu.VMEM((tm, D_MODEL), BF16)],
        compiler_params=pltpu.CompilerParams(
            dimension_semantics=("arbitrary",), vmem_limit_bytes=56 * 1024 * 1024),
        name="ffn",
    )(x2d, ya, yb, yc, yd, p_all, wout, gf, wup, wdn, gp, wple, wgate)


def _head_mask(head_of_lane):
    m = np.zeros((8, GROUP_WIDTH), np.float32)
    for h in range(GROUP_HEADS):
        m[h] = head_of_lane == h
    return m


def _tables(seq):
    pos = jnp.arange(seq, dtype=F32)
    ang_a = ROPE_THETA ** (-jnp.arange(0, HEAD_DIM, 2, dtype=F32) / HEAD_DIM)
    ang_c = 1.0 / (10000.0 ** jnp.linspace(0.0, 1.0, HEAD_DIM // 2, dtype=F32))

    def cs(ang):
        a = pos[:, None] * ang[None, :]
        return jnp.tile(jnp.cos(a), (1, 4)), jnp.tile(jnp.sin(a), (1, 4))

    cosa, sina = cs(ang_a)
    cosc, sinc = cs(ang_c)
    log_g = jnp.log(1.0 - 2.0 ** (-5.0 - jnp.arange(GROUP_HEADS, dtype=F32)))
    idx = jnp.arange(CHUNK, dtype=F32)
    dist = idx[:, None] - idx[None, :]
    dmat = jnp.where((dist >= 0)[None], jnp.exp(jnp.maximum(dist, 0.0)[None] * log_g[:, None, None]), 0.0)
    zeta = jnp.exp((CHUNK - 1 - idx)[:, None] * log_g)[:, _HEAD_PERM]
    xi = jnp.exp((idx + 1.0)[:, None] * log_g)[:, _HEAD_NAT]
    gch = jnp.exp(CHUNK * log_g)[_HEAD_NAT][None, :]
    return dict(
        cosa=cosa, sina=sina, cosc=cosc, sinc=sinc, dmat=dmat, zeta=zeta, xi=xi, gch=gch,
        bd_perm=jnp.asarray(_HEAD_PERM[:, None] == _HEAD_PERM[None, :], BF16),
        bd_nat=jnp.asarray(_HEAD_NAT[:, None] == _HEAD_NAT[None, :], BF16),
        rmask=jnp.asarray(_HEAD_PERM[:, None] == _HEAD_NAT[None, :], F32),
        smask=jnp.asarray((_LANE[:, None] // SSM_STATE) == (_HEAD_NAT[None, :] // (GROUP_HEADS // SSM_GROUPS)), F32),
        qmask=jnp.asarray(_head_mask(_HEAD_PERM)),
        vmask=jnp.asarray(_head_mask(_HEAD_NAT)),
        ltri=jnp.asarray(np.tril(np.ones((CHUNK, CHUNK), np.float32)), BF16),
    )


def _build_w_cat(w_in):
    w = w_in
    depth = w.shape[0]

    def nat(c0):
        return w[:, :, c0:c0 + 256]

    def perm(c0):
        blk = w[:, :, c0:c0 + 256].reshape(depth, D_MODEL, GROUP_HEADS, 2, 32)
        return blk.transpose(0, 1, 3, 2, 4).reshape(depth, D_MODEL, 256)

    dt = jnp.repeat(w[:, :, _BDT:_BDT + GROUP_HEADS], HEAD_DIM, axis=2)
    return jnp.concatenate([
        perm(_A0), perm(_A0 + 256), nat(_A0 + 512),
        nat(_BZ), nat(_BX), nat(_BB), nat(_BC), dt,
        perm(_C0), perm(_C0 + 256), nat(_C0 + 512), nat(_C0 + 768),
        nat(_D0), nat(_D0 + 256)], axis=2).astype(BF16)


def kernel(x, p, norm_mix, w_in, attn_q_norm, attn_k_norm, ssm_conv_w, ssm_conv_b, ssm_dt_bias,
           ssm_a_log, ssm_d, ssm_norm, conv_dw_w, conv_dw_b, conv_ln_g, conv_ln_b, w_out,
           norm_ffn, w_up, w_down, norm_ple, w_ple, w_ple_gate):
    bsz, seq, _ = x.shape
    depth = w_in.shape[0]
    tm = 512
    tb = _tables(seq)
    row = lambda a: a[:, None, :]
    w_cat = _build_w_cat(w_in)
    g_mix, g_ffn, g_ple = row(norm_mix), row(norm_ffn), row(norm_ple)
    gq, gk = row(attn_q_norm[:, _PERM % HEAD_DIM]), row(attn_k_norm[:, _PERM % HEAD_DIM])
    cw = jnp.pad(ssm_conv_w, ((0, 0), (0, 8 - SSM_CONV), (0, 0)))
    dw = jnp.pad(conv_dw_w, ((0, 0), (0, 32 - CONF_KERNEL), (0, 0)))
    dtb, alog, dsk = row(ssm_dt_bias[:, _HEAD_NAT]), row(ssm_a_log[:, _HEAD_NAT]), row(ssm_d[:, _HEAD_NAT])
    wout, wup, wdn = w_out.astype(BF16), w_up.astype(BF16), w_down.astype(BF16)
    wple, wgate = w_ple.astype(BF16), w_ple_gate.astype(BF16)
    p_all = p.reshape(depth, bsz * seq, PLE_DIM)
    x2d = x.reshape(bsz * seq, D_MODEL)
    for i in range(depth):
        oa, ob, oc, yd = _inproj(x2d, g_mix, w_cat, tb["cosa"], tb["sina"], tb["cosc"], tb["sinc"],
                                 gq, gk, tb["bd_perm"], dw, row(conv_dw_b), row(conv_ln_g), row(conv_ln_b),
                                 seq, tm, i)
        ya = _attention(oa, tb["qmask"], bsz, seq)
        yb = _ssd(ob, cw, row(ssm_conv_b), dtb, alog, dsk, row(ssm_norm), tb["ltri"], tb["smask"],
                  tb["vmask"], bsz, seq, i)
        yc = _retention(oc, tb["dmat"].reshape(GROUP_HEADS * CHUNK, CHUNK), tb["zeta"], tb["xi"], tb["gch"], tb["bd_nat"], tb["rmask"],
                        tb["qmask"], tb["vmask"], bsz, seq)
        x2d = _ffn(x2d, ya, yb, yc, yd, p_all, wout, g_ffn, wup, wdn, g_ple, wple, wgate, tm, i)
    return x2d.reshape(bsz, seq, D_MODEL)
```

```python
import functools

import numpy as np
import jax
import jax.numpy as jnp
from jax import lax
from jax.experimental import pallas as pl
from jax.experimental.pallas import tpu as pltpu

F32 = jnp.float32
BF16 = jnp.bfloat16

D_MODEL = 1024
GROUP_WIDTH = 256
GROUP_HEADS = 4
HEAD_DIM = 64
EPS = 1e-6
ATTN_PATTERNS = ((128, 1), (512, 4), (2048, 16))
ATTN_BLOCK = 128
ROPE_THETA = 10000.0
SSM_STATE = 128
SSM_GROUPS = 2
SSM_CONV = 4
CHUNK = 128
CONF_KERNEL = 31
D_FF = 4 * D_MODEL
PLE_DIM = 256
NEG = -1e30
LOG2E = 1.4426950408889634
CONF_HALO = 32
CONF_ROWS = 64

_A0 = 0
_B0 = 3 * GROUP_WIDTH
_BZ, _BX, _BB, _BC = _B0, _B0 + 256, _B0 + 512, _B0 + 768
_BDT = _B0 + 1024
_C0 = _BDT + GROUP_HEADS
_D0 = _C0 + 4 * GROUP_WIDTH
IN_COLS = _D0 + 2 * GROUP_WIDTH

_LANE = np.arange(GROUP_WIDTH)
_PERM = ((_LANE % 128) // 32) * HEAD_DIM + (_LANE // 128) * 32 + (_LANE % 32)
_HEAD_PERM = (_LANE % 128) // 32
_HEAD_NAT = _LANE // HEAD_DIM

N_PROJ = 14 * GROUP_WIDTH


def _sigmoid(x):
    return 1.0 / (1.0 + jnp.exp(-x))


def _silu(x):
    return x * _sigmoid(x)


def _softplus(x):
    return jnp.maximum(x, 0.0) + jnp.log(1.0 + jnp.exp(-jnp.abs(x)))


def _dot(a, b):
    return jnp.dot(a, b, preferred_element_type=F32)


def _dot_nt(a, b):
    return lax.dot_general(a, b, (((1,), (1,)), ((), ())), preferred_element_type=F32)


def _dot_tn(a, b):
    return lax.dot_general(a, b, (((0,), (0,)), ((), ())), preferred_element_type=F32)


def _inproj_kernel(x_ref, g_ref, w_ref, cosa_ref, sina_ref, cosc_ref, sinc_ref,
                   gq_ref, gk_ref, bd_ref, dw_ref, db_ref, lg_ref, lb_ref,
                   oa_ref, ob_ref, oc_ref, yd_ref, hb0_ref, hb1_ref, ext_ref, *, nseq):
    step = pl.program_id(0)
    tm = oa_ref.shape[0]

    def norm_into(dst_ref):
        x = x_ref[...]
        h = x * lax.rsqrt(jnp.mean(x * x, axis=-1, keepdims=True) + EPS) * g_ref[...]
        dst_ref[...] = h.astype(BF16)

    @pl.when(step == 0)
    def _():
        norm_into(hb0_ref)
        ext_ref[CONF_HALO + tm:CONF_HALO + tm + 8, :] = jnp.zeros((8, 256), F32)

    @pl.when(step % nseq == 1)
    def _():
        ext_ref[0:CONF_HALO, :] = jnp.zeros((CONF_HALO, 256), F32)

    rest = (w_ref, cosa_ref, sina_ref, cosc_ref, sinc_ref, gq_ref, gk_ref, bd_ref, dw_ref, db_ref, lg_ref,
            lb_ref, oa_ref, ob_ref, oc_ref, yd_ref, ext_ref)

    @pl.when((step > 0) & (step % 2 == 0))
    def _():
        norm_into(hb0_ref)
        _inproj_body(hb1_ref, *rest)

    @pl.when(step % 2 == 1)
    def _():
        norm_into(hb1_ref)
        _inproj_body(hb0_ref, *rest)


def _inproj_body(hb_ref, w_ref, cosa_ref, sina_ref, cosc_ref, sinc_ref, gq_ref, gk_ref, bd_ref,
                 dw_ref, db_ref, lg_ref, lb_ref, oa_ref, ob_ref, oc_ref, yd_ref, ext_ref):
    tm = oa_ref.shape[0]

    def mm(j):
        return _dot(hb_ref[...], w_ref[:, j * 256:(j + 1) * 256])

    def rot(t, cos, sin):
        t1, t2 = t[:, :128], t[:, 128:]
        return jnp.concatenate([t1 * cos - t2 * sin, t2 * cos + t1 * sin], axis=1)

    def headnorm(t, gain):
        ss = _dot((t * t).astype(BF16), bd_ref[...])
        return t * lax.rsqrt(ss * (1.0 / HEAD_DIM) + EPS) * gain

    cosa, sina = cosa_ref[...], sina_ref[...]
    cosc, sinc = cosc_ref[...], sinc_ref[...]
    scale = HEAD_DIM ** -0.5
    ext_ref[CONF_HALO:CONF_HALO + tm, :] = mm(12) * _sigmoid(mm(13))
    conv_chunks = iter(range(tm // CONF_ROWS))

    def conv_some(n):
        for _ in range(n):
            c = next(conv_chunks, None)
            if c is not None:
                _conformer_rows(ext_ref, dw_ref, db_ref, lg_ref, lb_ref, yd_ref, c)

    oa_ref[:, 0:256] = mm(0)
    conv_some(1)
    oa_ref[:, 256:512] = mm(1)
    conv_some(1)
    oa_ref[:, 512:768] = mm(2)
    conv_some(1)
    for j in range(5):
        ob_ref[:, j * 256:(j + 1) * 256] = mm(3 + j)
        conv_some(1)
    oa_ref[:, 0:256] = rot(headnorm(oa_ref[:, 0:256], gq_ref[...]), cosa, sina) * (scale * LOG2E)
    oa_ref[:, 256:512] = rot(headnorm(oa_ref[:, 256:512], gk_ref[...]), cosa, sina)
    oc_ref[:, 0:256] = rot(mm(8), cosc, sinc)
    oc_ref[:, 256:512] = rot(mm(9), cosc, sinc) * scale
    oc_ref[:, 512:768] = mm(10)
    oc_ref[:, 768:1024] = mm(11)
    conv_some(tm // CONF_ROWS)
    ext_ref[0:CONF_HALO, :] = ext_ref[tm:tm + CONF_HALO, :]


def _conformer_rows(ext_ref, w_ref, b_ref, lg_ref, lb_ref, o_ref, c):
    nwin = CONF_ROWS + CONF_HALO + 8
    r0 = c * CONF_ROWS
    win = ext_ref[pl.ds(r0, nwin), :]
    acc = jnp.zeros((CONF_ROWS, 256), F32) + b_ref[...]
    for off in range(8):
        shifted = win if off == 0 else pltpu.roll(win, nwin - off, axis=0)
        for sh in range(CONF_HALO - CONF_KERNEL + 1, CONF_HALO + 1):
            if sh % 8 == off:
                k = sh - (CONF_HALO - CONF_KERNEL + 1)
                acc = acc + w_ref[k:k + 1, :] * shifted[sh - off:sh - off + CONF_ROWS, :]
    mu = jnp.mean(acc, axis=-1, keepdims=True)
    xc = acc - mu
    var = jnp.mean(xc * xc, axis=-1, keepdims=True)
    o_ref[pl.ds(r0, CONF_ROWS), :] = _silu(xc * lax.rsqrt(var + EPS) * lg_ref[...] + lb_ref[...])


def _inproj(x2d, g, w_cat, cosa, sina, cosc, sinc, gq, gk, bd, dw, db, lg, lb, seq, tm, layer):
    t = x2d.shape[0]
    nseq = seq // tm
    const = lambda i: (0, 0)
    lay = lambda i: (layer, 0, 0)
    nt = t // tm
    src = lambda i: (jnp.minimum(i, nt - 1), 0)
    tok = lambda i: (jnp.maximum(i - 1, 0), 0)
    pos = lambda i: (jnp.maximum(i - 1, 0) % nseq, 0)
    return pl.pallas_call(
        functools.partial(_inproj_kernel, nseq=nseq),
        grid=(nt + 1,),
        in_specs=[
            pl.BlockSpec((tm, D_MODEL), src),
            pl.BlockSpec((None, 1, D_MODEL), lay),
            pl.BlockSpec((None, D_MODEL, N_PROJ), lay),
            pl.BlockSpec((tm, 128), pos), pl.BlockSpec((tm, 128), pos),
            pl.BlockSpec((tm, 128), pos), pl.BlockSpec((tm, 128), pos),
            pl.BlockSpec((None, 1, 256), lay), pl.BlockSpec((None, 1, 256), lay),
            pl.BlockSpec((256, 256), const),
            pl.BlockSpec((None, 32, 256), lay), pl.BlockSpec((None, 1, 256), lay),
            pl.BlockSpec((None, 1, 256), lay), pl.BlockSpec((None, 1, 256), lay),
        ],
        out_specs=[
            pl.BlockSpec((tm, 768), tok), pl.BlockSpec((tm, 1280), tok),
            pl.BlockSpec((tm, 1024), tok), pl.BlockSpec((tm, 256), tok),
        ],
        out_shape=[
            jax.ShapeDtypeStruct((t, 768), F32), jax.ShapeDtypeStruct((t, 1280), F32),
            jax.ShapeDtypeStruct((t, 1024), F32), jax.ShapeDtypeStruct((t, 256), F32),
        ],
        scratch_shapes=[pltpu.VMEM((tm, D_MODEL), BF16), pltpu.VMEM((tm, D_MODEL), BF16),
                        pltpu.VMEM((tm + CONF_HALO + 8, 256), F32)],
        compiler_params=pltpu.CompilerParams(
            dimension_semantics=("arbitrary",), vmem_limit_bytes=56 * 1024 * 1024),
        name="inproj",
    )(x2d, g, w_cat, cosa, sina, cosc, sinc, gq, gk, bd, dw, db, lg, lb)


ATTN_TILE = 2048
ATTN_GROUP = 4


def _attn_kernel(*refs):
    q_refs, kp_refs, kc_refs, vp_refs, vc_refs = (refs[2 * i:2 * i + 2] for i in range(5))
    qm_ref, o_ref = refs[10], refs[11]
    m_refs, l_refs, u_refs = refs[12:16], refs[16:20], refs[20:22]
    tile = pl.program_id(1)

    ii = lax.broadcasted_iota(jnp.int32, (128, 256), 0)
    jj = lax.broadcasted_iota(jnp.int32, (128, 256), 1)
    band_bias = jnp.where((jj >= ii) & (jj <= ii + ATTN_BLOCK), 0.0, NEG).astype(F32)
    noprev_bias = jnp.where(jj < ATTN_BLOCK, NEG, 0.0).astype(F32)
    first_bias = band_bias + jnp.where(tile == 0, 1.0, 0.0).astype(F32) * noprev_bias
    lo64 = lax.broadcasted_iota(jnp.int32, (128, 128), 1) < HEAD_DIM

    def rows(start, dil):
        if dil == 1:
            return pl.ds(pl.multiple_of(start, 128), 128)
        return pl.ds(start, 128, stride=dil)

    def wide(pair, rr):
        return jnp.concatenate([pair[0][rr, :], pair[1][rr, :]], axis=1)

    def blocks(descs, dil, init):
        st = []
        for qs, klo_refs, vlo_refs, lo_s, hi_s, bias in descs:
            qr, lo, hi = rows(qs, dil), rows(lo_s, dil), rows(hi_s, dil)
            q = wide(q_refs, qr)
            k = jnp.concatenate([wide(klo_refs, lo), wide(kc_refs, hi)], axis=0).astype(BF16)
            v = [jnp.concatenate([vlo_refs[c][lo, :], vc_refs[c][hi, :]], axis=0).astype(BF16) for c in range(2)]
            qb = q.astype(BF16)
            q4 = jnp.concatenate([qb * qm_ref[h:h + 1, :] for h in range(GROUP_HEADS)], axis=0)
            st.append(dict(qr=qr, v=v, bias=bias, s_all=_dot_nt(q4, k)))
        if not init:
            for d in st:
                d["m_old"] = [m_refs[h][d["qr"], :] for h in range(GROUP_HEADS)]
                d["l_old"] = [l_refs[h][d["qr"], :] for h in range(GROUP_HEADS)]
        for d in st:
            d["m_new"], d["l_new"], d["alpha"], ps = [], [], [], []
            for h in range(GROUP_HEADS):
                s = d["s_all"][h * 128:(h + 1) * 128, :] + d["bias"]
                rmax = jnp.max(s, axis=-1, keepdims=True)
                m_new = jnp.broadcast_to(rmax, (128, 128)) if init else jnp.maximum(d["m_old"][h], rmax)
                p = jnp.exp2(s - jnp.concatenate([m_new, m_new], axis=1))
                rsum = jnp.sum(p, axis=-1, keepdims=True)
                if init:
                    d["l_new"].append(jnp.broadcast_to(rsum, (128, 128)))
                else:
                    alpha = jnp.exp2(d["m_old"][h] - m_new)
                    d["l_new"].append(alpha * d["l_old"][h] + rsum)
                    d["alpha"].append(alpha)
                d["m_new"].append(m_new)
                ps.append(p.astype(BF16))
            d["pv"] = []
            for c in range(2):
                pv = _dot(jnp.concatenate(ps[2 * c:2 * c + 2], axis=0), d["v"][c])
                d["pv"].append(jnp.where(lo64, pv[0:128, :], pv[128:256, :]))
        if not init:
            for d in st:
                d["u_old"] = [u_refs[c][d["qr"], :] for c in range(2)]
        for d in st:
            for h in range(GROUP_HEADS):
                m_refs[h][d["qr"], :] = d["m_new"][h]
                l_refs[h][d["qr"], :] = d["l_new"][h]
            for c in range(2):
                if init:
                    u_refs[c][d["qr"], :] = d["pv"][c]
                else:
                    a_c = jnp.where(lo64, d["alpha"][2 * c], d["alpha"][2 * c + 1])
                    u_refs[c][d["qr"], :] = d["u_old"][c] * a_c + d["pv"][c]

    def first_desc(r, span):
        return (r, kp_refs, vp_refs, ATTN_TILE - span + r, r, first_bias)

    def later_desc(r, m, span):
        base = r + (m - 1) * span
        return (base + span, kc_refs, vc_refs, base, base + span, band_bias)

    for pi, (_, dil) in enumerate(reversed(ATTN_PATTERNS)):
        span = ATTN_BLOCK * dil
        nblk = ATTN_TILE // span
        init = pi == 0
        grp = ATTN_GROUP
        if nblk == 1:
            def class_group(i, carry, dil=dil, span=span, init=init):
                blocks([first_desc(grp * i + j, span) for j in range(grp)], dil, init)
                return carry

            lax.fori_loop(0, dil // grp, class_group, 0)
        else:
            def per_class(r, carry, dil=dil, span=span, nblk=nblk, init=init):
                blocks([first_desc(r, span)] + [later_desc(r, j, span) for j in range(1, grp)], dil, init)

                def later_group(i, c):
                    blocks([later_desc(r, grp * (i + 1) + j, span) for j in range(grp)], dil, init)
                    return c

                if nblk > grp:
                    lax.fori_loop(0, nblk // grp - 1, later_group, 0)
                return carry

            lax.fori_loop(0, dil, per_class, 0)

    lo64w = lax.broadcasted_iota(jnp.int32, (256, 128), 1) < HEAD_DIM

    def finish(c, carry):
        rr = pl.ds(pl.multiple_of(c * 256, 256), 256)
        for hlf in range(2):
            den = jnp.where(lo64w, l_refs[2 * hlf][rr, :], l_refs[2 * hlf + 1][rr, :])
            o_ref[rr, hlf * 128:(hlf + 1) * 128] = u_refs[hlf][rr, :] / den
        return carry

    lax.fori_loop(0, ATTN_TILE // 256, finish, 0)


def _attention(qkv, qmask, bsz, seq):
    nt = seq // ATTN_TILE
    cur = lambda c: (lambda b, n: (b * nt + n, c))
    prev = lambda c: (lambda b, n: (b * nt + jnp.maximum(n - 1, 0), c))
    blk = (ATTN_TILE, 128)
    col_maps = [cur(0), cur(1),
                prev(2), prev(3), cur(2), cur(3),
                prev(4), prev(5), cur(4), cur(5)]
    return pl.pallas_call(
        _attn_kernel,
        grid=(bsz, nt),
        in_specs=[pl.BlockSpec(blk, m) for m in col_maps] + [pl.BlockSpec((16, 256), lambda b, n: (0, 0))],
        out_specs=pl.BlockSpec((ATTN_TILE, 256), lambda b, n: (b * nt + n, 0)),
        out_shape=jax.ShapeDtypeStruct((bsz * seq, GROUP_WIDTH), F32),
        scratch_shapes=[pltpu.VMEM((ATTN_TILE, 128), F32)] * 10,
        compiler_params=pltpu.CompilerParams(
            dimension_semantics=("arbitrary", "arbitrary"), vmem_limit_bytes=56 * 1024 * 1024),
        name="attn",
    )(*([qkv] * 10), qmask)


SEQ_TILE = 1024
SSM_ROWS = 64


def _ssd_kernel(u_ref, cw_ref, cb_ref, dtb_ref, a_ref, dsk_ref, ng_ref, ltri_ref, smask_ref,
                vm_ref, o_ref, ext_ref, act_ref, hs_ref):
    ts = SEQ_TILE

    @pl.when(pl.program_id(1) == 0)
    def _():
        ext_ref[0:8, :] = jnp.zeros((8, 768), F32)
        hs_ref[...] = jnp.zeros(hs_ref.shape, F32)

    ext_ref[8:8 + ts, :] = u_ref[:, 256:1024]
    for c in range(ts // SSM_ROWS):
        for gc in range(3):
            cols = slice(gc * 256, (gc + 1) * 256)
            win = ext_ref[pl.ds(c * SSM_ROWS, SSM_ROWS + 8), cols]
            acc = cb_ref[:, cols] + cw_ref[SSM_CONV - 1:SSM_CONV, cols] * win[8:8 + SSM_ROWS, :]
            for sh in range(1, SSM_CONV):
                k = SSM_CONV - 1 - sh
                acc = acc + cw_ref[k:k + 1, cols] * pltpu.roll(win, sh, axis=0)[8:8 + SSM_ROWS, :]
            act_ref[pl.ds(c * SSM_ROWS, SSM_ROWS), cols] = _silu(acc)
    ext_ref[0:8, :] = ext_ref[ts:ts + 8, :]

    ii = lax.broadcasted_iota(jnp.int32, (CHUNK, CHUNK), 0)
    jj = lax.broadcasted_iota(jnp.int32, (CHUNK, CHUNK), 1)
    causal = jj <= ii
    ltri = ltri_ref[...]

    hs = hs_ref[...]
    for c in range(ts // CHUNK):
        rr = pl.ds(c * CHUNK, CHUNK)
        z = u_ref[rr, 0:256]
        xs = act_ref[rr, 0:256]
        bm = act_ref[rr, 256:512]
        cm = act_ref[rr, 512:768]
        dt = _softplus(u_ref[rr, 1024:1280] + dtb_ref[...])
        da = dt * (-jnp.exp(a_ref[...]))
        da_hi = da.astype(BF16)
        rem = da - da_hi.astype(F32)
        da_mid = rem.astype(BF16)
        da_lo = (rem - da_mid.astype(F32)).astype(BF16)
        acum = _dot(ltri, da_hi) + _dot(ltri, da_mid) + _dot(ltri, da_lo)
        total = acum[CHUNK - 1:CHUNK, :]
        bmb = bm.astype(BF16)
        cmb = cm.astype(BF16)
        xdt = xs * dt
        acum_t = (jnp.transpose(acum[:, 0:128]), jnp.transpose(acum[:, 128:256]))
        cbs = [_dot_nt(cmb[:, g * 128:(g + 1) * 128], bmb[:, g * 128:(g + 1) * 128]) for g in range(SSM_GROUPS)]
        mhs = []
        for h in range(GROUP_HEADS):
            col = acum[:, h * HEAD_DIM:h * HEAD_DIM + 1]
            rsel = (h % 2) * HEAD_DIM
            row = acum_t[h // 2][rsel:rsel + 1, :]
            lmat = jnp.exp(jnp.where(causal, col - row, NEG))
            mhs.append((cbs[h // 2] * lmat).astype(BF16))
        xdt4 = jnp.concatenate([(xdt * vm_ref[h:h + 1, :]).astype(BF16) for h in range(GROUP_HEADS)], axis=0)
        y = _dot(cmb, hs.astype(BF16)) * jnp.exp(acum) + _dot(jnp.concatenate(mhs, axis=1), xdt4)
        xw = (xs * (jnp.exp(total - acum) * dt)).astype(BF16)
        hs = jnp.exp(total) * hs + _dot_tn(bmb, xw) * smask_ref[...]
        y = (y + dsk_ref[...] * xs) * _silu(z)
        halves = []
        for g in range(SSM_GROUPS):
            yg = y[:, g * 128:(g + 1) * 128]
            halves.append(yg * lax.rsqrt(jnp.mean(yg * yg, axis=-1, keepdims=True) + EPS))
        o_ref[rr, :] = jnp.concatenate(halves, axis=1) * ng_ref[...]
    hs_ref[...] = hs


def _ssd(ub, cw, cb, dtb, a_exp, dsk, ng, ltri, smask, vmask, bsz, seq, layer):
    ts = SEQ_TILE
    nt = seq // ts
    const = lambda b, n: (0, 0)
    lay = lambda b, n: (layer, 0, 0)
    tok = lambda b, n: (b * nt + n, 0)
    return pl.pallas_call(
        _ssd_kernel,
        grid=(bsz, nt),
        in_specs=[
            pl.BlockSpec((ts, 1280), tok),
            pl.BlockSpec((None, 8, 768), lay), pl.BlockSpec((None, 1, 768), lay),
            pl.BlockSpec((None, 1, 256), lay), pl.BlockSpec((None, 1, 256), lay),
            pl.BlockSpec((None, 1, 256), lay), pl.BlockSpec((None, 1, 256), lay),
            pl.BlockSpec((CHUNK, CHUNK), const), pl.BlockSpec((256, 256), const),
            pl.BlockSpec((8, 256), const),
        ],
        out_specs=pl.BlockSpec((ts, 256), tok),
        out_shape=jax.ShapeDtypeStruct((bsz * seq, GROUP_WIDTH), F32),
        scratch_shapes=[pltpu.VMEM((ts + 8, 768), F32), pltpu.VMEM((ts, 768), F32), pltpu.VMEM((256, 256), F32)],
        compiler_params=pltpu.CompilerParams(dimension_semantics=("arbitrary", "arbitrary")),
        name="ssd",
    )(ub, cw, cb, dtb, a_exp, dsk, ng, ltri, smask, vmask)


def _ret_kernel(u_ref, dmat_ref, zeta_ref, xi_ref, gch_ref, bd_ref, rmask_ref, qm_ref, vm_ref,
                o_ref, r_ref):
    @pl.when(pl.program_id(1) == 0)
    def _():
        r_ref[...] = jnp.zeros(r_ref.shape, F32)

    r = r_ref[...]
    for c in range(SEQ_TILE // CHUNK):
        rr = pl.ds(c * CHUNK, CHUNK)
        q = u_ref[rr, 0:256]
        k = u_ref[rr, 256:512]
        v = u_ref[rr, 512:768]
        g = u_ref[rr, 768:1024]
        kb = k.astype(BF16)
        q4 = jnp.concatenate([(q * qm_ref[h:h + 1, :]).astype(BF16) for h in range(GROUP_HEADS)], axis=0)
        s4 = _dot_nt(q4, kb) * dmat_ref[...]
        s_cat = jnp.concatenate([s4[h * CHUNK:(h + 1) * CHUNK, :] for h in range(GROUP_HEADS)], axis=1)
        v4 = jnp.concatenate([(v * vm_ref[h:h + 1, :]).astype(BF16) for h in range(GROUP_HEADS)], axis=0)
        y = _dot(q.astype(BF16), r.astype(BF16)) * xi_ref[...] + _dot(s_cat.astype(BF16), v4)
        ss = _dot((y * y).astype(BF16), bd_ref[...])
        o_ref[rr, :] = y * lax.rsqrt(ss * (1.0 / HEAD_DIM) + EPS) * _silu(g)
        kz = (k * zeta_ref[...]).astype(BF16)
        r = gch_ref[...] * r + _dot_tn(kz, v.astype(BF16)) * rmask_ref[...]
    r_ref[...] = r


def _retention(uc, dmat, zeta, xi, gch, bdn, rmask, qmask, vmask, bsz, seq):
    ts = SEQ_TILE
    nt = seq // ts
    const = lambda b, n: (0, 0)
    tok = lambda b, n: (b * nt + n, 0)
    return pl.pallas_call(
        _ret_kernel,
        grid=(bsz, nt),
        in_specs=[
            pl.BlockSpec((ts, 1024), tok),
            pl.BlockSpec((GROUP_HEADS * CHUNK, CHUNK), const),
            pl.BlockSpec((CHUNK, 256), const), pl.BlockSpec((CHUNK, 256), const),
            pl.BlockSpec((1, 256), const), pl.BlockSpec((256, 256), const),
            pl.BlockSpec((256, 256), const),
            pl.BlockSpec((8, 256), const), pl.BlockSpec((8, 256), const),
        ],
        out_specs=pl.BlockSpec((ts, 256), tok),
        out_shape=jax.ShapeDtypeStruct((bsz * seq, GROUP_WIDTH), F32),
        scratch_shapes=[pltpu.VMEM((256, 256), F32)],
        compiler_params=pltpu.CompilerParams(dimension_semantics=("arbitrary", "arbitrary")),
        name="retention",
    )(uc, dmat, zeta, xi, gch, bdn, rmask, qmask, vmask)


FF_CHUNK = 1024


def _rms_rows(x, g):
    return x * lax.rsqrt(jnp.mean(x * x, axis=-1, keepdims=True) + EPS) * g


def _ffn_kernel(x_ref, ya_ref, yb_ref, yc_ref, yd_ref, p_ref, wout_ref, gf_ref, wup_ref, wdn_ref,
                gp_ref, wple_ref, wgate_ref, o_ref, hb_ref):
    tm = x_ref.shape[0]
    halves = [pl.ds(0, tm // 2), pl.ds(tm // 2, tm // 2)]
    x1 = []
    for rr in halves:
        acc = x_ref[rr, :]
        for m, y_ref in enumerate((ya_ref, yb_ref, yc_ref, yd_ref)):
            acc = acc + _dot(y_ref[rr, :].astype(BF16), wout_ref[m * 256:(m + 1) * 256, :])
        hb_ref[rr, :] = _rms_rows(acc, gf_ref[...]).astype(BF16)
        x1.append(acc)

    def down(rr, c):
        up = jnp.maximum(_dot(hb_ref[rr, :], wup_ref[:, c * FF_CHUNK:(c + 1) * FF_CHUNK]), 0.0)
        return _dot((up * up).astype(BF16), wdn_ref[c * FF_CHUNK:(c + 1) * FF_CHUNK, :])

    ffs = [down(rr, 0) for rr in halves]
    for c in range(1, D_FF // FF_CHUNK):
        ffs = [ff + down(rr, c) for ff, rr in zip(ffs, halves)]
    for rr, acc, ff in zip(halves, x1, ffs):
        x2 = acc + ff
        gate = _sigmoid(_dot(_rms_rows(x2, gp_ref[...]).astype(BF16), wgate_ref[...]))
        o_ref[rr, :] = x2 + _dot(p_ref[rr, :].astype(BF16), wple_ref[...]) * gate


def _ffn(x2d, ya, yb, yc, yd, p_all, wout, gf, wup, wdn, gp, wple, wgate, tm, layer):
    t = x2d.shape[0]
    tok = lambda i: (i, 0)
    lay = lambda i: (layer, 0, 0)
    once = dict(pipeline_mode=pl.Buffered(1))
    return pl.pallas_call(
        _ffn_kernel,
        grid=(t // tm,),
        in_specs=[
            pl.BlockSpec((tm, D_MODEL), tok),
            pl.BlockSpec((tm, 256), tok), pl.BlockSpec((tm, 256), tok),
            pl.BlockSpec((tm, 256), tok), pl.BlockSpec((tm, 256), tok),
            pl.BlockSpec((None, tm, PLE_DIM), lambda i: (layer, i, 0)),
            pl.BlockSpec((None, D_MODEL, D_MODEL), lay, **once),
            pl.BlockSpec((None, 1, D_MODEL), lay),
            pl.BlockSpec((None, D_MODEL, D_FF), lay, **once),
            pl.BlockSpec((None, D_FF, D_MODEL), lay, **once),
            pl.BlockSpec((None, 1, D_MODEL), lay),
            pl.BlockSpec((None, PLE_DIM, D_MODEL), lay, **once),
            pl.BlockSpec((None, D_MODEL, D_MODEL), lay, **once),
        ],
        out_specs=pl.BlockSpec((tm, D_MODEL), tok),
        out_shape=jax.ShapeDtypeStruct((t, D_MODEL), F32),
        scratch_shapes=[pltpu.VMEM((tm, D_MODEL), BF16)],
        compiler_params=pltpu.CompilerParams(
            dimension_semantics=("arbitrary",), vmem_limit_bytes=56 * 1024 * 1024),
        name="ffn",
    )(x2d, ya, yb, yc, yd, p_all, wout, gf, wup, wdn, gp, wple, wgate)


def _head_mask(head_of_lane):
    m = np.zeros((8, GROUP_WIDTH), np.float32)
    for h in range(GROUP_HEADS):
        m[h] = head_of_lane == h
    return m


def _tables(seq):
    pos = jnp.arange(seq, dtype=F32)
    ang_a = ROPE_THETA ** (-jnp.arange(0, HEAD_DIM, 2, dtype=F32) / HEAD_DIM)
    ang_c = 1.0 / (10000.0 ** jnp.linspace(0.0, 1.0, HEAD_DIM // 2, dtype=F32))

    def cs(ang):
        a = pos[:, None] * ang[None, :]
        return jnp.tile(jnp.cos(a), (1, 4)), jnp.tile(jnp.sin(a), (1, 4))

    cosa, sina = cs(ang_a)
    cosc, sinc = cs(ang_c)
    log_g = jnp.log(1.0 - 2.0 ** (-5.0 - jnp.arange(GROUP_HEADS, dtype=F32)))
    idx = jnp.arange(CHUNK, dtype=F32)
    dist = idx[:, None] - idx[None, :]
    dmat = jnp.where((dist >= 0)[None], jnp.exp(jnp.maximum(dist, 0.0)[None] * log_g[:, None, None]), 0.0)
    zeta = jnp.exp((CHUNK - 1 - idx)[:, None] * log_g)[:, _HEAD_PERM]
    xi = jnp.exp((idx + 1.0)[:, None] * log_g)[:, _HEAD_NAT]
    gch = jnp.exp(CHUNK * log_g)[_HEAD_NAT][None, :]
    return dict(
        cosa=cosa, sina=sina, cosc=cosc, sinc=sinc, dmat=dmat, zeta=zeta, xi=xi, gch=gch,
        bd_perm=jnp.asarray(_HEAD_PERM[:, None] == _HEAD_PERM[None, :], BF16),
        bd_nat=jnp.asarray(_HEAD_NAT[:, None] == _HEAD_NAT[None, :], BF16),
        rmask=jnp.asarray(_HEAD_PERM[:, None] == _HEAD_NAT[None, :], F32),
        smask=jnp.asarray((_LANE[:, None] // SSM_STATE) == (_HEAD_NAT[None, :] // (GROUP_HEADS // SSM_GROUPS)), F32),
        qmask=jnp.asarray(_head_mask(_HEAD_PERM)),
        qmask_bf=jnp.asarray(np.concatenate([_head_mask(_HEAD_PERM)] * 2), BF16),
        vmask=jnp.asarray(_head_mask(_HEAD_NAT)),
        ltri=jnp.asarray(np.tril(np.ones((CHUNK, CHUNK), np.float32)), BF16),
    )


def _build_w_cat(w_in):
    w = w_in
    depth = w.shape[0]

    def nat(c0):
        return w[:, :, c0:c0 + 256]

    def perm(c0):
        blk = w[:, :, c0:c0 + 256].reshape(depth, D_MODEL, GROUP_HEADS, 2, 32)
        return blk.transpose(0, 1, 3, 2, 4).reshape(depth, D_MODEL, 256)

    dt = jnp.repeat(w[:, :, _BDT:_BDT + GROUP_HEADS], HEAD_DIM, axis=2)
    return jnp.concatenate([
        perm(_A0), perm(_A0 + 256), nat(_A0 + 512),
        nat(_BZ), nat(_BX), nat(_BB), nat(_BC), dt,
        perm(_C0), perm(_C0 + 256), nat(_C0 + 512), nat(_C0 + 768),
        nat(_D0), nat(_D0 + 256)], axis=2).astype(BF16)


def kernel(x, p, norm_mix, w_in, attn_q_norm, attn_k_norm, ssm_conv_w, ssm_conv_b, ssm_dt_bias,
           ssm_a_log, ssm_d, ssm_norm, conv_dw_w, conv_dw_b, conv_ln_g, conv_ln_b, w_out,
           norm_ffn, w_up, w_down, norm_ple, w_ple, w_ple_gate):
    bsz, seq, _ = x.shape
    depth = w_in.shape[0]
    tm = 512
    tb = _tables(seq)
    row = lambda a: a[:, None, :]
    w_cat = _build_w_cat(w_in)
    g_mix, g_ffn, g_ple = row(norm_mix), row(norm_ffn), row(norm_ple)
    gq, gk = row(attn_q_norm[:, _PERM % HEAD_DIM]), row(attn_k_norm[:, _PERM % HEAD_DIM])
    cw = jnp.pad(ssm_conv_w, ((0, 0), (0, 8 - SSM_CONV), (0, 0)))
    dw = jnp.pad(conv_dw_w, ((0, 0), (0, 32 - CONF_KERNEL), (0, 0)))
    dtb, alog, dsk = row(ssm_dt_bias[:, _HEAD_NAT]), row(ssm_a_log[:, _HEAD_NAT]), row(ssm_d[:, _HEAD_NAT])
    wout, wup, wdn = w_out.astype(BF16), w_up.astype(BF16), w_down.astype(BF16)
    wple, wgate = w_ple.astype(BF16), w_ple_gate.astype(BF16)
    p_all = p.reshape(depth, bsz * seq, PLE_DIM)
    x2d = x.reshape(bsz * seq, D_MODEL)
    for i in range(depth):
        oa, ob, oc, yd = _inproj(x2d, g_mix, w_cat, tb["cosa"], tb["sina"], tb["cosc"], tb["sinc"],
                                 gq, gk, tb["bd_perm"], dw, row(conv_dw_b), row(conv_ln_g), row(conv_ln_b),
                                 seq, tm, i)
        ya = _attention(oa, tb["qmask_bf"], bsz, seq)
        yb = _ssd(ob, cw, row(ssm_conv_b), dtb, alog, dsk, row(ssm_norm), tb["ltri"], tb["smask"],
                  tb["vmask"], bsz, seq, i)
        yc = _retention(oc, tb["dmat"].reshape(GROUP_HEADS * CHUNK, CHUNK), tb["zeta"], tb["xi"], tb["gch"], tb["bd_nat"], tb["rmask"],
                        tb["qmask"], tb["vmask"], bsz, seq)
        x2d = _ffn(x2d, ya, yb, yc, yd, p_all, wout, g_ffn, wup, wdn, g_ple, wple, wgate, tm, i)
    return x2d.reshape(bsz, seq, D_MODEL)
```

```python
import functools

import numpy as np
import jax
import jax.numpy as jnp
from jax import lax
from jax.experimental import pallas as pl
from jax.experimental.pallas import tpu as pltpu

F32 = jnp.float32
BF16 = jnp.bfloat16

D_MODEL = 1024
GROUP_WIDTH = 256
GROUP_HEADS = 4
HEAD_DIM = 64
EPS = 1e-6
ATTN_PATTERNS = ((128, 1), (512, 4), (2048, 16))
ATTN_BLOCK = 128
ROPE_THETA = 10000.0
SSM_STATE = 128
SSM_GROUPS = 2
SSM_CONV = 4
CHUNK = 128
CONF_KERNEL = 31
D_FF = 4 * D_MODEL
PLE_DIM = 256
NEG = -1e30
LOG2E = 1.4426950408889634
CONF_HALO = 32
CONF_ROWS = 64

_A0 = 0
_B0 = 3 * GROUP_WIDTH
_BZ, _BX, _BB, _BC = _B0, _B0 + 256, _B0 + 512, _B0 + 768
_BDT = _B0 + 1024
_C0 = _BDT + GROUP_HEADS
_D0 = _C0 + 4 * GROUP_WIDTH
IN_COLS = _D0 + 2 * GROUP_WIDTH

_LANE = np.arange(GROUP_WIDTH)
_PERM = ((_LANE % 128) // 32) * HEAD_DIM + (_LANE // 128) * 32 + (_LANE % 32)
_HEAD_PERM = (_LANE % 128) // 32
_HEAD_NAT = _LANE // HEAD_DIM

N_PROJ = 14 * GROUP_WIDTH


def _sigmoid(x):
    return 1.0 / (1.0 + jnp.exp(-x))


def _silu(x):
    return x * _sigmoid(x)


def _softplus(x):
    return jnp.maximum(x, 0.0) + jnp.log(1.0 + jnp.exp(-jnp.abs(x)))


def _dot(a, b):
    return jnp.dot(a, b, preferred_element_type=F32)


def _dot_nt(a, b):
    return lax.dot_general(a, b, (((1,), (1,)), ((), ())), preferred_element_type=F32)


def _dot_tn(a, b):
    return lax.dot_general(a, b, (((0,), (0,)), ((), ())), preferred_element_type=F32)


def _inproj_kernel(x_ref, g_ref, w_ref, cosa_ref, sina_ref, cosc_ref, sinc_ref,
                   gq_ref, gk_ref, bd_ref, dw_ref, db_ref, lg_ref, lb_ref,
                   oa_ref, ob_ref, oc_ref, yd_ref, hb0_ref, hb1_ref, ext_ref, *, nseq):
    step = pl.program_id(0)
    tm = oa_ref.shape[0]

    def norm_into(dst_ref):
        x = x_ref[...]
        h = x * lax.rsqrt(jnp.mean(x * x, axis=-1, keepdims=True) + EPS) * g_ref[...]
        dst_ref[...] = h.astype(BF16)

    @pl.when(step == 0)
    def _():
        norm_into(hb0_ref)
        ext_ref[CONF_HALO + tm:CONF_HALO + tm + 8, :] = jnp.zeros((8, 256), F32)

    @pl.when(step % nseq == 1)
    def _():
        ext_ref[0:CONF_HALO, :] = jnp.zeros((CONF_HALO, 256), F32)

    rest = (w_ref, cosa_ref, sina_ref, cosc_ref, sinc_ref, gq_ref, gk_ref, bd_ref, dw_ref, db_ref, lg_ref,
            lb_ref, oa_ref, ob_ref, oc_ref, yd_ref, ext_ref)

    @pl.when((step > 0) & (step % 2 == 0))
    def _():
        norm_into(hb0_ref)
        _inproj_body(hb1_ref, *rest)

    @pl.when(step % 2 == 1)
    def _():
        norm_into(hb1_ref)
        _inproj_body(hb0_ref, *rest)


def _inproj_body(hb_ref, w_ref, cosa_ref, sina_ref, cosc_ref, sinc_ref, gq_ref, gk_ref, bd_ref,
                 dw_ref, db_ref, lg_ref, lb_ref, oa_ref, ob_ref, oc_ref, yd_ref, ext_ref):
    tm = oa_ref.shape[0]

    def mm(j):
        return _dot(hb_ref[...], w_ref[:, j * 256:(j + 1) * 256])

    def rot(t, cos, sin):
        t1, t2 = t[:, :128], t[:, 128:]
        return jnp.concatenate([t1 * cos - t2 * sin, t2 * cos + t1 * sin], axis=1)

    def headnorm(t, gain):
        ss = _dot((t * t).astype(BF16), bd_ref[...])
        return t * lax.rsqrt(ss * (1.0 / HEAD_DIM) + EPS) * gain

    cosa, sina = cosa_ref[...], sina_ref[...]
    cosc, sinc = cosc_ref[...], sinc_ref[...]
    scale = HEAD_DIM ** -0.5
    ext_ref[CONF_HALO:CONF_HALO + tm, :] = mm(12) * _sigmoid(mm(13))
    conv_chunks = iter(range(tm // CONF_ROWS))

    def conv_some(n):
        for _ in range(n):
            c = next(conv_chunks, None)
            if c is not None:
                _conformer_rows(ext_ref, dw_ref, db_ref, lg_ref, lb_ref, yd_ref, c)

    oa_ref[:, 0:256] = mm(0)
    conv_some(1)
    oa_ref[:, 256:512] = mm(1)
    conv_some(1)
    oa_ref[:, 512:768] = mm(2)
    conv_some(1)
    for j in range(5):
        ob_ref[:, j * 256:(j + 1) * 256] = mm(3 + j)
        conv_some(1)
    oa_ref[:, 0:256] = rot(headnorm(oa_ref[:, 0:256], gq_ref[...]), cosa, sina) * (scale * LOG2E)
    oa_ref[:, 256:512] = rot(headnorm(oa_ref[:, 256:512], gk_ref[...]), cosa, sina)
    oc_ref[:, 0:256] = rot(mm(8), cosc, sinc)
    oc_ref[:, 256:512] = rot(mm(9), cosc, sinc) * scale
    oc_ref[:, 512:768] = mm(10)
    oc_ref[:, 768:1024] = mm(11)
    conv_some(tm // CONF_ROWS)
    ext_ref[0:CONF_HALO, :] = ext_ref[tm:tm + CONF_HALO, :]


def _conformer_rows(ext_ref, w_ref, b_ref, lg_ref, lb_ref, o_ref, c):
    nwin = CONF_ROWS + CONF_HALO + 8
    r0 = c * CONF_ROWS
    win = ext_ref[pl.ds(r0, nwin), :]
    acc = jnp.zeros((CONF_ROWS, 256), F32) + b_ref[...]
    for off in range(8):
        shifted = win if off == 0 else pltpu.roll(win, nwin - off, axis=0)
        for sh in range(CONF_HALO - CONF_KERNEL + 1, CONF_HALO + 1):
            if sh % 8 == off:
                k = sh - (CONF_HALO - CONF_KERNEL + 1)
                acc = acc + w_ref[k:k + 1, :] * shifted[sh - off:sh - off + CONF_ROWS, :]
    mu = jnp.mean(acc, axis=-1, keepdims=True)
    xc = acc - mu
    var = jnp.mean(xc * xc, axis=-1, keepdims=True)
    o_ref[pl.ds(r0, CONF_ROWS), :] = _silu(xc * lax.rsqrt(var + EPS) * lg_ref[...] + lb_ref[...])


def _inproj(x2d, g, w_cat, cosa, sina, cosc, sinc, gq, gk, bd, dw, db, lg, lb, seq, tm, layer):
    t = x2d.shape[0]
    nseq = seq // tm
    const = lambda i: (0, 0)
    lay = lambda i: (layer, 0, 0)
    nt = t // tm
    src = lambda i: (jnp.minimum(i, nt - 1), 0)
    tok = lambda i: (jnp.maximum(i - 1, 0), 0)
    pos = lambda i: (jnp.maximum(i - 1, 0) % nseq, 0)
    return pl.pallas_call(
        functools.partial(_inproj_kernel, nseq=nseq),
        grid=(nt + 1,),
        in_specs=[
            pl.BlockSpec((tm, D_MODEL), src),
            pl.BlockSpec((None, 1, D_MODEL), lay),
            pl.BlockSpec((None, D_MODEL, N_PROJ), lay),
            pl.BlockSpec((tm, 128), pos), pl.BlockSpec((tm, 128), pos),
            pl.BlockSpec((tm, 128), pos), pl.BlockSpec((tm, 128), pos),
            pl.BlockSpec((None, 1, 256), lay), pl.BlockSpec((None, 1, 256), lay),
            pl.BlockSpec((256, 256), const),
            pl.BlockSpec((None, 32, 256), lay), pl.BlockSpec((None, 1, 256), lay),
            pl.BlockSpec((None, 1, 256), lay), pl.BlockSpec((None, 1, 256), lay),
        ],
        out_specs=[
            pl.BlockSpec((tm, 768), tok), pl.BlockSpec((tm, 1280), tok),
            pl.BlockSpec((tm, 1024), tok), pl.BlockSpec((tm, 256), tok),
        ],
        out_shape=[
            jax.ShapeDtypeStruct((t, 768), F32), jax.ShapeDtypeStruct((t, 1280), F32),
            jax.ShapeDtypeStruct((t, 1024), F32), jax.ShapeDtypeStruct((t, 256), F32),
        ],
        scratch_shapes=[pltpu.VMEM((tm, D_MODEL), BF16), pltpu.VMEM((tm, D_MODEL), BF16),
                        pltpu.VMEM((tm + CONF_HALO + 8, 256), F32)],
        compiler_params=pltpu.CompilerParams(
            dimension_semantics=("arbitrary",), vmem_limit_bytes=56 * 1024 * 1024),
        name="inproj",
    )(x2d, g, w_cat, cosa, sina, cosc, sinc, gq, gk, bd, dw, db, lg, lb)


ATTN_TILE = 2048
ATTN_GROUP = 4


def _attn_kernel(*refs):
    q_refs, kp_refs, kc_refs, vp_refs, vc_refs = (refs[2 * i:2 * i + 2] for i in range(5))
    qm_ref, o_ref = refs[10], refs[11]
    m_refs, l_refs, u_refs = refs[12:16], refs[16:20], refs[20:22]
    tile = pl.program_id(1)

    ii = lax.broadcasted_iota(jnp.int32, (128, 256), 0)
    jj = lax.broadcasted_iota(jnp.int32, (128, 256), 1)
    band_bias = jnp.where((jj >= ii) & (jj <= ii + ATTN_BLOCK), 0.0, NEG).astype(F32)
    noprev_bias = jnp.where(jj < ATTN_BLOCK, NEG, 0.0).astype(F32)
    first_bias = band_bias + jnp.where(tile == 0, 1.0, 0.0).astype(F32) * noprev_bias
    lo64 = lax.broadcasted_iota(jnp.int32, (128, 128), 1) < HEAD_DIM

    def rows(start, dil):
        if dil == 1:
            return pl.ds(pl.multiple_of(start, 128), 128)
        return pl.ds(start, 128, stride=dil)

    def wide(pair, rr):
        return jnp.concatenate([pair[0][rr, :], pair[1][rr, :]], axis=1)

    def blocks(descs, dil, init):
        st = []
        for qs, klo_refs, vlo_refs, lo_s, hi_s, bias in descs:
            qr, lo, hi = rows(qs, dil), rows(lo_s, dil), rows(hi_s, dil)
            q = wide(q_refs, qr)
            k = jnp.concatenate([wide(klo_refs, lo), wide(kc_refs, hi)], axis=0).astype(BF16)
            v = [jnp.concatenate([vlo_refs[c][lo, :], vc_refs[c][hi, :]], axis=0).astype(BF16) for c in range(2)]
            qb = q.astype(BF16)
            q4 = jnp.concatenate([qb * qm_ref[h:h + 1, :] for h in range(GROUP_HEADS)], axis=0)
            st.append(dict(qr=qr, v=v, bias=bias, s_all=_dot_nt(q4, k)))
        if not init:
            for d in st:
                d["m_old"] = [m_refs[h][d["qr"], :] for h in range(GROUP_HEADS)]
                d["l_old"] = [l_refs[h][d["qr"], :] for h in range(GROUP_HEADS)]
        for d in st:
            d["m_new"], d["l_new"], d["alpha"], ps = [], [], [], []
            for h in range(GROUP_HEADS):
                s = d["s_all"][h * 128:(h + 1) * 128, :] + d["bias"]
                rmax = jnp.max(s, axis=-1, keepdims=True)
                m_new = jnp.broadcast_to(rmax, (128, 128)) if init else jnp.maximum(d["m_old"][h], rmax)
                p = jnp.exp2(s - jnp.concatenate([m_new, m_new], axis=1))
                rsum = jnp.sum(p, axis=-1, keepdims=True)
                if init:
                    d["l_new"].append(jnp.broadcast_to(rsum, (128, 128)))
                else:
                    alpha = jnp.exp2(d["m_old"][h] - m_new)
                    d["l_new"].append(alpha * d["l_old"][h] + rsum)
                    d["alpha"].append(alpha)
                d["m_new"].append(m_new)
                ps.append(p.astype(BF16))
            d["pv"] = []
            for c in range(2):
                pv = _dot(jnp.concatenate(ps[2 * c:2 * c + 2], axis=0), d["v"][c])
                d["pv"].append(jnp.where(lo64, pv[0:128, :], pv[128:256, :]))
        if not init:
            for d in st:
                d["u_old"] = [u_refs[c][d["qr"], :] for c in range(2)]
        for d in st:
            for h in range(GROUP_HEADS):
                m_refs[h][d["qr"], :] = d["m_new"][h]
                l_refs[h][d["qr"], :] = d["l_new"][h]
            for c in range(2):
                if init:
                    u_refs[c][d["qr"], :] = d["pv"][c]
                else:
                    a_c = jnp.where(lo64, d["alpha"][2 * c], d["alpha"][2 * c + 1])
                    u_refs[c][d["qr"], :] = d["u_old"][c] * a_c + d["pv"][c]

    def first_desc(r, span):
        return (r, kp_refs, vp_refs, ATTN_TILE - span + r, r, first_bias)

    def later_desc(r, m, span):
        base = r + (m - 1) * span
        return (base + span, kc_refs, vc_refs, base, base + span, band_bias)

    for pi, (_, dil) in enumerate(reversed(ATTN_PATTERNS)):
        span = ATTN_BLOCK * dil
        nblk = ATTN_TILE // span
        init = pi == 0
        grp = ATTN_GROUP
        if nblk == 1:
            def class_group(i, carry, dil=dil, span=span, init=init):
                blocks([first_desc(grp * i + j, span) for j in range(grp)], dil, init)
                return carry

            lax.fori_loop(0, dil // grp, class_group, 0)
        else:
            def per_class(r, carry, dil=dil, span=span, nblk=nblk, init=init):
                blocks([first_desc(r, span)] + [later_desc(r, j, span) for j in range(1, grp)], dil, init)

                def later_group(i, c):
                    blocks([later_desc(r, grp * (i + 1) + j, span) for j in range(grp)], dil, init)
                    return c

                if nblk > grp:
                    lax.fori_loop(0, nblk // grp - 1, later_group, 0)
                return carry

            lax.fori_loop(0, dil, per_class, 0)

    lo64w = lax.broadcasted_iota(jnp.int32, (256, 128), 1) < HEAD_DIM

    def finish(c, carry):
        rr = pl.ds(pl.multiple_of(c * 256, 256), 256)
        for hlf in range(2):
            den = jnp.where(lo64w, l_refs[2 * hlf][rr, :], l_refs[2 * hlf + 1][rr, :])
            o_ref[rr, hlf * 128:(hlf + 1) * 128] = u_refs[hlf][rr, :] / den
        return carry

    lax.fori_loop(0, ATTN_TILE // 256, finish, 0)


def _attention(qkv, qmask, bsz, seq):
    nt = seq // ATTN_TILE
    cur = lambda c: (lambda b, n: (b * nt + n, c))
    prev = lambda c: (lambda b, n: (b * nt + jnp.maximum(n - 1, 0), c))
    blk = (ATTN_TILE, 128)
    col_maps = [cur(0), cur(1),
                prev(2), prev(3), cur(2), cur(3),
                prev(4), prev(5), cur(4), cur(5)]
    return pl.pallas_call(
        _attn_kernel,
        grid=(bsz, nt),
        in_specs=[pl.BlockSpec(blk, m) for m in col_maps] + [pl.BlockSpec((16, 256), lambda b, n: (0, 0))],
        out_specs=pl.BlockSpec((ATTN_TILE, 256), lambda b, n: (b * nt + n, 0)),
        out_shape=jax.ShapeDtypeStruct((bsz * seq, GROUP_WIDTH), F32),
        scratch_shapes=[pltpu.VMEM((ATTN_TILE, 128), F32)] * 10,
        compiler_params=pltpu.CompilerParams(
            dimension_semantics=("arbitrary", "arbitrary"), vmem_limit_bytes=56 * 1024 * 1024),
        name="attn",
    )(*([qkv] * 10), qmask)


SEQ_TILE = 1024
SSM_ROWS = 64


def _ssd_kernel(u_ref, cw_ref, cb_ref, dtb_ref, a_ref, dsk_ref, ng_ref, ltri_ref, smask_ref,
                vm_ref, o_ref, ext_ref, act_ref, hs_ref):
    ts = SEQ_TILE

    @pl.when(pl.program_id(1) == 0)
    def _():
        ext_ref[0:8, :] = jnp.zeros((8, 768), F32)
        hs_ref[...] = jnp.zeros(hs_ref.shape, F32)

    ext_ref[8:8 + ts, :] = u_ref[:, 256:1024]
    for c in range(ts // SSM_ROWS):
        for gc in range(3):
            cols = slice(gc * 256, (gc + 1) * 256)
            win = ext_ref[pl.ds(c * SSM_ROWS, SSM_ROWS + 8), cols]
            acc = cb_ref[:, cols] + cw_ref[SSM_CONV - 1:SSM_CONV, cols] * win[8:8 + SSM_ROWS, :]
            for sh in range(1, SSM_CONV):
                k = SSM_CONV - 1 - sh
                acc = acc + cw_ref[k:k + 1, cols] * pltpu.roll(win, sh, axis=0)[8:8 + SSM_ROWS, :]
            act_ref[pl.ds(c * SSM_ROWS, SSM_ROWS), cols] = _silu(acc)
    ext_ref[0:8, :] = ext_ref[ts:ts + 8, :]

    ii = lax.broadcasted_iota(jnp.int32, (CHUNK, CHUNK), 0)
    jj = lax.broadcasted_iota(jnp.int32, (CHUNK, CHUNK), 1)
    causal = jj <= ii
    ltri = ltri_ref[...]

    hs = hs_ref[...]
    for c in range(ts // CHUNK):
        rr = pl.ds(c * CHUNK, CHUNK)
        z = u_ref[rr, 0:256]
        xs = act_ref[rr, 0:256]
        bm = act_ref[rr, 256:512]
        cm = act_ref[rr, 512:768]
        dt = _softplus(u_ref[rr, 1024:1280] + dtb_ref[...])
        da = dt * (-jnp.exp(a_ref[...]))
        da_hi = da.astype(BF16)
        rem = da - da_hi.astype(F32)
        da_mid = rem.astype(BF16)
        da_lo = (rem - da_mid.astype(F32)).astype(BF16)
        acum = _dot(ltri, da_hi) + _dot(ltri, da_mid) + _dot(ltri, da_lo)
        total = acum[CHUNK - 1:CHUNK, :]
        bmb = bm.astype(BF16)
        cmb = cm.astype(BF16)
        xdt = xs * dt
        acum_t = (jnp.transpose(acum[:, 0:128]), jnp.transpose(acum[:, 128:256]))
        cbs = [_dot_nt(cmb[:, g * 128:(g + 1) * 128], bmb[:, g * 128:(g + 1) * 128]) for g in range(SSM_GROUPS)]
        mhs = []
        for h in range(GROUP_HEADS):
            col = acum[:, h * HEAD_DIM:h * HEAD_DIM + 1]
            rsel = (h % 2) * HEAD_DIM
            row = acum_t[h // 2][rsel:rsel + 1, :]
            lmat = jnp.exp(jnp.where(causal, col - row, NEG))
            mhs.append((cbs[h // 2] * lmat).astype(BF16))
        xdt4 = jnp.concatenate([(xdt * vm_ref[h:h + 1, :]).astype(BF16) for h in range(GROUP_HEADS)], axis=0)
        y = _dot(cmb, hs.astype(BF16)) * jnp.exp(acum) + _dot(jnp.concatenate(mhs, axis=1), xdt4)
        xw = (xs * (jnp.exp(total - acum) * dt)).astype(BF16)
        hs = jnp.exp(total) * hs + _dot_tn(bmb, xw) * smask_ref[...]
        y = (y + dsk_ref[...] * xs) * _silu(z)
        halves = []
        for g in range(SSM_GROUPS):
            yg = y[:, g * 128:(g + 1) * 128]
            halves.append(yg * lax.rsqrt(jnp.mean(yg * yg, axis=-1, keepdims=True) + EPS))
        o_ref[rr, :] = jnp.concatenate(halves, axis=1) * ng_ref[...]
    hs_ref[...] = hs


def _ssd(ub, cw, cb, dtb, a_exp, dsk, ng, ltri, smask, vmask, bsz, seq, layer):
    ts = SEQ_TILE
    nt = seq // ts
    const = lambda b, n: (0, 0)
    lay = lambda b, n: (layer, 0, 0)
    tok = lambda b, n: (b * nt + n, 0)
    return pl.pallas_call(
        _ssd_kernel,
        grid=(bsz, nt),
        in_specs=[
            pl.BlockSpec((ts, 1280), tok),
            pl.BlockSpec((None, 8, 768), lay), pl.BlockSpec((None, 1, 768), lay),
            pl.BlockSpec((None, 1, 256), lay), pl.BlockSpec((None, 1, 256), lay),
            pl.BlockSpec((None, 1, 256), lay), pl.BlockSpec((None, 1, 256), lay),
            pl.BlockSpec((CHUNK, CHUNK), const), pl.BlockSpec((256, 256), const),
            pl.BlockSpec((8, 256), const),
        ],
        out_specs=pl.BlockSpec((ts, 256), tok),
        out_shape=jax.ShapeDtypeStruct((bsz * seq, GROUP_WIDTH), F32),
        scratch_shapes=[pltpu.VMEM((ts + 8, 768), F32), pltpu.VMEM((ts, 768), F32), pltpu.VMEM((256, 256), F32)],
        compiler_params=pltpu.CompilerParams(dimension_semantics=("arbitrary", "arbitrary")),
        name="ssd",
    )(ub, cw, cb, dtb, a_exp, dsk, ng, ltri, smask, vmask)


def _ret_kernel(u_ref, dmat_ref, zeta_ref, xi_ref, gch_ref, bd_ref, rmask_ref, qm_ref, vm_ref,
                o_ref, r_ref):
    @pl.when(pl.program_id(1) == 0)
    def _():
        r_ref[...] = jnp.zeros(r_ref.shape, F32)

    r = r_ref[...]
    for c in range(SEQ_TILE // CHUNK):
        rr = pl.ds(c * CHUNK, CHUNK)
        q = u_ref[rr, 0:256]
        k = u_ref[rr, 256:512]
        v = u_ref[rr, 512:768]
        g = u_ref[rr, 768:1024]
        kb = k.astype(BF16)
        q4 = jnp.concatenate([(q * qm_ref[h:h + 1, :]).astype(BF16) for h in range(GROUP_HEADS)], axis=0)
        s4 = _dot_nt(q4, kb) * dmat_ref[...]
        s_cat = jnp.concatenate([s4[h * CHUNK:(h + 1) * CHUNK, :] for h in range(GROUP_HEADS)], axis=1)
        v4 = jnp.concatenate([(v * vm_ref[h:h + 1, :]).astype(BF16) for h in range(GROUP_HEADS)], axis=0)
        y = _dot(q.astype(BF16), r.astype(BF16)) * xi_ref[...] + _dot(s_cat.astype(BF16), v4)
        ss = _dot((y * y).astype(BF16), bd_ref[...])
        o_ref[rr, :] = y * lax.rsqrt(ss * (1.0 / HEAD_DIM) + EPS) * _silu(g)
        kz = (k * zeta_ref[...]).astype(BF16)
        r = gch_ref[...] * r + _dot_tn(kz, v.astype(BF16)) * rmask_ref[...]
    r_ref[...] = r


def _retention(uc, dmat, zeta, xi, gch, bdn, rmask, qmask, vmask, bsz, seq):
    ts = SEQ_TILE
    nt = seq // ts
    const = lambda b, n: (0, 0)
    tok = lambda b, n: (b * nt + n, 0)
    return pl.pallas_call(
        _ret_kernel,
        grid=(bsz, nt),
        in_specs=[
            pl.BlockSpec((ts, 1024), tok),
            pl.BlockSpec((GROUP_HEADS * CHUNK, CHUNK), const),
            pl.BlockSpec((CHUNK, 256), const), pl.BlockSpec((CHUNK, 256), const),
            pl.BlockSpec((1, 256), const), pl.BlockSpec((256, 256), const),
            pl.BlockSpec((256, 256), const),
            pl.BlockSpec((8, 256), const), pl.BlockSpec((8, 256), const),
        ],
        out_specs=pl.BlockSpec((ts, 256), tok),
        out_shape=jax.ShapeDtypeStruct((bsz * seq, GROUP_WIDTH), F32),
        scratch_shapes=[pltpu.VMEM((256, 256), F32)],
        compiler_params=pltpu.CompilerParams(dimension_semantics=("arbitrary", "arbitrary")),
        name="retention",
    )(uc, dmat, zeta, xi, gch, bdn, rmask, qmask, vmask)


FF_CHUNK = 1024


def _rms_rows(x, g):
    return x * lax.rsqrt(jnp.mean(x * x, axis=-1, keepdims=True) + EPS) * g


def _ffn_kernel(x_ref, ya_ref, yb_ref, yc_ref, yd_ref, p_ref, wout_ref, gf_ref, wup_ref, wdn_ref,
                gp_ref, wple_ref, wgate_ref, o_ref, hb_ref):
    tm = x_ref.shape[0]
    halves = [pl.ds(0, tm // 2), pl.ds(tm // 2, tm // 2)]
    x1 = []
    for rr in halves:
        acc = x_ref[rr, :]
        for m, y_ref in enumerate((ya_ref, yb_ref, yc_ref, yd_ref)):
            acc = acc + _dot(y_ref[rr, :].astype(BF16), wout_ref[m * 256:(m + 1) * 256, :])
        hb_ref[rr, :] = _rms_rows(acc, gf_ref[...]).astype(BF16)
        x1.append(acc)

    def down(rr, c):
        up = jnp.maximum(_dot(hb_ref[rr, :], wup_ref[:, c * FF_CHUNK:(c + 1) * FF_CHUNK]), 0.0)
        return _dot((up * up).astype(BF16), wdn_ref[c * FF_CHUNK:(c + 1) * FF_CHUNK, :])

    ffs = [down(rr, 0) for rr in halves]
    for c in range(1, D_FF // FF_CHUNK):
        ffs = [ff + down(rr, c) for ff, rr in zip(ffs, halves)]
    for rr, acc, ff in zip(halves, x1, ffs):
        x2 = acc + ff
        gate = _sigmoid(_dot(_rms_rows(x2, gp_ref[...]).astype(BF16), wgate_ref[...]))
        o_ref[rr, :] = x2 + _dot(p_ref[rr, :].astype(BF16), wple_ref[...]) * gate


def _ffn(x2d, ya, yb, yc, yd, p_all, wout, gf, wup, wdn, gp, wple, wgate, tm, layer):
    t = x2d.shape[0]
    tok = lambda i: (i, 0)
    lay = lambda i: (layer, 0, 0)
    once = dict(pipeline_mode=pl.Buffered(1))
    return pl.pallas_call(
        _ffn_kernel,
        grid=(t // tm,),
        in_specs=[
            pl.BlockSpec((tm, D_MODEL), tok),
            pl.BlockSpec((tm, 256), tok), pl.BlockSpec((tm, 256), tok),
            pl.BlockSpec((tm, 256), tok), pl.BlockSpec((tm, 256), tok),
            pl.BlockSpec((None, tm, PLE_DIM), lambda i: (layer, i, 0)),
            pl.BlockSpec((None, D_MODEL, D_MODEL), lay, **once),
            pl.BlockSpec((None, 1, D_MODEL), lay),
            pl.BlockSpec((None, D_MODEL, D_FF), lay, **once),
            pl.BlockSpec((None, D_FF, D_MODEL), lay, **once),
            pl.BlockSpec((None, 1, D_MODEL), lay),
            pl.BlockSpec((None, PLE_DIM, D_MODEL), lay, **once),
            pl.BlockSpec((None, D_MODEL, D_MODEL), lay, **once),
        ],
        out_specs=pl.BlockSpec((tm, D_MODEL), tok),
        out_shape=jax.ShapeDtypeStruct((t, D_MODEL), F32),
        scratch_shapes=[pltpu.VMEM((tm, D_MODEL), BF16)],
        compiler_params=pltpu.CompilerParams(
            dimension_semantics=("arbitrary",), vmem_limit_bytes=56 * 1024 * 1024),
        name="ffn",
    )(x2d, ya, yb, yc, yd, p_all, wout, gf, wup, wdn, gp, wple, wgate)


def _head_mask(head_of_lane):
    m = np.zeros((8, GROUP_WIDTH), np.float32)
    for h in range(GROUP_HEADS):
        m[h] = head_of_lane == h
    return m


def _tables(seq):
    pos = jnp.arange(seq, dtype=F32)
    ang_a = ROPE_THETA ** (-jnp.arange(0, HEAD_DIM, 2, dtype=F32) / HEAD_DIM)
    ang_c = 1.0 / (10000.0 ** jnp.linspace(0.0, 1.0, HEAD_DIM // 2, dtype=F32))

    def cs(ang):
        a = pos[:, None] * ang[None, :]
        return jnp.tile(jnp.cos(a), (1, 4)), jnp.tile(jnp.sin(a), (1, 4))

    cosa, sina = cs(ang_a)
    cosc, sinc = cs(ang_c)
    log_g = jnp.log(1.0 - 2.0 ** (-5.0 - jnp.arange(GROUP_HEADS, dtype=F32)))
    idx = jnp.arange(CHUNK, dtype=F32)
    dist = idx[:, None] - idx[None, :]
    dmat = jnp.where((dist >= 0)[None], jnp.exp(jnp.maximum(dist, 0.0)[None] * log_g[:, None, None]), 0.0)
    zeta = jnp.exp((CHUNK - 1 - idx)[:, None] * log_g)[:, _HEAD_PERM]
    xi = jnp.exp((idx + 1.0)[:, None] * log_g)[:, _HEAD_NAT]
    gch = jnp.exp(CHUNK * log_g)[_HEAD_NAT][None, :]
    return dict(
        cosa=cosa, sina=sina, cosc=cosc, sinc=sinc, dmat=dmat, zeta=zeta, xi=xi, gch=gch,
        bd_perm=jnp.asarray(_HEAD_PERM[:, None] == _HEAD_PERM[None, :], BF16),
        bd_nat=jnp.asarray(_HEAD_NAT[:, None] == _HEAD_NAT[None, :], BF16),
        rmask=jnp.asarray(_HEAD_PERM[:, None] == _HEAD_NAT[None, :], F32),
        smask=jnp.asarray((_LANE[:, None] // SSM_STATE) == (_HEAD_NAT[None, :] // (GROUP_HEADS // SSM_GROUPS)), F32),
        qmask=jnp.asarray(_head_mask(_HEAD_PERM)),
        qmask_bf=jnp.asarray(np.concatenate([_head_mask(_HEAD_PERM)] * 2), BF16),
        vmask=jnp.asarray(_head_mask(_HEAD_NAT)),
        ltri=jnp.asarray(np.tril(np.ones((CHUNK, CHUNK), np.float32)), BF16),
    )


WPREP_ROWS = 256
_TAIL0 = (IN_COLS // 128) * 128


def _selectors():
    off = _C0 % 128
    perm = np.zeros((256, 256), np.float32)
    perm[_PERM, _LANE] = 1
    dt = np.zeros((128, 256), np.float32)
    dt[_HEAD_NAT, _LANE] = 1
    shift = np.zeros((384, 256), np.float32)
    shift[off + _LANE, _LANE] = 1
    shift_perm = np.zeros((384, 256), np.float32)
    shift_perm[off + _PERM, _LANE] = 1
    head = np.zeros((256, 256), np.float32)
    keep = _LANE[: 256 - off]
    head[off + keep, keep] = 1
    tail = np.zeros((128, 256), np.float32)
    tail[np.arange(off), 256 - off + np.arange(off)] = 1
    return [jnp.asarray(m, BF16) for m in (perm, dt, shift, shift_perm, head, tail)]


def _wprep_kernel(w_ref, t_ref, perm_ref, dt_ref, shift_ref, shiftp_ref, head_ref, tail_ref, o_ref):
    def win(lo, width):
        return w_ref[:, lo:lo + width].astype(BF16)

    def put(j, val):
        o_ref[:, j * 256:(j + 1) * 256] = val.astype(BF16)

    a128 = lambda c: (c // 128) * 128
    put(0, _dot(win(_A0, 256), perm_ref[...]))
    put(1, _dot(win(_A0 + 256, 256), perm_ref[...]))
    put(2, win(_A0 + 512, 256))
    for j, c0 in enumerate((_BZ, _BX, _BB, _BC)):
        put(3 + j, win(c0, 256))
    put(7, _dot(win(_BDT, 128), dt_ref[...]))
    put(8, _dot(win(a128(_C0), 384), shiftp_ref[...]))
    put(9, _dot(win(a128(_C0 + 256), 384), shiftp_ref[...]))
    put(10, _dot(win(a128(_C0 + 512), 384), shift_ref[...]))
    put(11, _dot(win(a128(_C0 + 768), 384), shift_ref[...]))
    put(12, _dot(win(a128(_D0), 384), shift_ref[...]))
    put(13, _dot(win(a128(_D0 + 256), 256), head_ref[...]) + _dot(t_ref[...].astype(BF16), tail_ref[...]))


def _build_w_cat(w_in):
    depth = w_in.shape[0]
    tail = jnp.pad(w_in[:, :, _TAIL0:], ((0, 0), (0, 0), (0, 128 - (IN_COLS - _TAIL0))))
    sels = _selectors()
    rows = lambda d, i: (d, i, 0)
    const = lambda d, i: (0, 0)
    return pl.pallas_call(
        _wprep_kernel,
        grid=(depth, D_MODEL // WPREP_ROWS),
        in_specs=[pl.BlockSpec((None, WPREP_ROWS, IN_COLS), rows), pl.BlockSpec((None, WPREP_ROWS, 128), rows)]
        + [pl.BlockSpec(s.shape, const) for s in sels],
        out_specs=pl.BlockSpec((None, WPREP_ROWS, N_PROJ), rows),
        out_shape=jax.ShapeDtypeStruct((depth, D_MODEL, N_PROJ), BF16),
        compiler_params=pltpu.CompilerParams(dimension_semantics=("arbitrary", "arbitrary")),
        name="wprep",
    )(w_in, tail, *sels)


def kernel(x, p, norm_mix, w_in, attn_q_norm, attn_k_norm, ssm_conv_w, ssm_conv_b, ssm_dt_bias,
           ssm_a_log, ssm_d, ssm_norm, conv_dw_w, conv_dw_b, conv_ln_g, conv_ln_b, w_out,
           norm_ffn, w_up, w_down, norm_ple, w_ple, w_ple_gate):
    bsz, seq, _ = x.shape
    depth = w_in.shape[0]
    tm = 512
    tb = _tables(seq)
    row = lambda a: a[:, None, :]
    w_cat = _build_w_cat(w_in)
    g_mix, g_ffn, g_ple = row(norm_mix), row(norm_ffn), row(norm_ple)
    gq, gk = row(attn_q_norm[:, _PERM % HEAD_DIM]), row(attn_k_norm[:, _PERM % HEAD_DIM])
    cw = jnp.pad(ssm_conv_w, ((0, 0), (0, 8 - SSM_CONV), (0, 0)))
    dw = jnp.pad(conv_dw_w, ((0, 0), (0, 32 - CONF_KERNEL), (0, 0)))
    dtb, alog, dsk = row(ssm_dt_bias[:, _HEAD_NAT]), row(ssm_a_log[:, _HEAD_NAT]), row(ssm_d[:, _HEAD_NAT])
    wout, wup, wdn = w_out.astype(BF16), w_up.astype(BF16), w_down.astype(BF16)
    wple, wgate = w_ple.astype(BF16), w_ple_gate.astype(BF16)
    p_all = p.reshape(depth, bsz * seq, PLE_DIM)
    x2d = x.reshape(bsz * seq, D_MODEL)
    for i in range(depth):
        oa, ob, oc, yd = _inproj(x2d, g_mix, w_cat, tb["cosa"], tb["sina"], tb["cosc"], tb["sinc"],
                                 gq, gk, tb["bd_perm"], dw, row(conv_dw_b), row(conv_ln_g), row(conv_ln_b),
                                 seq, tm, i)
        ya = _attention(oa, tb["qmask_bf"], bsz, seq)
        yb = _ssd(ob, cw, row(ssm_conv_b), dtb, alog, dsk, row(ssm_norm), tb["ltri"], tb["smask"],
                  tb["vmask"], bsz, seq, i)
        yc = _retention(oc, tb["dmat"].reshape(GROUP_HEADS * CHUNK, CHUNK), tb["zeta"], tb["xi"], tb["gch"], tb["bd_nat"], tb["rmask"],
                        tb["qmask"], tb["vmask"], bsz, seq)
        x2d = _ffn(x2d, ya, yb, yc, yd, p_all, wout, g_ffn, wup, wdn, g_ple, wple, wgate, tm, i)
    return x2d.reshape(bsz, seq, D_MODEL)
```

```python
import functools

import numpy as np
import jax
import jax.numpy as jnp
from jax import lax
from jax.experimental import pallas as pl
from jax.experimental.pallas import tpu as pltpu

F32 = jnp.float32
BF16 = jnp.bfloat16

D_MODEL = 1024
GROUP_WIDTH = 256
GROUP_HEADS = 4
HEAD_DIM = 64
EPS = 1e-6
ATTN_PATTERNS = ((128, 1), (512, 4), (2048, 16))
ATTN_BLOCK = 128
ROPE_THETA = 10000.0
SSM_STATE = 128
SSM_GROUPS = 2
SSM_CONV = 4
CHUNK = 128
CONF_KERNEL = 31
D_FF = 4 * D_MODEL
PLE_DIM = 256
NEG = -1e30
LOG2E = 1.4426950408889634
CONF_HALO = 32
CONF_ROWS = 64

_A0 = 0
_B0 = 3 * GROUP_WIDTH
_BZ, _BX, _BB, _BC = _B0, _B0 + 256, _B0 + 512, _B0 + 768
_BDT = _B0 + 1024
_C0 = _BDT + GROUP_HEADS
_D0 = _C0 + 4 * GROUP_WIDTH
IN_COLS = _D0 + 2 * GROUP_WIDTH

_LANE = np.arange(GROUP_WIDTH)
_PERM = ((_LANE % 128) // 32) * HEAD_DIM + (_LANE // 128) * 32 + (_LANE % 32)
_HEAD_PERM = (_LANE % 128) // 32
_HEAD_NAT = _LANE // HEAD_DIM

N_PROJ = 14 * GROUP_WIDTH


def _sigmoid(x):
    return 1.0 / (1.0 + jnp.exp(-x))


def _silu(x):
    return x * _sigmoid(x)


def _softplus(x):
    return jnp.maximum(x, 0.0) + jnp.log(1.0 + jnp.exp(-jnp.abs(x)))


def _dot(a, b):
    return jnp.dot(a, b, preferred_element_type=F32)


def _dot_nt(a, b):
    return lax.dot_general(a, b, (((1,), (1,)), ((), ())), preferred_element_type=F32)


def _dot_tn(a, b):
    return lax.dot_general(a, b, (((0,), (0,)), ((), ())), preferred_element_type=F32)


def _inproj_kernel(x_ref, g_ref, w_ref, cosa_ref, sina_ref, cosc_ref, sinc_ref,
                   gq_ref, gk_ref, bd_ref, dw_ref, db_ref, lg_ref, lb_ref,
                   oa_ref, ob_ref, oc_ref, yd_ref, hb0_ref, hb1_ref, ext_ref, *, nseq):
    step = pl.program_id(0)
    tm = oa_ref.shape[0]

    def norm_into(dst_ref):
        x = x_ref[...]
        h = x * lax.rsqrt(jnp.mean(x * x, axis=-1, keepdims=True) + EPS) * g_ref[...]
        dst_ref[...] = h.astype(BF16)

    @pl.when(step == 0)
    def _():
        norm_into(hb0_ref)
        ext_ref[CONF_HALO + tm:CONF_HALO + tm + 8, :] = jnp.zeros((8, 256), F32)

    @pl.when(step % nseq == 1)
    def _():
        ext_ref[0:CONF_HALO, :] = jnp.zeros((CONF_HALO, 256), F32)

    rest = (w_ref, cosa_ref, sina_ref, cosc_ref, sinc_ref, gq_ref, gk_ref, bd_ref, dw_ref, db_ref, lg_ref,
            lb_ref, oa_ref, ob_ref, oc_ref, yd_ref, ext_ref)

    @pl.when((step > 0) & (step % 2 == 0))
    def _():
        norm_into(hb0_ref)
        _inproj_body(hb1_ref, *rest)

    @pl.when(step % 2 == 1)
    def _():
        norm_into(hb1_ref)
        _inproj_body(hb0_ref, *rest)


def _inproj_body(hb_ref, w_ref, cosa_ref, sina_ref, cosc_ref, sinc_ref, gq_ref, gk_ref, bd_ref,
                 dw_ref, db_ref, lg_ref, lb_ref, oa_ref, ob_ref, oc_ref, yd_ref, ext_ref):
    tm = oa_ref.shape[0]

    def mm(j):
        return _dot_nt(hb_ref[...], w_ref[j * 256:(j + 1) * 256, :])

    def rot(t, cos, sin):
        t1, t2 = t[:, :128], t[:, 128:]
        return jnp.concatenate([t1 * cos - t2 * sin, t2 * cos + t1 * sin], axis=1)

    def headnorm(t, gain):
        ss = _dot((t * t).astype(BF16), bd_ref[...])
        return t * lax.rsqrt(ss * (1.0 / HEAD_DIM) + EPS) * gain

    cosa, sina = cosa_ref[...], sina_ref[...]
    cosc, sinc = cosc_ref[...], sinc_ref[...]
    scale = HEAD_DIM ** -0.5
    ext_ref[CONF_HALO:CONF_HALO + tm, :] = mm(12) * _sigmoid(mm(13))
    conv_chunks = iter(range(tm // CONF_ROWS))

    def conv_some(n):
        for _ in range(n):
            c = next(conv_chunks, None)
            if c is not None:
                _conformer_rows(ext_ref, dw_ref, db_ref, lg_ref, lb_ref, yd_ref, c)

    oa_ref[:, 0:256] = mm(0)
    conv_some(1)
    oa_ref[:, 256:512] = mm(1)
    conv_some(1)
    oa_ref[:, 512:768] = mm(2)
    conv_some(1)
    for j in range(5):
        ob_ref[:, j * 256:(j + 1) * 256] = mm(3 + j)
        conv_some(1)
    oa_ref[:, 0:256] = rot(headnorm(oa_ref[:, 0:256], gq_ref[...]), cosa, sina) * (scale * LOG2E)
    oa_ref[:, 256:512] = rot(headnorm(oa_ref[:, 256:512], gk_ref[...]), cosa, sina)
    oc_ref[:, 0:256] = rot(mm(8), cosc, sinc)
    oc_ref[:, 256:512] = rot(mm(9), cosc, sinc) * scale
    oc_ref[:, 512:768] = mm(10)
    oc_ref[:, 768:1024] = mm(11)
    conv_some(tm // CONF_ROWS)
    ext_ref[0:CONF_HALO, :] = ext_ref[tm:tm + CONF_HALO, :]


def _conformer_rows(ext_ref, w_ref, b_ref, lg_ref, lb_ref, o_ref, c):
    nwin = CONF_ROWS + CONF_HALO + 8
    r0 = c * CONF_ROWS
    win = ext_ref[pl.ds(r0, nwin), :]
    acc = jnp.zeros((CONF_ROWS, 256), F32) + b_ref[...]
    for off in range(8):
        shifted = win if off == 0 else pltpu.roll(win, nwin - off, axis=0)
        for sh in range(CONF_HALO - CONF_KERNEL + 1, CONF_HALO + 1):
            if sh % 8 == off:
                k = sh - (CONF_HALO - CONF_KERNEL + 1)
                acc = acc + w_ref[k:k + 1, :] * shifted[sh - off:sh - off + CONF_ROWS, :]
    mu = jnp.mean(acc, axis=-1, keepdims=True)
    xc = acc - mu
    var = jnp.mean(xc * xc, axis=-1, keepdims=True)
    o_ref[pl.ds(r0, CONF_ROWS), :] = _silu(xc * lax.rsqrt(var + EPS) * lg_ref[...] + lb_ref[...])


def _inproj(x2d, g, w_cat, cosa, sina, cosc, sinc, gq, gk, bd, dw, db, lg, lb, seq, tm, layer):
    t = x2d.shape[0]
    nseq = seq // tm
    const = lambda i: (0, 0)
    lay = lambda i: (layer, 0, 0)
    nt = t // tm
    src = lambda i: (jnp.minimum(i, nt - 1), 0)
    tok = lambda i: (jnp.maximum(i - 1, 0), 0)
    pos = lambda i: (jnp.maximum(i - 1, 0) % nseq, 0)
    return pl.pallas_call(
        functools.partial(_inproj_kernel, nseq=nseq),
        grid=(nt + 1,),
        in_specs=[
            pl.BlockSpec((tm, D_MODEL), src),
            pl.BlockSpec((None, 1, D_MODEL), lay),
            pl.BlockSpec((None, N_PROJ, D_MODEL), lay),
            pl.BlockSpec((tm, 128), pos), pl.BlockSpec((tm, 128), pos),
            pl.BlockSpec((tm, 128), pos), pl.BlockSpec((tm, 128), pos),
            pl.BlockSpec((None, 1, 256), lay), pl.BlockSpec((None, 1, 256), lay),
            pl.BlockSpec((256, 256), const),
            pl.BlockSpec((None, 32, 256), lay), pl.BlockSpec((None, 1, 256), lay),
            pl.BlockSpec((None, 1, 256), lay), pl.BlockSpec((None, 1, 256), lay),
        ],
        out_specs=[
            pl.BlockSpec((tm, 768), tok), pl.BlockSpec((tm, 1280), tok),
            pl.BlockSpec((tm, 1024), tok), pl.BlockSpec((tm, 256), tok),
        ],
        out_shape=[
            jax.ShapeDtypeStruct((t, 768), F32), jax.ShapeDtypeStruct((t, 1280), F32),
            jax.ShapeDtypeStruct((t, 1024), F32), jax.ShapeDtypeStruct((t, 256), F32),
        ],
        scratch_shapes=[pltpu.VMEM((tm, D_MODEL), BF16), pltpu.VMEM((tm, D_MODEL), BF16),
                        pltpu.VMEM((tm + CONF_HALO + 8, 256), F32)],
        compiler_params=pltpu.CompilerParams(
            dimension_semantics=("arbitrary",), vmem_limit_bytes=56 * 1024 * 1024),
        name="inproj",
    )(x2d, g, w_cat, cosa, sina, cosc, sinc, gq, gk, bd, dw, db, lg, lb)


ATTN_TILE = 2048
ATTN_GROUP = 4


def _attn_kernel(*refs):
    q_refs, kp_refs, kc_refs, vp_refs, vc_refs = (refs[2 * i:2 * i + 2] for i in range(5))
    qm_ref, o_ref = refs[10], refs[11]
    m_refs, l_refs, u_refs = refs[12:16], refs[16:20], refs[20:22]
    tile = pl.program_id(1)

    ii = lax.broadcasted_iota(jnp.int32, (128, 256), 0)
    jj = lax.broadcasted_iota(jnp.int32, (128, 256), 1)
    band_bias = jnp.where((jj >= ii) & (jj <= ii + ATTN_BLOCK), 0.0, NEG).astype(F32)
    noprev_bias = jnp.where(jj < ATTN_BLOCK, NEG, 0.0).astype(F32)
    first_bias = band_bias + jnp.where(tile == 0, 1.0, 0.0).astype(F32) * noprev_bias
    lo64 = lax.broadcasted_iota(jnp.int32, (128, 128), 1) < HEAD_DIM

    def rows(start, dil):
        if dil == 1:
            return pl.ds(pl.multiple_of(start, 128), 128)
        return pl.ds(start, 128, stride=dil)

    def wide(pair, rr):
        return jnp.concatenate([pair[0][rr, :], pair[1][rr, :]], axis=1)

    def blocks(descs, dil, init):
        st = []
        for qs, klo_refs, vlo_refs, lo_s, hi_s, bias in descs:
            qr, lo, hi = rows(qs, dil), rows(lo_s, dil), rows(hi_s, dil)
            q = wide(q_refs, qr)
            k = jnp.concatenate([wide(klo_refs, lo), wide(kc_refs, hi)], axis=0).astype(BF16)
            v = [jnp.concatenate([vlo_refs[c][lo, :], vc_refs[c][hi, :]], axis=0).astype(BF16) for c in range(2)]
            qb = q.astype(BF16)
            q4 = jnp.concatenate([qb * qm_ref[h:h + 1, :] for h in range(GROUP_HEADS)], axis=0)
            st.append(dict(qr=qr, v=v, bias=bias, s_all=_dot_nt(q4, k)))
        if not init:
            for d in st:
                d["m_old"] = [m_refs[h][d["qr"], :] for h in range(GROUP_HEADS)]
                d["l_old"] = [l_refs[h][d["qr"], :] for h in range(GROUP_HEADS)]
        for d in st:
            d["m_new"], d["l_new"], d["alpha"], ps = [], [], [], []
            for h in range(GROUP_HEADS):
                s = d["s_all"][h * 128:(h + 1) * 128, :] + d["bias"]
                rmax = jnp.max(s, axis=-1, keepdims=True)
                m_new = jnp.broadcast_to(rmax, (128, 128)) if init else jnp.maximum(d["m_old"][h], rmax)
                p = jnp.exp2(s - jnp.concatenate([m_new, m_new], axis=1))
                rsum = jnp.sum(p, axis=-1, keepdims=True)
                if init:
                    d["l_new"].append(jnp.broadcast_to(rsum, (128, 128)))
                else:
                    alpha = jnp.exp2(d["m_old"][h] - m_new)
                    d["l_new"].append(alpha * d["l_old"][h] + rsum)
                    d["alpha"].append(alpha)
                d["m_new"].append(m_new)
                ps.append(p.astype(BF16))
            d["pv"] = []
            for c in range(2):
                pv = _dot(jnp.concatenate(ps[2 * c:2 * c + 2], axis=0), d["v"][c])
                d["pv"].append(jnp.where(lo64, pv[0:128, :], pv[128:256, :]))
        if not init:
            for d in st:
                d["u_old"] = [u_refs[c][d["qr"], :] for c in range(2)]
        for d in st:
            for h in range(GROUP_HEADS):
                m_refs[h][d["qr"], :] = d["m_new"][h]
                l_refs[h][d["qr"], :] = d["l_new"][h]
            for c in range(2):
                if init:
                    u_refs[c][d["qr"], :] = d["pv"][c]
                else:
                    a_c = jnp.where(lo64, d["alpha"][2 * c], d["alpha"][2 * c + 1])
                    u_refs[c][d["qr"], :] = d["u_old"][c] * a_c + d["pv"][c]

    def first_desc(r, span):
        return (r, kp_refs, vp_refs, ATTN_TILE - span + r, r, first_bias)

    def later_desc(r, m, span):
        base = r + (m - 1) * span
        return (base + span, kc_refs, vc_refs, base, base + span, band_bias)

    for pi, (_, dil) in enumerate(reversed(ATTN_PATTERNS)):
        span = ATTN_BLOCK * dil
        nblk = ATTN_TILE // span
        init = pi == 0
        grp = ATTN_GROUP
        if nblk == 1:
            def class_group(i, carry, dil=dil, span=span, init=init):
                blocks([first_desc(grp * i + j, span) for j in range(grp)], dil, init)
                return carry

            lax.fori_loop(0, dil // grp, class_group, 0)
        else:
            def per_class(r, carry, dil=dil, span=span, nblk=nblk, init=init):
                blocks([first_desc(r, span)] + [later_desc(r, j, span) for j in range(1, grp)], dil, init)

                def later_group(i, c):
                    blocks([later_desc(r, grp * (i + 1) + j, span) for j in range(grp)], dil, init)
                    return c

                if nblk > grp:
                    lax.fori_loop(0, nblk // grp - 1, later_group, 0)
                return carry

            lax.fori_loop(0, dil, per_class, 0)

    lo64w = lax.broadcasted_iota(jnp.int32, (256, 128), 1) < HEAD_DIM

    def finish(c, carry):
        rr = pl.ds(pl.multiple_of(c * 256, 256), 256)
        for hlf in range(2):
            den = jnp.where(lo64w, l_refs[2 * hlf][rr, :], l_refs[2 * hlf + 1][rr, :])
            o_ref[rr, hlf * 128:(hlf + 1) * 128] = u_refs[hlf][rr, :] / den
        return carry

    lax.fori_loop(0, ATTN_TILE // 256, finish, 0)


def _attention(qkv, qmask, bsz, seq):
    nt = seq // ATTN_TILE
    cur = lambda c: (lambda b, n: (b * nt + n, c))
    prev = lambda c: (lambda b, n: (b * nt + jnp.maximum(n - 1, 0), c))
    blk = (ATTN_TILE, 128)
    col_maps = [cur(0), cur(1),
                prev(2), prev(3), cur(2), cur(3),
                prev(4), prev(5), cur(4), cur(5)]
    return pl.pallas_call(
        _attn_kernel,
        grid=(bsz, nt),
        in_specs=[pl.BlockSpec(blk, m) for m in col_maps] + [pl.BlockSpec((16, 256), lambda b, n: (0, 0))],
        out_specs=pl.BlockSpec((ATTN_TILE, 256), lambda b, n: (b * nt + n, 0)),
        out_shape=jax.ShapeDtypeStruct((bsz * seq, GROUP_WIDTH), F32),
        scratch_shapes=[pltpu.VMEM((ATTN_TILE, 128), F32)] * 10,
        compiler_params=pltpu.CompilerParams(
            dimension_semantics=("arbitrary", "arbitrary"), vmem_limit_bytes=56 * 1024 * 1024),
        name="attn",
    )(*([qkv] * 10), qmask)


SEQ_TILE = 1024
SSM_ROWS = 64


def _ssd_kernel(u_ref, cw_ref, cb_ref, dtb_ref, a_ref, dsk_ref, ng_ref, ltri_ref, smask_ref,
                vm_ref, o_ref, ext_ref, act_ref, hs_ref):
    ts = SEQ_TILE

    @pl.when(pl.program_id(1) == 0)
    def _():
        ext_ref[0:8, :] = jnp.zeros((8, 768), F32)
        hs_ref[...] = jnp.zeros(hs_ref.shape, F32)

    ext_ref[8:8 + ts, :] = u_ref[:, 256:1024]
    for c in range(ts // SSM_ROWS):
        for gc in range(3):
            cols = slice(gc * 256, (gc + 1) * 256)
            win = ext_ref[pl.ds(c * SSM_ROWS, SSM_ROWS + 8), cols]
            acc = cb_ref[:, cols] + cw_ref[SSM_CONV - 1:SSM_CONV, cols] * win[8:8 + SSM_ROWS, :]
            for sh in range(1, SSM_CONV):
                k = SSM_CONV - 1 - sh
                acc = acc + cw_ref[k:k + 1, cols] * pltpu.roll(win, sh, axis=0)[8:8 + SSM_ROWS, :]
            act_ref[pl.ds(c * SSM_ROWS, SSM_ROWS), cols] = _silu(acc)
    ext_ref[0:8, :] = ext_ref[ts:ts + 8, :]

    ii = lax.broadcasted_iota(jnp.int32, (CHUNK, CHUNK), 0)
    jj = lax.broadcasted_iota(jnp.int32, (CHUNK, CHUNK), 1)
    causal = jj <= ii
    ltri = ltri_ref[...]

    hs = hs_ref[...]
    for c in range(ts // CHUNK):
        rr = pl.ds(c * CHUNK, CHUNK)
        z = u_ref[rr, 0:256]
        xs = act_ref[rr, 0:256]
        bm = act_ref[rr, 256:512]
        cm = act_ref[rr, 512:768]
        dt = _softplus(u_ref[rr, 1024:1280] + dtb_ref[...])
        da = dt * (-jnp.exp(a_ref[...]))
        da_hi = da.astype(BF16)
        rem = da - da_hi.astype(F32)
        da_mid = rem.astype(BF16)
        da_lo = (rem - da_mid.astype(F32)).astype(BF16)
        acum = _dot(ltri, da_hi) + _dot(ltri, da_mid) + _dot(ltri, da_lo)
        total = acum[CHUNK - 1:CHUNK, :]
        bmb = bm.astype(BF16)
        cmb = cm.astype(BF16)
        xdt = xs * dt
        acum_t = (jnp.transpose(acum[:, 0:128]), jnp.transpose(acum[:, 128:256]))
        cbs = [_dot_nt(cmb[:, g * 128:(g + 1) * 128], bmb[:, g * 128:(g + 1) * 128]) for g in range(SSM_GROUPS)]
        mhs = []
        for h in range(GROUP_HEADS):
            col = acum[:, h * HEAD_DIM:h * HEAD_DIM + 1]
            rsel = (h % 2) * HEAD_DIM
            row = acum_t[h // 2][rsel:rsel + 1, :]
            lmat = jnp.exp(jnp.where(causal, col - row, NEG))
            mhs.append((cbs[h // 2] * lmat).astype(BF16))
        xdt4 = jnp.concatenate([(xdt * vm_ref[h:h + 1, :]).astype(BF16) for h in range(GROUP_HEADS)], axis=0)
        y = _dot(cmb, hs.astype(BF16)) * jnp.exp(acum) + _dot(jnp.concatenate(mhs, axis=1), xdt4)
        xw = (xs * (jnp.exp(total - acum) * dt)).astype(BF16)
        hs = jnp.exp(total) * hs + _dot_tn(bmb, xw) * smask_ref[...]
        y = (y + dsk_ref[...] * xs) * _silu(z)
        halves = []
        for g in range(SSM_GROUPS):
            yg = y[:, g * 128:(g + 1) * 128]
            halves.append(yg * lax.rsqrt(jnp.mean(yg * yg, axis=-1, keepdims=True) + EPS))
        o_ref[rr, :] = jnp.concatenate(halves, axis=1) * ng_ref[...]
    hs_ref[...] = hs


def _ssd(ub, cw, cb, dtb, a_exp, dsk, ng, ltri, smask, vmask, bsz, seq, layer):
    ts = SEQ_TILE
    nt = seq // ts
    const = lambda b, n: (0, 0)
    lay = lambda b, n: (layer, 0, 0)
    tok = lambda b, n: (b * nt + n, 0)
    return pl.pallas_call(
        _ssd_kernel,
        grid=(bsz, nt),
        in_specs=[
            pl.BlockSpec((ts, 1280), tok),
            pl.BlockSpec((None, 8, 768), lay), pl.BlockSpec((None, 1, 768), lay),
            pl.BlockSpec((None, 1, 256), lay), pl.BlockSpec((None, 1, 256), lay),
            pl.BlockSpec((None, 1, 256), lay), pl.BlockSpec((None, 1, 256), lay),
            pl.BlockSpec((CHUNK, CHUNK), const), pl.BlockSpec((256, 256), const),
            pl.BlockSpec((8, 256), const),
        ],
        out_specs=pl.BlockSpec((ts, 256), tok),
        out_shape=jax.ShapeDtypeStruct((bsz * seq, GROUP_WIDTH), F32),
        scratch_shapes=[pltpu.VMEM((ts + 8, 768), F32), pltpu.VMEM((ts, 768), F32), pltpu.VMEM((256, 256), F32)],
        compiler_params=pltpu.CompilerParams(dimension_semantics=("arbitrary", "arbitrary")),
        name="ssd",
    )(ub, cw, cb, dtb, a_exp, dsk, ng, ltri, smask, vmask)


def _ret_kernel(u_ref, dmat_ref, zeta_ref, xi_ref, gch_ref, bd_ref, rmask_ref, qm_ref, vm_ref,
                o_ref, r_ref):
    @pl.when(pl.program_id(1) == 0)
    def _():
        r_ref[...] = jnp.zeros(r_ref.shape, F32)

    r = r_ref[...]
    for c in range(SEQ_TILE // CHUNK):
        rr = pl.ds(c * CHUNK, CHUNK)
        q = u_ref[rr, 0:256]
        k = u_ref[rr, 256:512]
        v = u_ref[rr, 512:768]
        g = u_ref[rr, 768:1024]
        kb = k.astype(BF16)
        q4 = jnp.concatenate([(q * qm_ref[h:h + 1, :]).astype(BF16) for h in range(GROUP_HEADS)], axis=0)
        s4 = _dot_nt(q4, kb) * dmat_ref[...]
        s_cat = jnp.concatenate([s4[h * CHUNK:(h + 1) * CHUNK, :] for h in range(GROUP_HEADS)], axis=1)
        v4 = jnp.concatenate([(v * vm_ref[h:h + 1, :]).astype(BF16) for h in range(GROUP_HEADS)], axis=0)
        y = _dot(q.astype(BF16), r.astype(BF16)) * xi_ref[...] + _dot(s_cat.astype(BF16), v4)
        ss = _dot((y * y).astype(BF16), bd_ref[...])
        o_ref[rr, :] = y * lax.rsqrt(ss * (1.0 / HEAD_DIM) + EPS) * _silu(g)
        kz = (k * zeta_ref[...]).astype(BF16)
        r = gch_ref[...] * r + _dot_tn(kz, v.astype(BF16)) * rmask_ref[...]
    r_ref[...] = r


def _retention(uc, dmat, zeta, xi, gch, bdn, rmask, qmask, vmask, bsz, seq):
    ts = SEQ_TILE
    nt = seq // ts
    const = lambda b, n: (0, 0)
    tok = lambda b, n: (b * nt + n, 0)
    return pl.pallas_call(
        _ret_kernel,
        grid=(bsz, nt),
        in_specs=[
            pl.BlockSpec((ts, 1024), tok),
            pl.BlockSpec((GROUP_HEADS * CHUNK, CHUNK), const),
            pl.BlockSpec((CHUNK, 256), const), pl.BlockSpec((CHUNK, 256), const),
            pl.BlockSpec((1, 256), const), pl.BlockSpec((256, 256), const),
            pl.BlockSpec((256, 256), const),
            pl.BlockSpec((8, 256), const), pl.BlockSpec((8, 256), const),
        ],
        out_specs=pl.BlockSpec((ts, 256), tok),
        out_shape=jax.ShapeDtypeStruct((bsz * seq, GROUP_WIDTH), F32),
        scratch_shapes=[pltpu.VMEM((256, 256), F32)],
        compiler_params=pltpu.CompilerParams(dimension_semantics=("arbitrary", "arbitrary")),
        name="retention",
    )(uc, dmat, zeta, xi, gch, bdn, rmask, qmask, vmask)


FF_CHUNK = 1024


def _rms_rows(x, g):
    return x * lax.rsqrt(jnp.mean(x * x, axis=-1, keepdims=True) + EPS) * g


def _ffn_kernel(x_ref, ya_ref, yb_ref, yc_ref, yd_ref, p_ref, wout_ref, gf_ref, wup_ref, wdn_ref,
                gp_ref, wple_ref, wgate_ref, o_ref, hb_ref):
    tm = x_ref.shape[0]
    halves = [pl.ds(0, tm // 2), pl.ds(tm // 2, tm // 2)]
    x1 = []
    for rr in halves:
        acc = x_ref[rr, :]
        for m, y_ref in enumerate((ya_ref, yb_ref, yc_ref, yd_ref)):
            acc = acc + _dot(y_ref[rr, :].astype(BF16), wout_ref[m * 256:(m + 1) * 256, :])
        hb_ref[rr, :] = _rms_rows(acc, gf_ref[...]).astype(BF16)
        x1.append(acc)

    def down(rr, c):
        up = jnp.maximum(_dot(hb_ref[rr, :], wup_ref[:, c * FF_CHUNK:(c + 1) * FF_CHUNK]), 0.0)
        return _dot((up * up).astype(BF16), wdn_ref[c * FF_CHUNK:(c + 1) * FF_CHUNK, :])

    ffs = [down(rr, 0) for rr in halves]
    for c in range(1, D_FF // FF_CHUNK):
        ffs = [ff + down(rr, c) for ff, rr in zip(ffs, halves)]
    for rr, acc, ff in zip(halves, x1, ffs):
        x2 = acc + ff
        gate = _sigmoid(_dot(_rms_rows(x2, gp_ref[...]).astype(BF16), wgate_ref[...]))
        o_ref[rr, :] = x2 + _dot(p_ref[rr, :].astype(BF16), wple_ref[...]) * gate


def _ffn(x2d, ya, yb, yc, yd, p_all, wout, gf, wup, wdn, gp, wple, wgate, tm, layer):
    t = x2d.shape[0]
    tok = lambda i: (i, 0)
    lay = lambda i: (layer, 0, 0)
    once = dict(pipeline_mode=pl.Buffered(1))
    return pl.pallas_call(
        _ffn_kernel,
        grid=(t // tm,),
        in_specs=[
            pl.BlockSpec((tm, D_MODEL), tok),
            pl.BlockSpec((tm, 256), tok), pl.BlockSpec((tm, 256), tok),
            pl.BlockSpec((tm, 256), tok), pl.BlockSpec((tm, 256), tok),
            pl.BlockSpec((None, tm, PLE_DIM), lambda i: (layer, i, 0)),
            pl.BlockSpec((None, D_MODEL, D_MODEL), lay, **once),
            pl.BlockSpec((None, 1, D_MODEL), lay),
            pl.BlockSpec((None, D_MODEL, D_FF), lay, **once),
            pl.BlockSpec((None, D_FF, D_MODEL), lay, **once),
            pl.BlockSpec((None, 1, D_MODEL), lay),
            pl.BlockSpec((None, PLE_DIM, D_MODEL), lay, **once),
            pl.BlockSpec((None, D_MODEL, D_MODEL), lay, **once),
        ],
        out_specs=pl.BlockSpec((tm, D_MODEL), tok),
        out_shape=jax.ShapeDtypeStruct((t, D_MODEL), F32),
        scratch_shapes=[pltpu.VMEM((tm, D_MODEL), BF16)],
        compiler_params=pltpu.CompilerParams(
            dimension_semantics=("arbitrary",), vmem_limit_bytes=56 * 1024 * 1024),
        name="ffn",
    )(x2d, ya, yb, yc, yd, p_all, wout, gf, wup, wdn, gp, wple, wgate)


def _head_mask(head_of_lane):
    m = np.zeros((8, GROUP_WIDTH), np.float32)
    for h in range(GROUP_HEADS):
        m[h] = head_of_lane == h
    return m


def _tables(seq):
    pos = jnp.arange(seq, dtype=F32)
    ang_a = ROPE_THETA ** (-jnp.arange(0, HEAD_DIM, 2, dtype=F32) / HEAD_DIM)
    ang_c = 1.0 / (10000.0 ** jnp.linspace(0.0, 1.0, HEAD_DIM // 2, dtype=F32))

    def cs(ang):
        a = pos[:, None] * ang[None, :]
        return jnp.tile(jnp.cos(a), (1, 4)), jnp.tile(jnp.sin(a), (1, 4))

    cosa, sina = cs(ang_a)
    cosc, sinc = cs(ang_c)
    log_g = jnp.log(1.0 - 2.0 ** (-5.0 - jnp.arange(GROUP_HEADS, dtype=F32)))
    idx = jnp.arange(CHUNK, dtype=F32)
    dist = idx[:, None] - idx[None, :]
    dmat = jnp.where((dist >= 0)[None], jnp.exp(jnp.maximum(dist, 0.0)[None] * log_g[:, None, None]), 0.0)
    zeta = jnp.exp((CHUNK - 1 - idx)[:, None] * log_g)[:, _HEAD_PERM]
    xi = jnp.exp((idx + 1.0)[:, None] * log_g)[:, _HEAD_NAT]
    gch = jnp.exp(CHUNK * log_g)[_HEAD_NAT][None, :]
    return dict(
        cosa=cosa, sina=sina, cosc=cosc, sinc=sinc, dmat=dmat, zeta=zeta, xi=xi, gch=gch,
        bd_perm=jnp.asarray(_HEAD_PERM[:, None] == _HEAD_PERM[None, :], BF16),
        bd_nat=jnp.asarray(_HEAD_NAT[:, None] == _HEAD_NAT[None, :], BF16),
        rmask=jnp.asarray(_HEAD_PERM[:, None] == _HEAD_NAT[None, :], F32),
        smask=jnp.asarray((_LANE[:, None] // SSM_STATE) == (_HEAD_NAT[None, :] // (GROUP_HEADS // SSM_GROUPS)), F32),
        qmask=jnp.asarray(_head_mask(_HEAD_PERM)),
        qmask_bf=jnp.asarray(np.concatenate([_head_mask(_HEAD_PERM)] * 2), BF16),
        vmask=jnp.asarray(_head_mask(_HEAD_NAT)),
        ltri=jnp.asarray(np.tril(np.ones((CHUNK, CHUNK), np.float32)), BF16),
    )


def _build_w_cat_t(w_in):
    wt = jnp.transpose(w_in, (0, 2, 1))

    def nat(c0):
        return [wt[:, c0:c0 + 256]]

    def perm(c0):
        return [wt[:, c0 + h * HEAD_DIM + half * 32:c0 + h * HEAD_DIM + half * 32 + 32]
                for half in range(2) for h in range(GROUP_HEADS)]

    dt = [jnp.repeat(wt[:, _BDT:_BDT + GROUP_HEADS], HEAD_DIM, axis=1)]
    parts = (perm(_A0) + perm(_A0 + 256) + nat(_A0 + 512)
             + nat(_BZ) + nat(_BX) + nat(_BB) + nat(_BC) + dt
             + perm(_C0) + perm(_C0 + 256) + nat(_C0 + 512) + nat(_C0 + 768)
             + nat(_D0) + nat(_D0 + 256))
    return jnp.concatenate(parts, axis=1).astype(BF16)


def kernel(x, p, norm_mix, w_in, attn_q_norm, attn_k_norm, ssm_conv_w, ssm_conv_b, ssm_dt_bias,
           ssm_a_log, ssm_d, ssm_norm, conv_dw_w, conv_dw_b, conv_ln_g, conv_ln_b, w_out,
           norm_ffn, w_up, w_down, norm_ple, w_ple, w_ple_gate):
    bsz, seq, _ = x.shape
    depth = w_in.shape[0]
    tm = 512
    tb = _tables(seq)
    row = lambda a: a[:, None, :]
    w_cat = _build_w_cat_t(w_in)
    g_mix, g_ffn, g_ple = row(norm_mix), row(norm_ffn), row(norm_ple)
    gq, gk = row(attn_q_norm[:, _PERM % HEAD_DIM]), row(attn_k_norm[:, _PERM % HEAD_DIM])
    cw = jnp.pad(ssm_conv_w, ((0, 0), (0, 8 - SSM_CONV), (0, 0)))
    dw = jnp.pad(conv_dw_w, ((0, 0), (0, 32 - CONF_KERNEL), (0, 0)))
    dtb, alog, dsk = row(ssm_dt_bias[:, _HEAD_NAT]), row(ssm_a_log[:, _HEAD_NAT]), row(ssm_d[:, _HEAD_NAT])
    wout, wup, wdn = w_out.astype(BF16), w_up.astype(BF16), w_down.astype(BF16)
    wple, wgate = w_ple.astype(BF16), w_ple_gate.astype(BF16)
    p_all = p.reshape(depth, bsz * seq, PLE_DIM)
    x2d = x.reshape(bsz * seq, D_MODEL)
    for i in range(depth):
        oa, ob, oc, yd = _inproj(x2d, g_mix, w_cat, tb["cosa"], tb["sina"], tb["cosc"], tb["sinc"],
                                 gq, gk, tb["bd_perm"], dw, row(conv_dw_b), row(conv_ln_g), row(conv_ln_b),
                                 seq, tm, i)
        ya = _attention(oa, tb["qmask_bf"], bsz, seq)
        yb = _ssd(ob, cw, row(ssm_conv_b), dtb, alog, dsk, row(ssm_norm), tb["ltri"], tb["smask"],
                  tb["vmask"], bsz, seq, i)
        yc = _retention(oc, tb["dmat"].reshape(GROUP_HEADS * CHUNK, CHUNK), tb["zeta"], tb["xi"], tb["gch"], tb["bd_nat"], tb["rmask"],
                        tb["qmask"], tb["vmask"], bsz, seq)
        x2d = _ffn(x2d, ya, yb, yc, yd, p_all, wout, g_ffn, wup, wdn, g_ple, wple, wgate, tm, i)
    return x2d.reshape(bsz, seq, D_MODEL)
```

```python
import functools

import numpy as np
import jax
import jax.numpy as jnp
from jax import lax
from jax.experimental import pallas as pl
from jax.experimental.pallas import tpu as pltpu

F32 = jnp.float32
BF16 = jnp.bfloat16

D_MODEL = 1024
GROUP_WIDTH = 256
GROUP_HEADS = 4
HEAD_DIM = 64
EPS = 1e-6
ATTN_PATTERNS = ((128, 1), (512, 4), (2048, 16))
ATTN_BLOCK = 128
ROPE_THETA = 10000.0
SSM_STATE = 128
SSM_GROUPS = 2
SSM_CONV = 4
CHUNK = 128
CONF_KERNEL = 31
D_FF = 4 * D_MODEL
PLE_DIM = 256
NEG = -1e30
LOG2E = 1.4426950408889634
CONF_HALO = 32
CONF_ROWS = 64

_A0 = 0
_B0 = 3 * GROUP_WIDTH
_BZ, _BX, _BB, _BC = _B0, _B0 + 256, _B0 + 512, _B0 + 768
_BDT = _B0 + 1024
_C0 = _BDT + GROUP_HEADS
_D0 = _C0 + 4 * GROUP_WIDTH
IN_COLS = _D0 + 2 * GROUP_WIDTH

_LANE = np.arange(GROUP_WIDTH)
_PERM = ((_LANE % 128) // 32) * HEAD_DIM + (_LANE // 128) * 32 + (_LANE % 32)
_HEAD_PERM = (_LANE % 128) // 32
_HEAD_NAT = _LANE // HEAD_DIM

N_PROJ = 14 * GROUP_WIDTH


def _sigmoid(x):
    return 1.0 / (1.0 + jnp.exp(-x))


def _silu(x):
    return x * _sigmoid(x)


def _softplus(x):
    return jnp.maximum(x, 0.0) + jnp.log(1.0 + jnp.exp(-jnp.abs(x)))


def _dot(a, b):
    return jnp.dot(a, b, preferred_element_type=F32)


def _dot_nt(a, b):
    return lax.dot_general(a, b, (((1,), (1,)), ((), ())), preferred_element_type=F32)


def _dot_tn(a, b):
    return lax.dot_general(a, b, (((0,), (0,)), ((), ())), preferred_element_type=F32)


def _inproj_kernel(x_ref, g_ref, w_ref, cosa_ref, sina_ref, cosc_ref, sinc_ref,
                   gq_ref, gk_ref, bd_ref, dw_ref, db_ref, lg_ref, lb_ref,
                   oa_ref, ob_ref, oc_ref, yd_ref, hb0_ref, hb1_ref, ext_ref, *, nseq):
    step = pl.program_id(0)
    tm = oa_ref.shape[0]

    def norm_into(dst_ref):
        x = x_ref[...]
        h = x * lax.rsqrt(jnp.mean(x * x, axis=-1, keepdims=True) + EPS) * g_ref[...]
        dst_ref[...] = h.astype(BF16)

    @pl.when(step == 0)
    def _():
        norm_into(hb0_ref)
        ext_ref[CONF_HALO + tm:CONF_HALO + tm + 8, :] = jnp.zeros((8, 256), F32)

    @pl.when(step % nseq == 1)
    def _():
        ext_ref[0:CONF_HALO, :] = jnp.zeros((CONF_HALO, 256), F32)

    rest = (w_ref, cosa_ref, sina_ref, cosc_ref, sinc_ref, gq_ref, gk_ref, bd_ref, dw_ref, db_ref, lg_ref,
            lb_ref, oa_ref, ob_ref, oc_ref, yd_ref, ext_ref)

    @pl.when((step > 0) & (step % 2 == 0))
    def _():
        norm_into(hb0_ref)
        _inproj_body(hb1_ref, *rest)

    @pl.when(step % 2 == 1)
    def _():
        norm_into(hb1_ref)
        _inproj_body(hb0_ref, *rest)


def _inproj_body(hb_ref, w_ref, cosa_ref, sina_ref, cosc_ref, sinc_ref, gq_ref, gk_ref, bd_ref,
                 dw_ref, db_ref, lg_ref, lb_ref, oa_ref, ob_ref, oc_ref, yd_ref, ext_ref):
    tm = oa_ref.shape[0]

    def mm(j):
        return _dot_nt(hb_ref[...], w_ref[j * 256:(j + 1) * 256, :])

    def rot(t, cos, sin):
        t1, t2 = t[:, :128], t[:, 128:]
        return jnp.concatenate([t1 * cos - t2 * sin, t2 * cos + t1 * sin], axis=1)

    def headnorm(t, gain):
        ss = _dot((t * t).astype(BF16), bd_ref[...])
        return t * lax.rsqrt(ss * (1.0 / HEAD_DIM) + EPS) * gain

    cosa, sina = cosa_ref[...], sina_ref[...]
    cosc, sinc = cosc_ref[...], sinc_ref[...]
    scale = HEAD_DIM ** -0.5
    ext_ref[CONF_HALO:CONF_HALO + tm, :] = mm(12) * _sigmoid(mm(13))
    conv_chunks = iter(range(tm // CONF_ROWS))

    def conv_some(n):
        for _ in range(n):
            c = next(conv_chunks, None)
            if c is not None:
                _conformer_rows(ext_ref, dw_ref, db_ref, lg_ref, lb_ref, yd_ref, c)

    oa_ref[:, 0:256] = mm(0)
    conv_some(1)
    oa_ref[:, 256:512] = mm(1)
    conv_some(1)
    oa_ref[:, 512:768] = mm(2)
    conv_some(1)
    for j in range(5):
        ob_ref[:, j * 256:(j + 1) * 256] = mm(3 + j)
        conv_some(1)
    oa_ref[:, 0:256] = rot(headnorm(oa_ref[:, 0:256], gq_ref[...]), cosa, sina) * (scale * LOG2E)
    oa_ref[:, 256:512] = rot(headnorm(oa_ref[:, 256:512], gk_ref[...]), cosa, sina)
    oc_ref[:, 0:256] = rot(mm(8), cosc, sinc)
    oc_ref[:, 256:512] = rot(mm(9), cosc, sinc) * scale
    oc_ref[:, 512:768] = mm(10)
    oc_ref[:, 768:1024] = mm(11)
    conv_some(tm // CONF_ROWS)
    ext_ref[0:CONF_HALO, :] = ext_ref[tm:tm + CONF_HALO, :]


def _conformer_rows(ext_ref, w_ref, b_ref, lg_ref, lb_ref, o_ref, c):
    nwin = CONF_ROWS + CONF_HALO + 8
    r0 = c * CONF_ROWS
    win = ext_ref[pl.ds(r0, nwin), :]
    acc = jnp.zeros((CONF_ROWS, 256), F32) + b_ref[...]
    for off in range(8):
        shifted = win if off == 0 else pltpu.roll(win, nwin - off, axis=0)
        for sh in range(CONF_HALO - CONF_KERNEL + 1, CONF_HALO + 1):
            if sh % 8 == off:
                k = sh - (CONF_HALO - CONF_KERNEL + 1)
                acc = acc + w_ref[k:k + 1, :] * shifted[sh - off:sh - off + CONF_ROWS, :]
    mu = jnp.mean(acc, axis=-1, keepdims=True)
    xc = acc - mu
    var = jnp.mean(xc * xc, axis=-1, keepdims=True)
    o_ref[pl.ds(r0, CONF_ROWS), :] = _silu(xc * lax.rsqrt(var + EPS) * lg_ref[...] + lb_ref[...])


def _inproj(x2d, g, w_cat, cosa, sina, cosc, sinc, gq, gk, bd, dw, db, lg, lb, seq, tm, layer):
    t = x2d.shape[0]
    nseq = seq // tm
    const = lambda i: (0, 0)
    lay = lambda i: (layer, 0, 0)
    nt = t // tm
    src = lambda i: (jnp.minimum(i, nt - 1), 0)
    tok = lambda i: (jnp.maximum(i - 1, 0), 0)
    pos = lambda i: (jnp.maximum(i - 1, 0) % nseq, 0)
    return pl.pallas_call(
        functools.partial(_inproj_kernel, nseq=nseq),
        grid=(nt + 1,),
        in_specs=[
            pl.BlockSpec((tm, D_MODEL), src),
            pl.BlockSpec((None, 1, D_MODEL), lay),
            pl.BlockSpec((None, N_PROJ, D_MODEL), lay),
            pl.BlockSpec((tm, 128), pos), pl.BlockSpec((tm, 128), pos),
            pl.BlockSpec((tm, 128), pos), pl.BlockSpec((tm, 128), pos),
            pl.BlockSpec((None, 1, 256), lay), pl.BlockSpec((None, 1, 256), lay),
            pl.BlockSpec((256, 256), const),
            pl.BlockSpec((None, 32, 256), lay), pl.BlockSpec((None, 1, 256), lay),
            pl.BlockSpec((None, 1, 256), lay), pl.BlockSpec((None, 1, 256), lay),
        ],
        out_specs=[
            pl.BlockSpec((tm, 768), tok), pl.BlockSpec((tm, 1280), tok),
            pl.BlockSpec((tm, 1024), tok), pl.BlockSpec((tm, 256), tok),
        ],
        out_shape=[
            jax.ShapeDtypeStruct((t, 768), F32), jax.ShapeDtypeStruct((t, 1280), F32),
            jax.ShapeDtypeStruct((t, 1024), F32), jax.ShapeDtypeStruct((t, 256), F32),
        ],
        scratch_shapes=[pltpu.VMEM((tm, D_MODEL), BF16), pltpu.VMEM((tm, D_MODEL), BF16),
                        pltpu.VMEM((tm + CONF_HALO + 8, 256), F32)],
        compiler_params=pltpu.CompilerParams(
            dimension_semantics=("arbitrary",), vmem_limit_bytes=56 * 1024 * 1024),
        name="inproj",
    )(x2d, g, w_cat, cosa, sina, cosc, sinc, gq, gk, bd, dw, db, lg, lb)


ATTN_TILE = 2048
ATTN_GROUP = 4


def _attn_kernel(*refs):
    q_refs, kp_refs, kc_refs, vp_refs, vc_refs = (refs[2 * i:2 * i + 2] for i in range(5))
    qm_ref, o_ref = refs[10], refs[11]
    m_refs, l_refs, u_refs = refs[12:16], refs[16:20], refs[20:22]
    tile = pl.program_id(1)

    ii = lax.broadcasted_iota(jnp.int32, (128, 256), 0)
    jj = lax.broadcasted_iota(jnp.int32, (128, 256), 1)
    band_bias = jnp.where((jj >= ii) & (jj <= ii + ATTN_BLOCK), 0.0, NEG).astype(F32)
    noprev_bias = jnp.where(jj < ATTN_BLOCK, NEG, 0.0).astype(F32)
    first_bias = band_bias + jnp.where(tile == 0, 1.0, 0.0).astype(F32) * noprev_bias
    lo64 = lax.broadcasted_iota(jnp.int32, (128, 128), 1) < HEAD_DIM

    def rows(start, dil):
        if dil == 1:
            return pl.ds(pl.multiple_of(start, 128), 128)
        return pl.ds(start, 128, stride=dil)

    def wide(pair, rr):
        return jnp.concatenate([pair[0][rr, :], pair[1][rr, :]], axis=1)

    def blocks(descs, dil, init):
        st = []
        for qs, klo_refs, vlo_refs, lo_s, hi_s, bias in descs:
            qr, lo, hi = rows(qs, dil), rows(lo_s, dil), rows(hi_s, dil)
            q = wide(q_refs, qr)
            k = jnp.concatenate([wide(klo_refs, lo), wide(kc_refs, hi)], axis=0).astype(BF16)
            v = [jnp.concatenate([vlo_refs[c][lo, :], vc_refs[c][hi, :]], axis=0).astype(BF16) for c in range(2)]
            qb = q.astype(BF16)
            q4 = jnp.concatenate([qb * qm_ref[h:h + 1, :] for h in range(GROUP_HEADS)], axis=0)
            st.append(dict(qr=qr, v=v, bias=bias, s_all=_dot_nt(q4, k)))
        if not init:
            for d in st:
                d["m_old"] = [m_refs[h][d["qr"], :] for h in range(GROUP_HEADS)]
                d["l_old"] = [l_refs[h][d["qr"], :] for h in range(GROUP_HEADS)]
        for d in st:
            d["m_new"], d["l_new"], d["alpha"], ps = [], [], [], []
            for h in range(GROUP_HEADS):
                s = d["s_all"][h * 128:(h + 1) * 128, :] + d["bias"]
                rmax = jnp.max(s, axis=-1, keepdims=True)
                m_new = jnp.broadcast_to(rmax, (128, 128)) if init else jnp.maximum(d["m_old"][h], rmax)
                p = jnp.exp2(s - jnp.concatenate([m_new, m_new], axis=1))
                rsum = jnp.sum(p, axis=-1, keepdims=True)
                if init:
                    d["l_new"].append(jnp.broadcast_to(rsum, (128, 128)))
                else:
                    alpha = jnp.exp2(d["m_old"][h] - m_new)
                    d["l_new"].append(alpha * d["l_old"][h] + rsum)
                    d["alpha"].append(alpha)
                d["m_new"].append(m_new)
                ps.append(p.astype(BF16))
            d["pv"] = []
            for c in range(2):
                pv = _dot(jnp.concatenate(ps[2 * c:2 * c + 2], axis=0), d["v"][c])
                d["pv"].append(jnp.where(lo64, pv[0:128, :], pv[128:256, :]))
        if not init:
            for d in st:
                d["u_old"] = [u_refs[c][d["qr"], :] for c in range(2)]
        for d in st:
            for h in range(GROUP_HEADS):
                m_refs[h][d["qr"], :] = d["m_new"][h]
                l_refs[h][d["qr"], :] = d["l_new"][h]
            for c in range(2):
                if init:
                    u_refs[c][d["qr"], :] = d["pv"][c]
                else:
                    a_c = jnp.where(lo64, d["alpha"][2 * c], d["alpha"][2 * c + 1])
                    u_refs[c][d["qr"], :] = d["u_old"][c] * a_c + d["pv"][c]

    def first_desc(r, span):
        return (r, kp_refs, vp_refs, ATTN_TILE - span + r, r, first_bias)

    def later_desc(r, m, span):
        base = r + (m - 1) * span
        return (base + span, kc_refs, vc_refs, base, base + span, band_bias)

    for pi, (_, dil) in enumerate(reversed(ATTN_PATTERNS)):
        span = ATTN_BLOCK * dil
        nblk = ATTN_TILE // span
        init = pi == 0
        grp = ATTN_GROUP
        if nblk == 1:
            def class_group(i, carry, dil=dil, span=span, init=init):
                blocks([first_desc(grp * i + j, span) for j in range(grp)], dil, init)
                return carry

            lax.fori_loop(0, dil // grp, class_group, 0)
        else:
            def per_class(r, carry, dil=dil, span=span, nblk=nblk, init=init):
                blocks([first_desc(r, span)] + [later_desc(r, j, span) for j in range(1, grp)], dil, init)

                def later_group(i, c):
                    blocks([later_desc(r, grp * (i + 1) + j, span) for j in range(grp)], dil, init)
                    return c

                if nblk > grp:
                    lax.fori_loop(0, nblk // grp - 1, later_group, 0)
                return carry

            lax.fori_loop(0, dil, per_class, 0)

    lo64w = lax.broadcasted_iota(jnp.int32, (256, 128), 1) < HEAD_DIM

    def finish(c, carry):
        rr = pl.ds(pl.multiple_of(c * 256, 256), 256)
        for hlf in range(2):
            den = jnp.where(lo64w, l_refs[2 * hlf][rr, :], l_refs[2 * hlf + 1][rr, :])
            o_ref[rr, hlf * 128:(hlf + 1) * 128] = u_refs[hlf][rr, :] / den
        return carry

    lax.fori_loop(0, ATTN_TILE // 256, finish, 0)


def _attention(qkv, qmask, bsz, seq):
    nt = seq // ATTN_TILE
    cur = lambda c: (lambda b, n: (b * nt + n, c))
    prev = lambda c: (lambda b, n: (b * nt + jnp.maximum(n - 1, 0), c))
    blk = (ATTN_TILE, 128)
    col_maps = [cur(0), cur(1),
                prev(2), prev(3), cur(2), cur(3),
                prev(4), prev(5), cur(4), cur(5)]
    return pl.pallas_call(
        _attn_kernel,
        grid=(bsz, nt),
        in_specs=[pl.BlockSpec(blk, m) for m in col_maps] + [pl.BlockSpec((16, 256), lambda b, n: (0, 0))],
        out_specs=pl.BlockSpec((ATTN_TILE, 256), lambda b, n: (b * nt + n, 0)),
        out_shape=jax.ShapeDtypeStruct((bsz * seq, GROUP_WIDTH), F32),
        scratch_shapes=[pltpu.VMEM((ATTN_TILE, 128), F32)] * 10,
        compiler_params=pltpu.CompilerParams(
            dimension_semantics=("arbitrary", "arbitrary"), vmem_limit_bytes=56 * 1024 * 1024),
        name="attn",
    )(*([qkv] * 10), qmask)


SEQ_TILE = 1024
SSM_ROWS = 64


def _ssd_kernel(u_ref, cw_ref, cb_ref, dtb_ref, a_ref, dsk_ref, ng_ref, ltri_ref, smask_ref,
                vm_ref, o_ref, ext_ref, act_ref, hs_ref):
    ts = SEQ_TILE

    @pl.when(pl.program_id(1) == 0)
    def _():
        ext_ref[0:8, :] = jnp.zeros((8, 768), F32)
        hs_ref[...] = jnp.zeros(hs_ref.shape, F32)

    ext_ref[8:8 + ts, :] = u_ref[:, 256:1024]
    for c in range(ts // SSM_ROWS):
        for gc in range(3):
            cols = slice(gc * 256, (gc + 1) * 256)
            win = ext_ref[pl.ds(c * SSM_ROWS, SSM_ROWS + 8), cols]
            acc = cb_ref[:, cols] + cw_ref[SSM_CONV - 1:SSM_CONV, cols] * win[8:8 + SSM_ROWS, :]
            for sh in range(1, SSM_CONV):
                k = SSM_CONV - 1 - sh
                acc = acc + cw_ref[k:k + 1, cols] * pltpu.roll(win, sh, axis=0)[8:8 + SSM_ROWS, :]
            act_ref[pl.ds(c * SSM_ROWS, SSM_ROWS), cols] = _silu(acc)
    ext_ref[0:8, :] = ext_ref[ts:ts + 8, :]

    ii = lax.broadcasted_iota(jnp.int32, (CHUNK, CHUNK), 0)
    jj = lax.broadcasted_iota(jnp.int32, (CHUNK, CHUNK), 1)
    causal = jj <= ii
    ltri = ltri_ref[...]

    hs = hs_ref[...]
    for c in range(ts // CHUNK):
        rr = pl.ds(c * CHUNK, CHUNK)
        z = u_ref[rr, 0:256]
        xs = act_ref[rr, 0:256]
        bm = act_ref[rr, 256:512]
        cm = act_ref[rr, 512:768]
        dt = _softplus(u_ref[rr, 1024:1280] + dtb_ref[...])
        da = dt * (-jnp.exp(a_ref[...]))
        da_hi = da.astype(BF16)
        rem = da - da_hi.astype(F32)
        da_mid = rem.astype(BF16)
        da_lo = (rem - da_mid.astype(F32)).astype(BF16)
        acum = _dot(ltri, da_hi) + _dot(ltri, da_mid) + _dot(ltri, da_lo)
        total = acum[CHUNK - 1:CHUNK, :]
        bmb = bm.astype(BF16)
        cmb = cm.astype(BF16)
        xdt = xs * dt
        acum_t = (jnp.transpose(acum[:, 0:128]), jnp.transpose(acum[:, 128:256]))
        cbs = [_dot_nt(cmb[:, g * 128:(g + 1) * 128], bmb[:, g * 128:(g + 1) * 128]) for g in range(SSM_GROUPS)]
        mhs = []
        for h in range(GROUP_HEADS):
            col = acum[:, h * HEAD_DIM:h * HEAD_DIM + 1]
            rsel = (h % 2) * HEAD_DIM
            row = acum_t[h // 2][rsel:rsel + 1, :]
            lmat = jnp.exp(jnp.where(causal, col - row, NEG))
            mhs.append((cbs[h // 2] * lmat).astype(BF16))
        xdtb = xdt.astype(BF16)
        xdt4 = jnp.concatenate([xdtb * vm_ref[h:h + 1, :] for h in range(GROUP_HEADS)], axis=0)
        y = _dot(cmb, hs.astype(BF16)) * jnp.exp(acum) + _dot(jnp.concatenate(mhs, axis=1), xdt4)
        xw = (xs * (jnp.exp(total - acum) * dt)).astype(BF16)
        hs = jnp.exp(total) * hs + _dot_tn(bmb, xw) * smask_ref[...]
        y = (y + dsk_ref[...] * xs) * _silu(z)
        halves = []
        for g in range(SSM_GROUPS):
            yg = y[:, g * 128:(g + 1) * 128]
            halves.append(yg * lax.rsqrt(jnp.mean(yg * yg, axis=-1, keepdims=True) + EPS))
        o_ref[rr, :] = jnp.concatenate(halves, axis=1) * ng_ref[...]
    hs_ref[...] = hs


def _ssd(ub, cw, cb, dtb, a_exp, dsk, ng, ltri, smask, vmask, bsz, seq, layer):
    ts = SEQ_TILE
    nt = seq // ts
    const = lambda b, n: (0, 0)
    lay = lambda b, n: (layer, 0, 0)
    tok = lambda b, n: (b * nt + n, 0)
    return pl.pallas_call(
        _ssd_kernel,
        grid=(bsz, nt),
        in_specs=[
            pl.BlockSpec((ts, 1280), tok),
            pl.BlockSpec((None, 8, 768), lay), pl.BlockSpec((None, 1, 768), lay),
            pl.BlockSpec((None, 1, 256), lay), pl.BlockSpec((None, 1, 256), lay),
            pl.BlockSpec((None, 1, 256), lay), pl.BlockSpec((None, 1, 256), lay),
            pl.BlockSpec((CHUNK, CHUNK), const), pl.BlockSpec((256, 256), const),
            pl.BlockSpec((16, 256), const),
        ],
        out_specs=pl.BlockSpec((ts, 256), tok),
        out_shape=jax.ShapeDtypeStruct((bsz * seq, GROUP_WIDTH), F32),
        scratch_shapes=[pltpu.VMEM((ts + 8, 768), F32), pltpu.VMEM((ts, 768), F32), pltpu.VMEM((256, 256), F32)],
        compiler_params=pltpu.CompilerParams(dimension_semantics=("arbitrary", "arbitrary")),
        name="ssd",
    )(ub, cw, cb, dtb, a_exp, dsk, ng, ltri, smask, vmask)


def _ret_kernel(u_ref, dmat_ref, zeta_ref, xi_ref, gch_ref, bd_ref, rmask_ref, qm_ref, vm_ref,
                o_ref, r_ref):
    @pl.when(pl.program_id(1) == 0)
    def _():
        r_ref[...] = jnp.zeros(r_ref.shape, F32)

    r = r_ref[...]
    for c in range(SEQ_TILE // CHUNK):
        rr = pl.ds(c * CHUNK, CHUNK)
        q = u_ref[rr, 0:256]
        k = u_ref[rr, 256:512]
        v = u_ref[rr, 512:768]
        kb = k.astype(BF16)
        qb = q.astype(BF16)
        q4 = jnp.concatenate([qb * qm_ref[h:h + 1, :] for h in range(GROUP_HEADS)], axis=0)
        s4 = _dot_nt(q4, kb) * dmat_ref[...]
        s_cat = jnp.concatenate([s4[h * CHUNK:(h + 1) * CHUNK, :] for h in range(GROUP_HEADS)], axis=1)
        vb = v.astype(BF16)
        v4 = jnp.concatenate([vb * vm_ref[h:h + 1, :] for h in range(GROUP_HEADS)], axis=0)
        y = _dot(qb, r.astype(BF16)) * xi_ref[...] + _dot(s_cat.astype(BF16), v4)
        o_ref[rr, :] = y
        kz = (k * zeta_ref[...]).astype(BF16)
        r = gch_ref[...] * r + _dot_tn(kz, vb) * rmask_ref[...]
    r_ref[...] = r
    for hlf in range(2):
        rr = pl.ds(hlf * (SEQ_TILE // 2), SEQ_TILE // 2)
        y = o_ref[rr, :]
        ss = _dot((y * y).astype(BF16), bd_ref[...])
        o_ref[rr, :] = y * lax.rsqrt(ss * (1.0 / HEAD_DIM) + EPS) * _silu(u_ref[rr, 768:1024])


def _retention(uc, dmat, zeta, xi, gch, bdn, rmask, qmask, vmask, bsz, seq):
    ts = SEQ_TILE
    nt = seq // ts
    const = lambda b, n: (0, 0)
    tok = lambda b, n: (b * nt + n, 0)
    return pl.pallas_call(
        _ret_kernel,
        grid=(bsz, nt),
        in_specs=[
            pl.BlockSpec((ts, 1024), tok),
            pl.BlockSpec((GROUP_HEADS * CHUNK, CHUNK), const),
            pl.BlockSpec((CHUNK, 256), const), pl.BlockSpec((CHUNK, 256), const),
            pl.BlockSpec((1, 256), const), pl.BlockSpec((256, 256), const),
            pl.BlockSpec((256, 256), const),
            pl.BlockSpec((16, 256), const), pl.BlockSpec((16, 256), const),
        ],
        out_specs=pl.BlockSpec((ts, 256), tok),
        out_shape=jax.ShapeDtypeStruct((bsz * seq, GROUP_WIDTH), F32),
        scratch_shapes=[pltpu.VMEM((256, 256), F32)],
        compiler_params=pltpu.CompilerParams(dimension_semantics=("arbitrary", "arbitrary")),
        name="retention",
    )(uc, dmat, zeta, xi, gch, bdn, rmask, qmask, vmask)


FF_CHUNK = 1024


def _rms_rows(x, g):
    return x * lax.rsqrt(jnp.mean(x * x, axis=-1, keepdims=True) + EPS) * g


def _ffn_kernel(x_ref, ya_ref, yb_ref, yc_ref, yd_ref, p_ref, wout_ref, gf_ref, wup_ref, wdn_ref,
                gp_ref, wple_ref, wgate_ref, o_ref, hb_ref):
    tm = x_ref.shape[0]
    halves = [pl.ds(0, tm // 2), pl.ds(tm // 2, tm // 2)]
    x1 = []
    for rr in halves:
        acc = x_ref[rr, :]
        for m, y_ref in enumerate((ya_ref, yb_ref, yc_ref, yd_ref)):
            acc = acc + _dot(y_ref[rr, :].astype(BF16), wout_ref[m * 256:(m + 1) * 256, :])
        hb_ref[rr, :] = _rms_rows(acc, gf_ref[...]).astype(BF16)
        x1.append(acc)

    def down(rr, c):
        up = jnp.maximum(_dot(hb_ref[rr, :], wup_ref[:, c * FF_CHUNK:(c + 1) * FF_CHUNK]), 0.0)
        return _dot((up * up).astype(BF16), wdn_ref[c * FF_CHUNK:(c + 1) * FF_CHUNK, :])

    ffs = [down(rr, 0) for rr in halves]
    for c in range(1, D_FF // FF_CHUNK):
        ffs = [ff + down(rr, c) for ff, rr in zip(ffs, halves)]
    for rr, acc, ff in zip(halves, x1, ffs):
        x2 = acc + ff
        gate = _sigmoid(_dot(_rms_rows(x2, gp_ref[...]).astype(BF16), wgate_ref[...]))
        o_ref[rr, :] = x2 + _dot(p_ref[rr, :].astype(BF16), wple_ref[...]) * gate


def _ffn(x2d, ya, yb, yc, yd, p_all, wout, gf, wup, wdn, gp, wple, wgate, tm, layer):
    t = x2d.shape[0]
    tok = lambda i: (i, 0)
    lay = lambda i: (layer, 0, 0)
    once = dict(pipeline_mode=pl.Buffered(1))
    return pl.pallas_call(
        _ffn_kernel,
        grid=(t // tm,),
        in_specs=[
            pl.BlockSpec((tm, D_MODEL), tok),
            pl.BlockSpec((tm, 256), tok), pl.BlockSpec((tm, 256), tok),
            pl.BlockSpec((tm, 256), tok), pl.BlockSpec((tm, 256), tok),
            pl.BlockSpec((None, tm, PLE_DIM), lambda i: (layer, i, 0)),
            pl.BlockSpec((None, D_MODEL, D_MODEL), lay, **once),
            pl.BlockSpec((None, 1, D_MODEL), lay),
            pl.BlockSpec((None, D_MODEL, D_FF), lay, **once),
            pl.BlockSpec((None, D_FF, D_MODEL), lay, **once),
            pl.BlockSpec((None, 1, D_MODEL), lay),
            pl.BlockSpec((None, PLE_DIM, D_MODEL), lay, **once),
            pl.BlockSpec((None, D_MODEL, D_MODEL), lay, **once),
        ],
        out_specs=pl.BlockSpec((tm, D_MODEL), tok),
        out_shape=jax.ShapeDtypeStruct((t, D_MODEL), F32),
        scratch_shapes=[pltpu.VMEM((tm, D_MODEL), BF16)],
        compiler_params=pltpu.CompilerParams(
            dimension_semantics=("arbitrary",), vmem_limit_bytes=56 * 1024 * 1024),
        name="ffn",
    )(x2d, ya, yb, yc, yd, p_all, wout, gf, wup, wdn, gp, wple, wgate)


def _head_mask(head_of_lane):
    m = np.zeros((8, GROUP_WIDTH), np.float32)
    for h in range(GROUP_HEADS):
        m[h] = head_of_lane == h
    return m


def _tables(seq):
    pos = jnp.arange(seq, dtype=F32)
    ang_a = ROPE_THETA ** (-jnp.arange(0, HEAD_DIM, 2, dtype=F32) / HEAD_DIM)
    ang_c = 1.0 / (10000.0 ** jnp.linspace(0.0, 1.0, HEAD_DIM // 2, dtype=F32))

    def cs(ang):
        a = pos[:, None] * ang[None, :]
        return jnp.tile(jnp.cos(a), (1, 4)), jnp.tile(jnp.sin(a), (1, 4))

    cosa, sina = cs(ang_a)
    cosc, sinc = cs(ang_c)
    log_g = jnp.log(1.0 - 2.0 ** (-5.0 - jnp.arange(GROUP_HEADS, dtype=F32)))
    idx = jnp.arange(CHUNK, dtype=F32)
    dist = idx[:, None] - idx[None, :]
    dmat = jnp.where((dist >= 0)[None], jnp.exp(jnp.maximum(dist, 0.0)[None] * log_g[:, None, None]), 0.0)
    zeta = jnp.exp((CHUNK - 1 - idx)[:, None] * log_g)[:, _HEAD_PERM]
    xi = jnp.exp((idx + 1.0)[:, None] * log_g)[:, _HEAD_NAT]
    gch = jnp.exp(CHUNK * log_g)[_HEAD_NAT][None, :]
    return dict(
        cosa=cosa, sina=sina, cosc=cosc, sinc=sinc, dmat=dmat, zeta=zeta, xi=xi, gch=gch,
        bd_perm=jnp.asarray(_HEAD_PERM[:, None] == _HEAD_PERM[None, :], BF16),
        bd_nat=jnp.asarray(_HEAD_NAT[:, None] == _HEAD_NAT[None, :], BF16),
        rmask=jnp.asarray(_HEAD_PERM[:, None] == _HEAD_NAT[None, :], F32),
        smask=jnp.asarray((_LANE[:, None] // SSM_STATE) == (_HEAD_NAT[None, :] // (GROUP_HEADS // SSM_GROUPS)), F32),
        qmask_bf=jnp.asarray(np.concatenate([_head_mask(_HEAD_PERM)] * 2), BF16),
        vmask_bf=jnp.asarray(np.concatenate([_head_mask(_HEAD_NAT)] * 2), BF16),
        ltri=jnp.asarray(np.tril(np.ones((CHUNK, CHUNK), np.float32)), BF16),
    )


def _build_w_cat_t(w_in):
    wt = jnp.transpose(w_in, (0, 2, 1))

    def nat(c0):
        return [wt[:, c0:c0 + 256]]

    def perm(c0):
        return [wt[:, c0 + h * HEAD_DIM + half * 32:c0 + h * HEAD_DIM + half * 32 + 32]
                for half in range(2) for h in range(GROUP_HEADS)]

    dt = [jnp.repeat(wt[:, _BDT:_BDT + GROUP_HEADS], HEAD_DIM, axis=1)]
    parts = (perm(_A0) + perm(_A0 + 256) + nat(_A0 + 512)
             + nat(_BZ) + nat(_BX) + nat(_BB) + nat(_BC) + dt
             + perm(_C0) + perm(_C0 + 256) + nat(_C0 + 512) + nat(_C0 + 768)
             + nat(_D0) + nat(_D0 + 256))
    return jnp.concatenate(parts, axis=1).astype(BF16)


def kernel(x, p, norm_mix, w_in, attn_q_norm, attn_k_norm, ssm_conv_w, ssm_conv_b, ssm_dt_bias,
           ssm_a_log, ssm_d, ssm_norm, conv_dw_w, conv_dw_b, conv_ln_g, conv_ln_b, w_out,
           norm_ffn, w_up, w_down, norm_ple, w_ple, w_ple_gate):
    bsz, seq, _ = x.shape
    depth = w_in.shape[0]
    tm = 512
    tb = _tables(seq)
    row = lambda a: a[:, None, :]
    w_cat = _build_w_cat_t(w_in)
    g_mix, g_ffn, g_ple = row(norm_mix), row(norm_ffn), row(norm_ple)
    gq, gk = row(attn_q_norm[:, _PERM % HEAD_DIM]), row(attn_k_norm[:, _PERM % HEAD_DIM])
    cw = jnp.pad(ssm_conv_w, ((0, 0), (0, 8 - SSM_CONV), (0, 0)))
    dw = jnp.pad(conv_dw_w, ((0, 0), (0, 32 - CONF_KERNEL), (0, 0)))
    dtb, alog, dsk = row(ssm_dt_bias[:, _HEAD_NAT]), row(ssm_a_log[:, _HEAD_NAT]), row(ssm_d[:, _HEAD_NAT])
    wout, wup, wdn = w_out.astype(BF16), w_up.astype(BF16), w_down.astype(BF16)
    wple, wgate = w_ple.astype(BF16), w_ple_gate.astype(BF16)
    p_all = p.reshape(depth, bsz * seq, PLE_DIM)
    x2d = x.reshape(bsz * seq, D_MODEL)
    for i in range(depth):
        oa, ob, oc, yd = _inproj(x2d, g_mix, w_cat, tb["cosa"], tb["sina"], tb["cosc"], tb["sinc"],
                                 gq, gk, tb["bd_perm"], dw, row(conv_dw_b), row(conv_ln_g), row(conv_ln_b),
                                 seq, tm, i)
        ya = _attention(oa, tb["qmask_bf"], bsz, seq)
        yb = _ssd(ob, cw, row(ssm_conv_b), dtb, alog, dsk, row(ssm_norm), tb["ltri"], tb["smask"],
                  tb["vmask_bf"], bsz, seq, i)
        yc = _retention(oc, tb["dmat"].reshape(GROUP_HEADS * CHUNK, CHUNK), tb["zeta"], tb["xi"], tb["gch"], tb["bd_nat"], tb["rmask"],
                        tb["qmask_bf"], tb["vmask_bf"], bsz, seq)
        x2d = _ffn(x2d, ya, yb, yc, yd, p_all, wout, g_ffn, wup, wdn, g_ple, wple, wgate, tm, i)
    return x2d.reshape(bsz, seq, D_MODEL)
```

```python
import functools

import numpy as np
import jax
import jax.numpy as jnp
from jax import lax
from jax.experimental import pallas as pl
from jax.experimental.pallas import tpu as pltpu

F32 = jnp.float32
BF16 = jnp.bfloat16

D_MODEL = 1024
GROUP_WIDTH = 256
GROUP_HEADS = 4
HEAD_DIM = 64
EPS = 1e-6
ATTN_PATTERNS = ((128, 1), (512, 4), (2048, 16))
ATTN_BLOCK = 128
ROPE_THETA = 10000.0
SSM_STATE = 128
SSM_GROUPS = 2
SSM_CONV = 4
CHUNK = 128
CONF_KERNEL = 31
D_FF = 4 * D_MODEL
PLE_DIM = 256
NEG = -1e30
LOG2E = 1.4426950408889634
CONF_HALO = 32
CONF_ROWS = 64

_A0 = 0
_B0 = 3 * GROUP_WIDTH
_BZ, _BX, _BB, _BC = _B0, _B0 + 256, _B0 + 512, _B0 + 768
_BDT = _B0 + 1024
_C0 = _BDT + GROUP_HEADS
_D0 = _C0 + 4 * GROUP_WIDTH
IN_COLS = _D0 + 2 * GROUP_WIDTH

_LANE = np.arange(GROUP_WIDTH)
_PERM = ((_LANE % 128) // 32) * HEAD_DIM + (_LANE // 128) * 32 + (_LANE % 32)
_HEAD_PERM = (_LANE % 128) // 32
_HEAD_NAT = _LANE // HEAD_DIM

N_PROJ = 14 * GROUP_WIDTH


def _sigmoid(x):
    return 1.0 / (1.0 + jnp.exp(-x))


def _silu(x):
    return x * _sigmoid(x)


def _softplus(x):
    return jnp.maximum(x, 0.0) + jnp.log(1.0 + jnp.exp(-jnp.abs(x)))


def _dot(a, b):
    return jnp.dot(a, b, preferred_element_type=F32)


def _dot_nt(a, b):
    return lax.dot_general(a, b, (((1,), (1,)), ((), ())), preferred_element_type=F32)


def _dot_tn(a, b):
    return lax.dot_general(a, b, (((0,), (0,)), ((), ())), preferred_element_type=F32)


def _inproj_kernel(x_ref, g_ref, w_ref, cosa_ref, sina_ref, cosc_ref, sinc_ref,
                   gq_ref, gk_ref, bd_ref, dw_ref, db_ref, lg_ref, lb_ref,
                   oa_ref, ob_ref, oc_ref, yd_ref, hb0_ref, hb1_ref, ext_ref, *, nseq):
    step = pl.program_id(0)
    tm = oa_ref.shape[0]

    def norm_into(dst_ref):
        x = x_ref[...]
        h = x * lax.rsqrt(jnp.mean(x * x, axis=-1, keepdims=True) + EPS) * g_ref[...]
        dst_ref[...] = h.astype(BF16)

    @pl.when(step == 0)
    def _():
        norm_into(hb0_ref)
        ext_ref[CONF_HALO + tm:CONF_HALO + tm + 8, :] = jnp.zeros((8, 256), F32)

    @pl.when(step % nseq == 1)
    def _():
        ext_ref[0:CONF_HALO, :] = jnp.zeros((CONF_HALO, 256), F32)

    rest = (w_ref, cosa_ref, sina_ref, cosc_ref, sinc_ref, gq_ref, gk_ref, bd_ref, dw_ref, db_ref, lg_ref,
            lb_ref, oa_ref, ob_ref, oc_ref, yd_ref, ext_ref)

    @pl.when((step > 0) & (step % 2 == 0))
    def _():
        norm_into(hb0_ref)
        _inproj_body(hb1_ref, *rest)

    @pl.when(step % 2 == 1)
    def _():
        norm_into(hb1_ref)
        _inproj_body(hb0_ref, *rest)


def _inproj_body(hb_ref, w_ref, cosa_ref, sina_ref, cosc_ref, sinc_ref, gq_ref, gk_ref, bd_ref,
                 dw_ref, db_ref, lg_ref, lb_ref, oa_ref, ob_ref, oc_ref, yd_ref, ext_ref):
    tm = oa_ref.shape[0]

    def mm(j):
        return _dot_nt(hb_ref[...], w_ref[j * 256:(j + 1) * 256, :])

    def rot(t, cos, sin):
        t1, t2 = t[:, :128], t[:, 128:]
        return jnp.concatenate([t1 * cos - t2 * sin, t2 * cos + t1 * sin], axis=1)

    def headnorm(t, gain):
        ss = _dot((t * t).astype(BF16), bd_ref[...])
        return t * lax.rsqrt(ss * (1.0 / HEAD_DIM) + EPS) * gain

    cosa, sina = cosa_ref[...], sina_ref[...]
    cosc, sinc = cosc_ref[...], sinc_ref[...]
    scale = HEAD_DIM ** -0.5
    ext_ref[CONF_HALO:CONF_HALO + tm, :] = mm(12) * _sigmoid(mm(13))
    conv_chunks = iter(range(tm // CONF_ROWS))

    def conv_some(n):
        for _ in range(n):
            c = next(conv_chunks, None)
            if c is not None:
                _conformer_rows(ext_ref, dw_ref, db_ref, lg_ref, lb_ref, yd_ref, c)

    oa_ref[:, 0:256] = mm(0)
    conv_some(1)
    oa_ref[:, 256:512] = mm(1)
    conv_some(1)
    oa_ref[:, 512:768] = mm(2)
    conv_some(1)
    for j in range(5):
        ob_ref[:, j * 256:(j + 1) * 256] = mm(3 + j)
        conv_some(1)
    oa_ref[:, 0:256] = rot(headnorm(oa_ref[:, 0:256], gq_ref[...]), cosa, sina) * (scale * LOG2E)
    oa_ref[:, 256:512] = rot(headnorm(oa_ref[:, 256:512], gk_ref[...]), cosa, sina)
    oc_ref[:, 0:256] = rot(mm(8), cosc, sinc)
    oc_ref[:, 256:512] = rot(mm(9), cosc, sinc) * scale
    oc_ref[:, 512:768] = mm(10)
    oc_ref[:, 768:1024] = mm(11)
    conv_some(tm // CONF_ROWS)
    ext_ref[0:CONF_HALO, :] = ext_ref[tm:tm + CONF_HALO, :]


def _conformer_rows(ext_ref, w_ref, b_ref, lg_ref, lb_ref, o_ref, c):
    nwin = CONF_ROWS + CONF_HALO + 8
    r0 = c * CONF_ROWS
    win = ext_ref[pl.ds(r0, nwin), :]
    acc = jnp.zeros((CONF_ROWS, 256), F32) + b_ref[...]
    for off in range(8):
        shifted = win if off == 0 else pltpu.roll(win, nwin - off, axis=0)
        for sh in range(CONF_HALO - CONF_KERNEL + 1, CONF_HALO + 1):
            if sh % 8 == off:
                k = sh - (CONF_HALO - CONF_KERNEL + 1)
                acc = acc + w_ref[k:k + 1, :] * shifted[sh - off:sh - off + CONF_ROWS, :]
    mu = jnp.mean(acc, axis=-1, keepdims=True)
    xc = acc - mu
    var = jnp.mean(xc * xc, axis=-1, keepdims=True)
    o_ref[pl.ds(r0, CONF_ROWS), :] = _silu(xc * lax.rsqrt(var + EPS) * lg_ref[...] + lb_ref[...])


def _inproj(x2d, g, w_cat, cosa, sina, cosc, sinc, gq, gk, bd, dw, db, lg, lb, seq, tm, layer):
    t = x2d.shape[0]
    nseq = seq // tm
    const = lambda i: (0, 0)
    lay = lambda i: (layer, 0, 0)
    nt = t // tm
    src = lambda i: (jnp.minimum(i, nt - 1), 0)
    tok = lambda i: (jnp.maximum(i - 1, 0), 0)
    pos = lambda i: (jnp.maximum(i - 1, 0) % nseq, 0)
    return pl.pallas_call(
        functools.partial(_inproj_kernel, nseq=nseq),
        grid=(nt + 1,),
        in_specs=[
            pl.BlockSpec((tm, D_MODEL), src),
            pl.BlockSpec((None, 1, D_MODEL), lay),
            pl.BlockSpec((None, N_PROJ, D_MODEL), lay),
            pl.BlockSpec((tm, 128), pos), pl.BlockSpec((tm, 128), pos),
            pl.BlockSpec((tm, 128), pos), pl.BlockSpec((tm, 128), pos),
            pl.BlockSpec((None, 1, 256), lay), pl.BlockSpec((None, 1, 256), lay),
            pl.BlockSpec((256, 256), const),
            pl.BlockSpec((None, 32, 256), lay), pl.BlockSpec((None, 1, 256), lay),
            pl.BlockSpec((None, 1, 256), lay), pl.BlockSpec((None, 1, 256), lay),
        ],
        out_specs=[
            pl.BlockSpec((tm, 768), tok), pl.BlockSpec((tm, 1280), tok),
            pl.BlockSpec((tm, 1024), tok), pl.BlockSpec((tm, 256), tok),
        ],
        out_shape=[
            jax.ShapeDtypeStruct((t, 768), F32), jax.ShapeDtypeStruct((t, 1280), F32),
            jax.ShapeDtypeStruct((t, 1024), F32), jax.ShapeDtypeStruct((t, 256), F32),
        ],
        scratch_shapes=[pltpu.VMEM((tm, D_MODEL), BF16), pltpu.VMEM((tm, D_MODEL), BF16),
                        pltpu.VMEM((tm + CONF_HALO + 8, 256), F32)],
        compiler_params=pltpu.CompilerParams(
            dimension_semantics=("arbitrary",), vmem_limit_bytes=56 * 1024 * 1024),
        name="inproj",
    )(x2d, g, w_cat, cosa, sina, cosc, sinc, gq, gk, bd, dw, db, lg, lb)


ATTN_TILE = 2048
ATTN_GROUP = 4


def _attn_kernel(*refs):
    q_refs, kp_refs, kc_refs, vp_refs, vc_refs = (refs[2 * i:2 * i + 2] for i in range(5))
    qm_ref, o_ref = refs[10], refs[11]
    m_refs, l_refs, u_refs = refs[12:16], refs[16:20], refs[20:22]
    tile = pl.program_id(1)

    ii = lax.broadcasted_iota(jnp.int32, (128, 256), 0)
    jj = lax.broadcasted_iota(jnp.int32, (128, 256), 1)
    band_bias = jnp.where((jj >= ii) & (jj <= ii + ATTN_BLOCK), 0.0, NEG).astype(F32)
    noprev_bias = jnp.where(jj < ATTN_BLOCK, NEG, 0.0).astype(F32)
    first_bias = band_bias + jnp.where(tile == 0, 1.0, 0.0).astype(F32) * noprev_bias
    lo64 = lax.broadcasted_iota(jnp.int32, (128, 128), 1) < HEAD_DIM

    def rows(start, dil):
        if dil == 1:
            return pl.ds(pl.multiple_of(start, 128), 128)
        return pl.ds(start, 128, stride=dil)

    def wide(pair, rr):
        return jnp.concatenate([pair[0][rr, :], pair[1][rr, :]], axis=1)

    def blocks(descs, dil, init):
        st = []
        for qs, klo_refs, vlo_refs, lo_s, hi_s, bias in descs:
            qr, lo, hi = rows(qs, dil), rows(lo_s, dil), rows(hi_s, dil)
            q = wide(q_refs, qr)
            k = jnp.concatenate([wide(klo_refs, lo), wide(kc_refs, hi)], axis=0).astype(BF16)
            v = [jnp.concatenate([vlo_refs[c][lo, :], vc_refs[c][hi, :]], axis=0).astype(BF16) for c in range(2)]
            qb = q.astype(BF16)
            q4 = jnp.concatenate([qb * qm_ref[h:h + 1, :] for h in range(GROUP_HEADS)], axis=0)
            st.append(dict(qr=qr, v=v, bias=bias, s_all=_dot_nt(q4, k)))
        if not init:
            for d in st:
                d["m_old"] = [m_refs[h][d["qr"], :] for h in range(GROUP_HEADS)]
                d["l_old"] = [l_refs[h][d["qr"], :] for h in range(GROUP_HEADS)]
        for d in st:
            d["m_new"], d["l_new"], d["alpha"], ps = [], [], [], []
            for h in range(GROUP_HEADS):
                s = d["s_all"][h * 128:(h + 1) * 128, :] + d["bias"]
                rmax = jnp.max(s, axis=-1, keepdims=True)
                m_new = jnp.broadcast_to(rmax, (128, 128)) if init else jnp.maximum(d["m_old"][h], rmax)
                p = jnp.exp2(s - jnp.concatenate([m_new, m_new], axis=1))
                rsum = jnp.sum(p, axis=-1, keepdims=True)
                if init:
                    d["l_new"].append(jnp.broadcast_to(rsum, (128, 128)))
                else:
                    alpha = jnp.exp2(d["m_old"][h] - m_new)
                    d["l_new"].append(alpha * d["l_old"][h] + rsum)
                    d["alpha"].append(alpha)
                d["m_new"].append(m_new)
                ps.append(p.astype(BF16))
            d["pv"] = []
            for c in range(2):
                pv = _dot(jnp.concatenate(ps[2 * c:2 * c + 2], axis=0), d["v"][c])
                d["pv"].append(jnp.where(lo64, pv[0:128, :], pv[128:256, :]))
        if not init:
            for d in st:
                d["u_old"] = [u_refs[c][d["qr"], :] for c in range(2)]
        for d in st:
            for h in range(GROUP_HEADS):
                m_refs[h][d["qr"], :] = d["m_new"][h]
                l_refs[h][d["qr"], :] = d["l_new"][h]
            for c in range(2):
                if init:
                    u_refs[c][d["qr"], :] = d["pv"][c]
                else:
                    a_c = jnp.where(lo64, d["alpha"][2 * c], d["alpha"][2 * c + 1])
                    u_refs[c][d["qr"], :] = d["u_old"][c] * a_c + d["pv"][c]

    def first_desc(r, span):
        return (r, kp_refs, vp_refs, ATTN_TILE - span + r, r, first_bias)

    def later_desc(r, m, span):
        base = r + (m - 1) * span
        return (base + span, kc_refs, vc_refs, base, base + span, band_bias)

    for pi, (_, dil) in enumerate(reversed(ATTN_PATTERNS)):
        span = ATTN_BLOCK * dil
        nblk = ATTN_TILE // span
        init = pi == 0
        grp = ATTN_GROUP
        if nblk == 1:
            def class_group(i, carry, dil=dil, span=span, init=init):
                blocks([first_desc(grp * i + j, span) for j in range(grp)], dil, init)
                return carry

            lax.fori_loop(0, dil // grp, class_group, 0)
        else:
            def per_class(r, carry, dil=dil, span=span, nblk=nblk, init=init):
                blocks([first_desc(r, span)] + [later_desc(r, j, span) for j in range(1, grp)], dil, init)

                def later_group(i, c):
                    blocks([later_desc(r, grp * (i + 1) + j, span) for j in range(grp)], dil, init)
                    return c

                if nblk > grp:
                    lax.fori_loop(0, nblk // grp - 1, later_group, 0)
                return carry

            lax.fori_loop(0, dil, per_class, 0)

    lo64w = lax.broadcasted_iota(jnp.int32, (256, 128), 1) < HEAD_DIM

    def finish(c, carry):
        rr = pl.ds(pl.multiple_of(c * 256, 256), 256)
        for hlf in range(2):
            den = jnp.where(lo64w, l_refs[2 * hlf][rr, :], l_refs[2 * hlf + 1][rr, :])
            o_ref[rr, hlf * 128:(hlf + 1) * 128] = u_refs[hlf][rr, :] / den
        return carry

    lax.fori_loop(0, ATTN_TILE // 256, finish, 0)


def _attention(qkv, qmask, bsz, seq):
    nt = seq // ATTN_TILE
    cur = lambda c: (lambda b, n: (b * nt + n, c))
    prev = lambda c: (lambda b, n: (b * nt + jnp.maximum(n - 1, 0), c))
    blk = (ATTN_TILE, 128)
    col_maps = [cur(0), cur(1),
                prev(2), prev(3), cur(2), cur(3),
                prev(4), prev(5), cur(4), cur(5)]
    return pl.pallas_call(
        _attn_kernel,
        grid=(bsz, nt),
        in_specs=[pl.BlockSpec(blk, m) for m in col_maps] + [pl.BlockSpec((16, 256), lambda b, n: (0, 0))],
        out_specs=pl.BlockSpec((ATTN_TILE, 256), lambda b, n: (b * nt + n, 0)),
        out_shape=jax.ShapeDtypeStruct((bsz * seq, GROUP_WIDTH), F32),
        scratch_shapes=[pltpu.VMEM((ATTN_TILE, 128), F32)] * 10,
        compiler_params=pltpu.CompilerParams(
            dimension_semantics=("arbitrary", "arbitrary"), vmem_limit_bytes=56 * 1024 * 1024),
        name="attn",
    )(*([qkv] * 10), qmask)


SEQ_TILE = 1024
SSM_ROWS = 64


def _ssd_tile(u_ref, cw_ref, cb_ref, dtb_ref, a_ref, dsk_ref, ng_ref, ltri_ref, smask_ref,
              vm_ref, o_ref, ext_ref, act_ref, hs_ref):
    ts = SEQ_TILE

    ext_ref[8:8 + ts, :] = u_ref[:, 256:1024]
    for c in range(ts // SSM_ROWS):
        for gc in range(3):
            cols = slice(gc * 256, (gc + 1) * 256)
            win = ext_ref[pl.ds(c * SSM_ROWS, SSM_ROWS + 8), cols]
            acc = cb_ref[:, cols] + cw_ref[SSM_CONV - 1:SSM_CONV, cols] * win[8:8 + SSM_ROWS, :]
            for sh in range(1, SSM_CONV):
                k = SSM_CONV - 1 - sh
                acc = acc + cw_ref[k:k + 1, cols] * pltpu.roll(win, sh, axis=0)[8:8 + SSM_ROWS, :]
            act_ref[pl.ds(c * SSM_ROWS, SSM_ROWS), cols] = _silu(acc)
    ext_ref[0:8, :] = ext_ref[ts:ts + 8, :]

    ii = lax.broadcasted_iota(jnp.int32, (CHUNK, CHUNK), 0)
    jj = lax.broadcasted_iota(jnp.int32, (CHUNK, CHUNK), 1)
    causal = jj <= ii
    ltri = ltri_ref[...]

    hs = hs_ref[...]
    for c in range(ts // CHUNK):
        rr = pl.ds(c * CHUNK, CHUNK)
        z = u_ref[rr, 0:256]
        xs = act_ref[rr, 0:256]
        bm = act_ref[rr, 256:512]
        cm = act_ref[rr, 512:768]
        dt = _softplus(u_ref[rr, 1024:1280] + dtb_ref[...])
        da = dt * (-jnp.exp(a_ref[...]))
        da_hi = da.astype(BF16)
        rem = da - da_hi.astype(F32)
        da_mid = rem.astype(BF16)
        da_lo = (rem - da_mid.astype(F32)).astype(BF16)
        acum = _dot(ltri, da_hi) + _dot(ltri, da_mid) + _dot(ltri, da_lo)
        total = acum[CHUNK - 1:CHUNK, :]
        bmb = bm.astype(BF16)
        cmb = cm.astype(BF16)
        xdt = xs * dt
        acum_t = (jnp.transpose(acum[:, 0:128]), jnp.transpose(acum[:, 128:256]))
        cbs = [_dot_nt(cmb[:, g * 128:(g + 1) * 128], bmb[:, g * 128:(g + 1) * 128]) for g in range(SSM_GROUPS)]
        mhs = []
        for h in range(GROUP_HEADS):
            col = acum[:, h * HEAD_DIM:h * HEAD_DIM + 1]
            rsel = (h % 2) * HEAD_DIM
            row = acum_t[h // 2][rsel:rsel + 1, :]
            lmat = jnp.exp(jnp.where(causal, col - row, NEG))
            mhs.append((cbs[h // 2] * lmat).astype(BF16))
        xdtb = xdt.astype(BF16)
        xdt4 = jnp.concatenate([xdtb * vm_ref[h:h + 1, :] for h in range(GROUP_HEADS)], axis=0)
        y = _dot(cmb, hs.astype(BF16)) * jnp.exp(acum) + _dot(jnp.concatenate(mhs, axis=1), xdt4)
        xw = (xs * (jnp.exp(total - acum) * dt)).astype(BF16)
        hs = jnp.exp(total) * hs + _dot_tn(bmb, xw) * smask_ref[...]
        y = (y + dsk_ref[...] * xs) * _silu(z)
        halves = []
        for g in range(SSM_GROUPS):
            yg = y[:, g * 128:(g + 1) * 128]
            halves.append(yg * lax.rsqrt(jnp.mean(yg * yg, axis=-1, keepdims=True) + EPS))
        o_ref[rr, :] = jnp.concatenate(halves, axis=1) * ng_ref[...]
    hs_ref[...] = hs


def _ret_tile(u_ref, dmat_ref, zeta_ref, xi_ref, gch_ref, bd_ref, rmask_ref, qm_ref, vm_ref,
              o_ref, r_ref):
    r = r_ref[...]
    for c in range(SEQ_TILE // CHUNK):
        rr = pl.ds(c * CHUNK, CHUNK)
        q = u_ref[rr, 0:256]
        k = u_ref[rr, 256:512]
        v = u_ref[rr, 512:768]
        kb = k.astype(BF16)
        qb = q.astype(BF16)
        q4 = jnp.concatenate([qb * qm_ref[h:h + 1, :] for h in range(GROUP_HEADS)], axis=0)
        s4 = _dot_nt(q4, kb) * dmat_ref[...]
        s_cat = jnp.concatenate([s4[h * CHUNK:(h + 1) * CHUNK, :] for h in range(GROUP_HEADS)], axis=1)
        vb = v.astype(BF16)
        v4 = jnp.concatenate([vb * vm_ref[h:h + 1, :] for h in range(GROUP_HEADS)], axis=0)
        y = _dot(qb, r.astype(BF16)) * xi_ref[...] + _dot(s_cat.astype(BF16), v4)
        o_ref[rr, :] = y
        kz = (k * zeta_ref[...]).astype(BF16)
        r = gch_ref[...] * r + _dot_tn(kz, vb) * rmask_ref[...]
    r_ref[...] = r
    for hlf in range(2):
        rr = pl.ds(hlf * (SEQ_TILE // 2), SEQ_TILE // 2)
        y = o_ref[rr, :]
        ss = _dot((y * y).astype(BF16), bd_ref[...])
        o_ref[rr, :] = y * lax.rsqrt(ss * (1.0 / HEAD_DIM) + EPS) * _silu(u_ref[rr, 768:1024])


def _seqmix_kernel(*refs):
    ssd_in, ret_in = refs[0:10], refs[10:19]
    yb_ref, yc_ref, ext_ref, act_ref, hs_ref, r_ref = refs[19:25]

    @pl.when(pl.program_id(1) == 0)
    def _():
        ext_ref[0:8, :] = jnp.zeros((8, 768), F32)
        hs_ref[...] = jnp.zeros(hs_ref.shape, F32)
        r_ref[...] = jnp.zeros(r_ref.shape, F32)

    _ssd_tile(*ssd_in, yb_ref, ext_ref, act_ref, hs_ref)
    _ret_tile(*ret_in, yc_ref, r_ref)


def _seqmix(ub, cw, cb, dtb, a_exp, dsk, ng, ltri, smask, uc, dmat, zeta, xi, gch, bdn, rmask, qmask, vmask,
            bsz, seq, layer):
    ts = SEQ_TILE
    nt = seq // ts
    const = lambda b, n: (0, 0)
    lay = lambda b, n: (layer, 0, 0)
    tok = lambda b, n: (b * nt + n, 0)
    return pl.pallas_call(
        _seqmix_kernel,
        grid=(bsz, nt),
        in_specs=[
            pl.BlockSpec((ts, 1280), tok),
            pl.BlockSpec((None, 8, 768), lay), pl.BlockSpec((None, 1, 768), lay),
            pl.BlockSpec((None, 1, 256), lay), pl.BlockSpec((None, 1, 256), lay),
            pl.BlockSpec((None, 1, 256), lay), pl.BlockSpec((None, 1, 256), lay),
            pl.BlockSpec((CHUNK, CHUNK), const), pl.BlockSpec((256, 256), const),
            pl.BlockSpec((16, 256), const),
            pl.BlockSpec((ts, 1024), tok),
            pl.BlockSpec((GROUP_HEADS * CHUNK, CHUNK), const),
            pl.BlockSpec((CHUNK, 256), const), pl.BlockSpec((CHUNK, 256), const),
            pl.BlockSpec((1, 256), const), pl.BlockSpec((256, 256), const),
            pl.BlockSpec((256, 256), const),
            pl.BlockSpec((16, 256), const), pl.BlockSpec((16, 256), const),
        ],
        out_specs=[pl.BlockSpec((ts, 256), tok), pl.BlockSpec((ts, 256), tok)],
        out_shape=[jax.ShapeDtypeStruct((bsz * seq, GROUP_WIDTH), F32)] * 2,
        scratch_shapes=[pltpu.VMEM((ts + 8, 768), F32), pltpu.VMEM((ts, 768), F32), pltpu.VMEM((256, 256), F32),
                        pltpu.VMEM((256, 256), F32)],
        compiler_params=pltpu.CompilerParams(
            dimension_semantics=("arbitrary", "arbitrary"), vmem_limit_bytes=56 * 1024 * 1024),
        name="seqmix",
    )(ub, cw, cb, dtb, a_exp, dsk, ng, ltri, smask, vmask, uc, dmat, zeta, xi, gch, bdn, rmask, qmask, vmask)


FF_CHUNK = 1024


def _rms_rows(x, g):
    return x * lax.rsqrt(jnp.mean(x * x, axis=-1, keepdims=True) + EPS) * g


def _ffn_kernel(x_ref, ya_ref, yb_ref, yc_ref, yd_ref, p_ref, wout_ref, gf_ref, wup_ref, wdn_ref,
                gp_ref, wple_ref, wgate_ref, o_ref, hb_ref):
    tm = x_ref.shape[0]
    halves = [pl.ds(0, tm // 2), pl.ds(tm // 2, tm // 2)]
    x1 = []
    for rr in halves:
        acc = x_ref[rr, :]
        for m, y_ref in enumerate((ya_ref, yb_ref, yc_ref, yd_ref)):
            acc = acc + _dot(y_ref[rr, :].astype(BF16), wout_ref[m * 256:(m + 1) * 256, :])
        hb_ref[rr, :] = _rms_rows(acc, gf_ref[...]).astype(BF16)
        x1.append(acc)

    def down(rr, c):
        up = jnp.maximum(_dot(hb_ref[rr, :], wup_ref[:, c * FF_CHUNK:(c + 1) * FF_CHUNK]), 0.0)
        return _dot((up * up).astype(BF16), wdn_ref[c * FF_CHUNK:(c + 1) * FF_CHUNK, :])

    ffs = [down(rr, 0) for rr in halves]
    for c in range(1, D_FF // FF_CHUNK):
        ffs = [ff + down(rr, c) for ff, rr in zip(ffs, halves)]
    for rr, acc, ff in zip(halves, x1, ffs):
        x2 = acc + ff
        gate = _sigmoid(_dot(_rms_rows(x2, gp_ref[...]).astype(BF16), wgate_ref[...]))
        o_ref[rr, :] = x2 + _dot(p_ref[rr, :].astype(BF16), wple_ref[...]) * gate


def _ffn(x2d, ya, yb, yc, yd, p_all, wout, gf, wup, wdn, gp, wple, wgate, tm, layer):
    t = x2d.shape[0]
    tok = lambda i: (i, 0)
    lay = lambda i: (layer, 0, 0)
    once = dict(pipeline_mode=pl.Buffered(1))
    return pl.pallas_call(
        _ffn_kernel,
        grid=(t // tm,),
        in_specs=[
            pl.BlockSpec((tm, D_MODEL), tok),
            pl.BlockSpec((tm, 256), tok), pl.BlockSpec((tm, 256), tok),
            pl.BlockSpec((tm, 256), tok), pl.BlockSpec((tm, 256), tok),
            pl.BlockSpec((None, tm, PLE_DIM), lambda i: (layer, i, 0)),
            pl.BlockSpec((None, D_MODEL, D_MODEL), lay, **once),
            pl.BlockSpec((None, 1, D_MODEL), lay),
            pl.BlockSpec((None, D_MODEL, D_FF), lay, **once),
            pl.BlockSpec((None, D_FF, D_MODEL), lay, **once),
            pl.BlockSpec((None, 1, D_MODEL), lay),
            pl.BlockSpec((None, PLE_DIM, D_MODEL), lay, **once),
            pl.BlockSpec((None, D_MODEL, D_MODEL), lay, **once),
        ],
        out_specs=pl.BlockSpec((tm, D_MODEL), tok),
        out_shape=jax.ShapeDtypeStruct((t, D_MODEL), F32),
        scratch_shapes=[pltpu.VMEM((tm, D_MODEL), BF16)],
        compiler_params=pltpu.CompilerParams(
            dimension_semantics=("arbitrary",), vmem_limit_bytes=56 * 1024 * 1024),
        name="ffn",
    )(x2d, ya, yb, yc, yd, p_all, wout, gf, wup, wdn, gp, wple, wgate)


def _head_mask(head_of_lane):
    m = np.zeros((8, GROUP_WIDTH), np.float32)
    for h in range(GROUP_HEADS):
        m[h] = head_of_lane == h
    return m


def _tables(seq):
    pos = jnp.arange(seq, dtype=F32)
    ang_a = ROPE_THETA ** (-jnp.arange(0, HEAD_DIM, 2, dtype=F32) / HEAD_DIM)
    ang_c = 1.0 / (10000.0 ** jnp.linspace(0.0, 1.0, HEAD_DIM // 2, dtype=F32))

    def cs(ang):
        a = pos[:, None] * ang[None, :]
        return jnp.tile(jnp.cos(a), (1, 4)), jnp.tile(jnp.sin(a), (1, 4))

    cosa, sina = cs(ang_a)
    cosc, sinc = cs(ang_c)
    log_g = jnp.log(1.0 - 2.0 ** (-5.0 - jnp.arange(GROUP_HEADS, dtype=F32)))
    idx = jnp.arange(CHUNK, dtype=F32)
    dist = idx[:, None] - idx[None, :]
    dmat = jnp.where((dist >= 0)[None], jnp.exp(jnp.maximum(dist, 0.0)[None] * log_g[:, None, None]), 0.0)
    zeta = jnp.exp((CHUNK - 1 - idx)[:, None] * log_g)[:, _HEAD_PERM]
    xi = jnp.exp((idx + 1.0)[:, None] * log_g)[:, _HEAD_NAT]
    gch = jnp.exp(CHUNK * log_g)[_HEAD_NAT][None, :]
    return dict(
        cosa=cosa, sina=sina, cosc=cosc, sinc=sinc, dmat=dmat, zeta=zeta, xi=xi, gch=gch,
        bd_perm=jnp.asarray(_HEAD_PERM[:, None] == _HEAD_PERM[None, :], BF16),
        bd_nat=jnp.asarray(_HEAD_NAT[:, None] == _HEAD_NAT[None, :], BF16),
        rmask=jnp.asarray(_HEAD_PERM[:, None] == _HEAD_NAT[None, :], F32),
        smask=jnp.asarray((_LANE[:, None] // SSM_STATE) == (_HEAD_NAT[None, :] // (GROUP_HEADS // SSM_GROUPS)), F32),
        qmask_bf=jnp.asarray(np.concatenate([_head_mask(_HEAD_PERM)] * 2), BF16),
        vmask_bf=jnp.asarray(np.concatenate([_head_mask(_HEAD_NAT)] * 2), BF16),
        ltri=jnp.asarray(np.tril(np.ones((CHUNK, CHUNK), np.float32)), BF16),
    )


def _build_w_cat_t(w_in):
    wt = jnp.transpose(w_in, (0, 2, 1))

    def nat(c0):
        return [wt[:, c0:c0 + 256]]

    def perm(c0):
        return [wt[:, c0 + h * HEAD_DIM + half * 32:c0 + h * HEAD_DIM + half * 32 + 32]
                for half in range(2) for h in range(GROUP_HEADS)]

    dt = [jnp.repeat(wt[:, _BDT:_BDT + GROUP_HEADS], HEAD_DIM, axis=1)]
    parts = (perm(_A0) + perm(_A0 + 256) + nat(_A0 + 512)
             + nat(_BZ) + nat(_BX) + nat(_BB) + nat(_BC) + dt
             + perm(_C0) + perm(_C0 + 256) + nat(_C0 + 512) + nat(_C0 + 768)
             + nat(_D0) + nat(_D0 + 256))
    return jnp.concatenate(parts, axis=1).astype(BF16)


def kernel(x, p, norm_mix, w_in, attn_q_norm, attn_k_norm, ssm_conv_w, ssm_conv_b, ssm_dt_bias,
           ssm_a_log, ssm_d, ssm_norm, conv_dw_w, conv_dw_b, conv_ln_g, conv_ln_b, w_out,
           norm_ffn, w_up, w_down, norm_ple, w_ple, w_ple_gate):
    bsz, seq, _ = x.shape
    depth = w_in.shape[0]
    tm = 512
    tb = _tables(seq)
    row = lambda a: a[:, None, :]
    w_cat = _build_w_cat_t(w_in)
    g_mix, g_ffn, g_ple = row(norm_mix), row(norm_ffn), row(norm_ple)
    gq, gk = row(attn_q_norm[:, _PERM % HEAD_DIM]), row(attn_k_norm[:, _PERM % HEAD_DIM])
    cw = jnp.pad(ssm_conv_w, ((0, 0), (0, 8 - SSM_CONV), (0, 0)))
    dw = jnp.pad(conv_dw_w, ((0, 0), (0, 32 - CONF_KERNEL), (0, 0)))
    dtb, alog, dsk = row(ssm_dt_bias[:, _HEAD_NAT]), row(ssm_a_log[:, _HEAD_NAT]), row(ssm_d[:, _HEAD_NAT])
    wout, wup, wdn = w_out.astype(BF16), w_up.astype(BF16), w_down.astype(BF16)
    wple, wgate = w_ple.astype(BF16), w_ple_gate.astype(BF16)
    p_all = p.reshape(depth, bsz * seq, PLE_DIM)
    x2d = x.reshape(bsz * seq, D_MODEL)
    for i in range(depth):
        oa, ob, oc, yd = _inproj(x2d, g_mix, w_cat, tb["cosa"], tb["sina"], tb["cosc"], tb["sinc"],
                                 gq, gk, tb["bd_perm"], dw, row(conv_dw_b), row(conv_ln_g), row(conv_ln_b),
                                 seq, tm, i)
        ya = _attention(oa, tb["qmask_bf"], bsz, seq)
        yb, yc = _seqmix(ob, cw, row(ssm_conv_b), dtb, alog, dsk, row(ssm_norm), tb["ltri"], tb["smask"],
                         oc, tb["dmat"].reshape(GROUP_HEADS * CHUNK, CHUNK), tb["zeta"], tb["xi"], tb["gch"],
                         tb["bd_nat"], tb["rmask"], tb["qmask_bf"], tb["vmask_bf"], bsz, seq, i)
        x2d = _ffn(x2d, ya, yb, yc, yd, p_all, wout, g_ffn, wup, wdn, g_ple, wple, wgate, tm, i)
    return x2d.reshape(bsz, seq, D_MODEL)
```

```python
import functools

import numpy as np
import jax
import jax.numpy as jnp
from jax import lax
from jax.experimental import pallas as pl
from jax.experimental.pallas import tpu as pltpu

F32 = jnp.float32
BF16 = jnp.bfloat16

D_MODEL = 1024
GROUP_WIDTH = 256
GROUP_HEADS = 4
HEAD_DIM = 64
EPS = 1e-6
ATTN_PATTERNS = ((128, 1), (512, 4), (2048, 16))
ATTN_BLOCK = 128
ROPE_THETA = 10000.0
SSM_STATE = 128
SSM_GROUPS = 2
SSM_CONV = 4
CHUNK = 128
CONF_KERNEL = 31
D_FF = 4 * D_MODEL
PLE_DIM = 256
NEG = -1e30
LOG2E = 1.4426950408889634
CONF_HALO = 32
CONF_ROWS = 64

_A0 = 0
_B0 = 3 * GROUP_WIDTH
_BZ, _BX, _BB, _BC = _B0, _B0 + 256, _B0 + 512, _B0 + 768
_BDT = _B0 + 1024
_C0 = _BDT + GROUP_HEADS
_D0 = _C0 + 4 * GROUP_WIDTH
IN_COLS = _D0 + 2 * GROUP_WIDTH

_LANE = np.arange(GROUP_WIDTH)
_PERM = ((_LANE % 128) // 32) * HEAD_DIM + (_LANE // 128) * 32 + (_LANE % 32)
_HEAD_PERM = (_LANE % 128) // 32
_HEAD_NAT = _LANE // HEAD_DIM

N_PROJ = 14 * GROUP_WIDTH


def _sigmoid(x):
    return 1.0 / (1.0 + jnp.exp(-x))


def _silu(x):
    return x * _sigmoid(x)


def _softplus(x):
    return jnp.maximum(x, 0.0) + jnp.log(1.0 + jnp.exp(-jnp.abs(x)))


def _dot(a, b):
    return jnp.dot(a, b, preferred_element_type=F32)


def _dot_nt(a, b):
    return lax.dot_general(a, b, (((1,), (1,)), ((), ())), preferred_element_type=F32)


def _dot_tn(a, b):
    return lax.dot_general(a, b, (((0,), (0,)), ((), ())), preferred_element_type=F32)


def _inproj_kernel(x_ref, g_ref, w_ref, cosa_ref, sina_ref, cosc_ref, sinc_ref,
                   gq_ref, gk_ref, bd_ref, dw_ref, db_ref, lg_ref, lb_ref,
                   oa_ref, ob_ref, oc_ref, yd_ref, hb0_ref, hb1_ref, ext_ref, *, nseq):
    step = pl.program_id(0)
    tm = oa_ref.shape[0]

    def norm_into(dst_ref):
        x = x_ref[...]
        h = x * lax.rsqrt(jnp.mean(x * x, axis=-1, keepdims=True) + EPS) * g_ref[...]
        dst_ref[...] = h.astype(BF16)

    @pl.when(step == 0)
    def _():
        norm_into(hb0_ref)
        ext_ref[CONF_HALO + tm:CONF_HALO + tm + 8, :] = jnp.zeros((8, 256), F32)

    @pl.when(step % nseq == 1)
    def _():
        ext_ref[0:CONF_HALO, :] = jnp.zeros((CONF_HALO, 256), F32)

    rest = (w_ref, cosa_ref, sina_ref, cosc_ref, sinc_ref, gq_ref, gk_ref, bd_ref, dw_ref, db_ref, lg_ref,
            lb_ref, oa_ref, ob_ref, oc_ref, yd_ref, ext_ref)

    @pl.when((step > 0) & (step % 2 == 0))
    def _():
        norm_into(hb0_ref)
        _inproj_body(hb1_ref, *rest)

    @pl.when(step % 2 == 1)
    def _():
        norm_into(hb1_ref)
        _inproj_body(hb0_ref, *rest)


def _inproj_body(hb_ref, w_ref, cosa_ref, sina_ref, cosc_ref, sinc_ref, gq_ref, gk_ref, bd_ref,
                 dw_ref, db_ref, lg_ref, lb_ref, oa_ref, ob_ref, oc_ref, yd_ref, ext_ref):
    tm = oa_ref.shape[0]

    def mm(j):
        return _dot_nt(hb_ref[...], w_ref[j * 256:(j + 1) * 256, :])

    def rot(t, cos, sin):
        t1, t2 = t[:, :128], t[:, 128:]
        return jnp.concatenate([t1 * cos - t2 * sin, t2 * cos + t1 * sin], axis=1)

    def headnorm(t, gain):
        ss = _dot((t * t).astype(BF16), bd_ref[...])
        return t * lax.rsqrt(ss * (1.0 / HEAD_DIM) + EPS) * gain

    cosa, sina = cosa_ref[...], sina_ref[...]
    cosc, sinc = cosc_ref[...], sinc_ref[...]
    scale = HEAD_DIM ** -0.5
    ext_ref[CONF_HALO:CONF_HALO + tm, :] = mm(12) * _sigmoid(mm(13))
    conv_chunks = iter(range(tm // CONF_ROWS))

    def conv_some(n):
        for _ in range(n):
            c = next(conv_chunks, None)
            if c is not None:
                _conformer_rows(ext_ref, dw_ref, db_ref, lg_ref, lb_ref, yd_ref, c)

    oa_ref[:, 0:256] = mm(0)
    conv_some(1)
    oa_ref[:, 256:512] = mm(1)
    conv_some(1)
    oa_ref[:, 512:768] = mm(2)
    conv_some(1)
    for j in range(5):
        ob_ref[:, j * 256:(j + 1) * 256] = mm(3 + j)
        conv_some(1)
    oa_ref[:, 0:256] = rot(headnorm(oa_ref[:, 0:256], gq_ref[...]), cosa, sina) * (scale * LOG2E)
    oa_ref[:, 256:512] = rot(headnorm(oa_ref[:, 256:512], gk_ref[...]), cosa, sina)
    oc_ref[:, 0:256] = rot(mm(8), cosc, sinc)
    oc_ref[:, 256:512] = rot(mm(9), cosc, sinc) * scale
    oc_ref[:, 512:768] = mm(10)
    oc_ref[:, 768:1024] = mm(11)
    conv_some(tm // CONF_ROWS)
    ext_ref[0:CONF_HALO, :] = ext_ref[tm:tm + CONF_HALO, :]


def _conformer_rows(ext_ref, w_ref, b_ref, lg_ref, lb_ref, o_ref, c):
    nwin = CONF_ROWS + CONF_HALO + 8
    r0 = c * CONF_ROWS
    win = ext_ref[pl.ds(r0, nwin), :]
    acc = jnp.zeros((CONF_ROWS, 256), F32) + b_ref[...]
    for off in range(8):
        shifted = win if off == 0 else pltpu.roll(win, nwin - off, axis=0)
        for sh in range(CONF_HALO - CONF_KERNEL + 1, CONF_HALO + 1):
            if sh % 8 == off:
                k = sh - (CONF_HALO - CONF_KERNEL + 1)
                acc = acc + w_ref[k:k + 1, :] * shifted[sh - off:sh - off + CONF_ROWS, :]
    mu = jnp.mean(acc, axis=-1, keepdims=True)
    xc = acc - mu
    var = jnp.mean(xc * xc, axis=-1, keepdims=True)
    o_ref[pl.ds(r0, CONF_ROWS), :] = _silu(xc * lax.rsqrt(var + EPS) * lg_ref[...] + lb_ref[...])


def _inproj(x2d, g, w_cat, cosa, sina, cosc, sinc, gq, gk, bd, dw, db, lg, lb, seq, tm, layer):
    t = x2d.shape[0]
    nseq = seq // tm
    const = lambda i: (0, 0)
    lay = lambda i: (layer, 0, 0)
    nt = t // tm
    src = lambda i: (jnp.minimum(i, nt - 1), 0)
    tok = lambda i: (jnp.maximum(i - 1, 0), 0)
    pos = lambda i: (jnp.maximum(i - 1, 0) % nseq, 0)
    return pl.pallas_call(
        functools.partial(_inproj_kernel, nseq=nseq),
        grid=(nt + 1,),
        in_specs=[
            pl.BlockSpec((tm, D_MODEL), src),
            pl.BlockSpec((None, 1, D_MODEL), lay),
            pl.BlockSpec((None, N_PROJ, D_MODEL), lay),
            pl.BlockSpec((tm, 128), pos), pl.BlockSpec((tm, 128), pos),
            pl.BlockSpec((tm, 128), pos), pl.BlockSpec((tm, 128), pos),
            pl.BlockSpec((None, 1, 256), lay), pl.BlockSpec((None, 1, 256), lay),
            pl.BlockSpec((256, 256), const),
            pl.BlockSpec((None, 32, 256), lay), pl.BlockSpec((None, 1, 256), lay),
            pl.BlockSpec((None, 1, 256), lay), pl.BlockSpec((None, 1, 256), lay),
        ],
        out_specs=[
            pl.BlockSpec((tm, 768), tok), pl.BlockSpec((tm, 1280), tok),
            pl.BlockSpec((tm, 1024), tok), pl.BlockSpec((tm, 256), tok),
        ],
        out_shape=[
            jax.ShapeDtypeStruct((t, 768), F32), jax.ShapeDtypeStruct((t, 1280), F32),
            jax.ShapeDtypeStruct((t, 1024), F32), jax.ShapeDtypeStruct((t, 256), F32),
        ],
        scratch_shapes=[pltpu.VMEM((tm, D_MODEL), BF16), pltpu.VMEM((tm, D_MODEL), BF16),
                        pltpu.VMEM((tm + CONF_HALO + 8, 256), F32)],
        compiler_params=pltpu.CompilerParams(
            dimension_semantics=("arbitrary",), vmem_limit_bytes=56 * 1024 * 1024),
        name="inproj",
    )(x2d, g, w_cat, cosa, sina, cosc, sinc, gq, gk, bd, dw, db, lg, lb)


ATTN_TILE = 2048
ATTN_GROUP = 4


def _attn_kernel(*refs):
    q_refs, kp_refs, kc_refs, vp_refs, vc_refs = (refs[2 * i:2 * i + 2] for i in range(5))
    qm_ref, o_ref = refs[10], refs[11]
    m_refs, l_refs, u_refs = refs[12:16], refs[16:20], refs[20:22]
    tile = pl.program_id(1)

    ii = lax.broadcasted_iota(jnp.int32, (128, 256), 0)
    jj = lax.broadcasted_iota(jnp.int32, (128, 256), 1)
    band_bias = jnp.where((jj >= ii) & (jj <= ii + ATTN_BLOCK), 0.0, NEG).astype(F32)
    noprev_bias = jnp.where(jj < ATTN_BLOCK, NEG, 0.0).astype(F32)
    first_bias = band_bias + jnp.where(tile == 0, 1.0, 0.0).astype(F32) * noprev_bias
    lo64 = lax.broadcasted_iota(jnp.int32, (128, 128), 1) < HEAD_DIM

    def rows(start, dil):
        if dil == 1:
            return pl.ds(pl.multiple_of(start, 128), 128)
        return pl.ds(start, 128, stride=dil)

    def wide(pair, rr):
        return jnp.concatenate([pair[0][rr, :], pair[1][rr, :]], axis=1)

    def blocks(descs, dil, init):
        st = []
        for qs, klo_refs, vlo_refs, lo_s, hi_s, bias in descs:
            qr, lo, hi = rows(qs, dil), rows(lo_s, dil), rows(hi_s, dil)
            q = wide(q_refs, qr)
            k = jnp.concatenate([wide(klo_refs, lo), wide(kc_refs, hi)], axis=0).astype(BF16)
            v = [jnp.concatenate([vlo_refs[c][lo, :], vc_refs[c][hi, :]], axis=0).astype(BF16) for c in range(2)]
            qb = q.astype(BF16)
            q4 = jnp.concatenate([qb * qm_ref[h:h + 1, :] for h in range(GROUP_HEADS)], axis=0)
            st.append(dict(qr=qr, v=v, bias=bias, s_all=_dot_nt(q4, k)))
        if not init:
            for d in st:
                d["m_old"] = [m_refs[h][d["qr"], :] for h in range(GROUP_HEADS)]
                d["l_old"] = [l_refs[h][d["qr"], :] for h in range(GROUP_HEADS)]
        for d in st:
            d["m_new"], d["l_new"], d["alpha"], ps = [], [], [], []
            for h in range(GROUP_HEADS):
                s = d["s_all"][h * 128:(h + 1) * 128, :] + d["bias"]
                rmax = jnp.max(s, axis=-1, keepdims=True)
                m_new = jnp.broadcast_to(rmax, (128, 128)) if init else jnp.maximum(d["m_old"][h], rmax)
                p = jnp.exp2(s - jnp.concatenate([m_new, m_new], axis=1))
                rsum = jnp.sum(p, axis=-1, keepdims=True)
                if init:
                    d["l_new"].append(jnp.broadcast_to(rsum, (128, 128)))
                else:
                    alpha = jnp.exp2(d["m_old"][h] - m_new)
                    d["l_new"].append(alpha * d["l_old"][h] + rsum)
                    d["alpha"].append(alpha)
                d["m_new"].append(m_new)
                ps.append(p.astype(BF16))
            d["pv"] = []
            for c in range(2):
                pv = _dot(jnp.concatenate(ps[2 * c:2 * c + 2], axis=0), d["v"][c])
                d["pv"].append(jnp.where(lo64, pv[0:128, :], pv[128:256, :]))
        if not init:
            for d in st:
                d["u_old"] = [u_refs[c][d["qr"], :] for c in range(2)]
        for d in st:
            for h in range(GROUP_HEADS):
                m_refs[h][d["qr"], :] = d["m_new"][h]
                l_refs[h][d["qr"], :] = d["l_new"][h]
            for c in range(2):
                if init:
                    u_refs[c][d["qr"], :] = d["pv"][c]
                else:
                    a_c = jnp.where(lo64, d["alpha"][2 * c], d["alpha"][2 * c + 1])
                    u_refs[c][d["qr"], :] = d["u_old"][c] * a_c + d["pv"][c]

    def first_desc(r, span):
        return (r, kp_refs, vp_refs, ATTN_TILE - span + r, r, first_bias)

    def later_desc(r, m, span):
        base = r + (m - 1) * span
        return (base + span, kc_refs, vc_refs, base, base + span, band_bias)

    for pi, (_, dil) in enumerate(reversed(ATTN_PATTERNS)):
        span = ATTN_BLOCK * dil
        nblk = ATTN_TILE // span
        init = pi == 0
        grp = ATTN_GROUP
        if nblk == 1:
            def class_group(i, carry, dil=dil, span=span, init=init):
                blocks([first_desc(grp * i + j, span) for j in range(grp)], dil, init)
                return carry

            lax.fori_loop(0, dil // grp, class_group, 0)
        else:
            def per_class(r, carry, dil=dil, span=span, nblk=nblk, init=init):
                blocks([first_desc(r, span)] + [later_desc(r, j, span) for j in range(1, grp)], dil, init)

                def later_group(i, c):
                    blocks([later_desc(r, grp * (i + 1) + j, span) for j in range(grp)], dil, init)
                    return c

                if nblk > grp:
                    lax.fori_loop(0, nblk // grp - 1, later_group, 0)
                return carry

            lax.fori_loop(0, dil, per_class, 0)

    lo64w = lax.broadcasted_iota(jnp.int32, (256, 128), 1) < HEAD_DIM

    def finish(c, carry):
        rr = pl.ds(pl.multiple_of(c * 256, 256), 256)
        for hlf in range(2):
            den = jnp.where(lo64w, l_refs[2 * hlf][rr, :], l_refs[2 * hlf + 1][rr, :])
            o_ref[rr, hlf * 128:(hlf + 1) * 128] = u_refs[hlf][rr, :] / den
        return carry

    lax.fori_loop(0, ATTN_TILE // 256, finish, 0)


def _attention(qkv, qmask, bsz, seq):
    nt = seq // ATTN_TILE
    cur = lambda c: (lambda b, n: (b * nt + n, c))
    prev = lambda c: (lambda b, n: (b * nt + jnp.maximum(n - 1, 0), c))
    blk = (ATTN_TILE, 128)
    col_maps = [cur(0), cur(1),
                prev(2), prev(3), cur(2), cur(3),
                prev(4), prev(5), cur(4), cur(5)]
    return pl.pallas_call(
        _attn_kernel,
        grid=(bsz, nt),
        in_specs=[pl.BlockSpec(blk, m) for m in col_maps] + [pl.BlockSpec((16, 256), lambda b, n: (0, 0))],
        out_specs=pl.BlockSpec((ATTN_TILE, 256), lambda b, n: (b * nt + n, 0)),
        out_shape=jax.ShapeDtypeStruct((bsz * seq, GROUP_WIDTH), F32),
        scratch_shapes=[pltpu.VMEM((ATTN_TILE, 128), F32)] * 10,
        compiler_params=pltpu.CompilerParams(
            dimension_semantics=("arbitrary", "arbitrary"), vmem_limit_bytes=56 * 1024 * 1024),
        name="attn",
    )(*([qkv] * 10), qmask)


SEQ_TILE = 1024
SSM_ROWS = 64


def _ssd_tile(u_ref, cw_ref, cb_ref, dtb_ref, a_ref, dsk_ref, ng_ref, ltri_ref, smask_ref,
              vm_ref, o_ref, ext_ref, act_ref, hs_ref):
    ts = SEQ_TILE

    ext_ref[8:8 + ts, :] = u_ref[:, 256:1024]
    for c in range(ts // SSM_ROWS):
        for gc in range(3):
            cols = slice(gc * 256, (gc + 1) * 256)
            win = ext_ref[pl.ds(c * SSM_ROWS, SSM_ROWS + 8), cols]
            acc = cb_ref[:, cols] + cw_ref[SSM_CONV - 1:SSM_CONV, cols] * win[8:8 + SSM_ROWS, :]
            for sh in range(1, SSM_CONV):
                k = SSM_CONV - 1 - sh
                acc = acc + cw_ref[k:k + 1, cols] * pltpu.roll(win, sh, axis=0)[8:8 + SSM_ROWS, :]
            act_ref[pl.ds(c * SSM_ROWS, SSM_ROWS), cols] = _silu(acc)
    ext_ref[0:8, :] = ext_ref[ts:ts + 8, :]

    ii = lax.broadcasted_iota(jnp.int32, (CHUNK, CHUNK), 0)
    jj = lax.broadcasted_iota(jnp.int32, (CHUNK, CHUNK), 1)
    causal = jj <= ii
    ltri = ltri_ref[...]

    hs = hs_ref[...]
    for c in range(ts // CHUNK):
        rr = pl.ds(c * CHUNK, CHUNK)
        z = u_ref[rr, 0:256]
        xs = act_ref[rr, 0:256]
        bm = act_ref[rr, 256:512]
        cm = act_ref[rr, 512:768]
        dt = _softplus(u_ref[rr, 1024:1280] + dtb_ref[...])
        da = dt * (-jnp.exp(a_ref[...]))
        da_hi = da.astype(BF16)
        rem = da - da_hi.astype(F32)
        da_mid = rem.astype(BF16)
        da_lo = (rem - da_mid.astype(F32)).astype(BF16)
        acum = _dot(ltri, da_hi) + _dot(ltri, da_mid) + _dot(ltri, da_lo)
        total = acum[CHUNK - 1:CHUNK, :]
        bmb = bm.astype(BF16)
        cmb = cm.astype(BF16)
        xdt = xs * dt
        acum_t = (jnp.transpose(acum[:, 0:128]), jnp.transpose(acum[:, 128:256]))
        cbs = [_dot_nt(cmb[:, g * 128:(g + 1) * 128], bmb[:, g * 128:(g + 1) * 128]) for g in range(SSM_GROUPS)]
        mhs = []
        for h in range(GROUP_HEADS):
            col = acum[:, h * HEAD_DIM:h * HEAD_DIM + 1]
            rsel = (h % 2) * HEAD_DIM
            row = acum_t[h // 2][rsel:rsel + 1, :]
            lmat = jnp.exp(jnp.where(causal, col - row, NEG))
            mhs.append((cbs[h // 2] * lmat).astype(BF16))
        xdtb = xdt.astype(BF16)
        xdt4 = jnp.concatenate([xdtb * vm_ref[h:h + 1, :] for h in range(GROUP_HEADS)], axis=0)
        y = _dot(cmb, hs.astype(BF16)) * jnp.exp(acum) + _dot(jnp.concatenate(mhs, axis=1), xdt4)
        xw = (xs * (jnp.exp(total - acum) * dt)).astype(BF16)
        hs = jnp.exp(total) * hs + _dot_tn(bmb, xw) * smask_ref[...]
        y = (y + dsk_ref[...] * xs) * _silu(z)
        halves = []
        for g in range(SSM_GROUPS):
            yg = y[:, g * 128:(g + 1) * 128]
            halves.append(yg * lax.rsqrt(jnp.mean(yg * yg, axis=-1, keepdims=True) + EPS))
        o_ref[rr, :] = jnp.concatenate(halves, axis=1) * ng_ref[...]
    hs_ref[...] = hs


def _ret_tile(u_ref, dmat_ref, zeta_ref, xi_ref, gch_ref, bd_ref, rmask_ref, qm_ref, vm_ref,
              o_ref, r_ref):
    r = r_ref[...]
    for c in range(SEQ_TILE // CHUNK):
        rr = pl.ds(c * CHUNK, CHUNK)
        q = u_ref[rr, 0:256]
        k = u_ref[rr, 256:512]
        v = u_ref[rr, 512:768]
        kb = k.astype(BF16)
        qb = q.astype(BF16)
        q4 = jnp.concatenate([qb * qm_ref[h:h + 1, :] for h in range(GROUP_HEADS)], axis=0)
        s4 = _dot_nt(q4, kb) * dmat_ref[...]
        s_cat = jnp.concatenate([s4[h * CHUNK:(h + 1) * CHUNK, :] for h in range(GROUP_HEADS)], axis=1)
        vb = v.astype(BF16)
        v4 = jnp.concatenate([vb * vm_ref[h:h + 1, :] for h in range(GROUP_HEADS)], axis=0)
        y = _dot(qb, r.astype(BF16)) * xi_ref[...] + _dot(s_cat.astype(BF16), v4)
        o_ref[rr, :] = y
        kz = (k * zeta_ref[...]).astype(BF16)
        r = gch_ref[...] * r + _dot_tn(kz, vb) * rmask_ref[...]
    r_ref[...] = r
    for hlf in range(2):
        rr = pl.ds(hlf * (SEQ_TILE // 2), SEQ_TILE // 2)
        y = o_ref[rr, :]
        ss = _dot((y * y).astype(BF16), bd_ref[...])
        o_ref[rr, :] = y * lax.rsqrt(ss * (1.0 / HEAD_DIM) + EPS) * _silu(u_ref[rr, 768:1024])


def _seqmix_kernel(*refs):
    ssd_in, ret_in = refs[0:10], refs[10:19]
    yb_ref, yc_ref, ext_ref, act_ref, hs_ref, r_ref = refs[19:25]

    @pl.when(pl.program_id(1) == 0)
    def _():
        ext_ref[0:8, :] = jnp.zeros((8, 768), F32)
        hs_ref[...] = jnp.zeros(hs_ref.shape, F32)
        r_ref[...] = jnp.zeros(r_ref.shape, F32)

    _ssd_tile(*ssd_in, yb_ref, ext_ref, act_ref, hs_ref)
    _ret_tile(*ret_in, yc_ref, r_ref)


def _seqmix(ub, cw, cb, dtb, a_exp, dsk, ng, ltri, smask, uc, dmat, zeta, xi, gch, bdn, rmask, qmask, vmask,
            bsz, seq, layer):
    ts = SEQ_TILE
    nt = seq // ts
    const = lambda b, n: (0, 0)
    lay = lambda b, n: (layer, 0, 0)
    tok = lambda b, n: (b * nt + n, 0)
    return pl.pallas_call(
        _seqmix_kernel,
        grid=(bsz, nt),
        in_specs=[
            pl.BlockSpec((ts, 1280), tok),
            pl.BlockSpec((None, 8, 768), lay), pl.BlockSpec((None, 1, 768), lay),
            pl.BlockSpec((None, 1, 256), lay), pl.BlockSpec((None, 1, 256), lay),
            pl.BlockSpec((None, 1, 256), lay), pl.BlockSpec((None, 1, 256), lay),
            pl.BlockSpec((CHUNK, CHUNK), const), pl.BlockSpec((256, 256), const),
            pl.BlockSpec((16, 256), const),
            pl.BlockSpec((ts, 1024), tok),
            pl.BlockSpec((GROUP_HEADS * CHUNK, CHUNK), const),
            pl.BlockSpec((CHUNK, 256), const), pl.BlockSpec((CHUNK, 256), const),
            pl.BlockSpec((1, 256), const), pl.BlockSpec((256, 256), const),
            pl.BlockSpec((256, 256), const),
            pl.BlockSpec((16, 256), const), pl.BlockSpec((16, 256), const),
        ],
        out_specs=[pl.BlockSpec((ts, 256), tok), pl.BlockSpec((ts, 256), tok)],
        out_shape=[jax.ShapeDtypeStruct((bsz * seq, GROUP_WIDTH), F32)] * 2,
        scratch_shapes=[pltpu.VMEM((ts + 8, 768), F32), pltpu.VMEM((ts, 768), F32), pltpu.VMEM((256, 256), F32),
                        pltpu.VMEM((256, 256), F32)],
        compiler_params=pltpu.CompilerParams(
            dimension_semantics=("arbitrary", "arbitrary"), vmem_limit_bytes=56 * 1024 * 1024),
        name="seqmix",
    )(ub, cw, cb, dtb, a_exp, dsk, ng, ltri, smask, vmask, uc, dmat, zeta, xi, gch, bdn, rmask, qmask, vmask)


FF_CHUNK = 1024


def _rms_rows(x, g):
    return x * lax.rsqrt(jnp.mean(x * x, axis=-1, keepdims=True) + EPS) * g


def _ffn_kernel(x_ref, ya_ref, yb_ref, yc_ref, yd_ref, p_ref, wout_ref, gf_ref, wup_ref, wdn_ref,
                gp_ref, wple_ref, wgate_ref, o_ref, hb_ref):
    tm = x_ref.shape[0]
    halves = [pl.ds(0, tm // 2), pl.ds(tm // 2, tm // 2)]
    x1 = []
    for rr in halves:
        acc = x_ref[rr, :]
        for m, y_ref in enumerate((ya_ref, yb_ref, yc_ref, yd_ref)):
            acc = acc + _dot(y_ref[rr, :].astype(BF16), wout_ref[m * 256:(m + 1) * 256, :])
        hb_ref[rr, :] = _rms_rows(acc, gf_ref[...]).astype(BF16)
        x1.append(acc)

    def down(rr, c):
        up = jnp.maximum(_dot(hb_ref[rr, :], wup_ref[:, c * FF_CHUNK:(c + 1) * FF_CHUNK]), 0.0)
        return _dot((up * up).astype(BF16), wdn_ref[c * FF_CHUNK:(c + 1) * FF_CHUNK, :])

    ffs = [down(rr, 0) for rr in halves]
    for c in range(1, D_FF // FF_CHUNK):
        ffs = [ff + down(rr, c) for ff, rr in zip(ffs, halves)]
    for rr, acc, ff in zip(halves, x1, ffs):
        x2 = acc + ff
        gate = _sigmoid(_dot(_rms_rows(x2, gp_ref[...]).astype(BF16), wgate_ref[...]))
        o_ref[rr, :] = x2 + _dot(p_ref[rr, :].astype(BF16), wple_ref[...]) * gate


def _ffn(x2d, ya, yb, yc, yd, p_all, wout, gf, wup, wdn, gp, wple, wgate, tm, layer):
    t = x2d.shape[0]
    tok = lambda i: (i, 0)
    lay = lambda i: (layer, 0, 0)
    once = dict(pipeline_mode=pl.Buffered(1))
    return pl.pallas_call(
        _ffn_kernel,
        grid=(t // tm,),
        in_specs=[
            pl.BlockSpec((tm, D_MODEL), tok),
            pl.BlockSpec((tm, 256), tok), pl.BlockSpec((tm, 256), tok),
            pl.BlockSpec((tm, 256), tok), pl.BlockSpec((tm, 256), tok),
            pl.BlockSpec((None, tm, PLE_DIM), lambda i: (layer, i, 0)),
            pl.BlockSpec((None, D_MODEL, D_MODEL), lay, **once),
            pl.BlockSpec((None, 1, D_MODEL), lay),
            pl.BlockSpec((None, D_MODEL, D_FF), lay, **once),
            pl.BlockSpec((None, D_FF, D_MODEL), lay, **once),
            pl.BlockSpec((None, 1, D_MODEL), lay),
            pl.BlockSpec((None, PLE_DIM, D_MODEL), lay, **once),
            pl.BlockSpec((None, D_MODEL, D_MODEL), lay, **once),
        ],
        out_specs=pl.BlockSpec((tm, D_MODEL), tok),
        out_shape=jax.ShapeDtypeStruct((t, D_MODEL), F32),
        scratch_shapes=[pltpu.VMEM((tm, D_MODEL), BF16)],
        compiler_params=pltpu.CompilerParams(
            dimension_semantics=("arbitrary",), vmem_limit_bytes=56 * 1024 * 1024),
        name="ffn",
    )(x2d, ya, yb, yc, yd, p_all, wout, gf, wup, wdn, gp, wple, wgate)


def _head_mask(head_of_lane):
    m = np.zeros((8, GROUP_WIDTH), np.float32)
    for h in range(GROUP_HEADS):
        m[h] = head_of_lane == h
    return m


def _tables(seq):
    pos = jnp.arange(seq, dtype=F32)
    ang_a = ROPE_THETA ** (-jnp.arange(0, HEAD_DIM, 2, dtype=F32) / HEAD_DIM)
    ang_c = 1.0 / (10000.0 ** jnp.linspace(0.0, 1.0, HEAD_DIM // 2, dtype=F32))

    def cs(ang):
        a = pos[:, None] * ang[None, :]
        return jnp.tile(jnp.cos(a), (1, 4)), jnp.tile(jnp.sin(a), (1, 4))

    cosa, sina = cs(ang_a)
    cosc, sinc = cs(ang_c)
    log_g = jnp.log(1.0 - 2.0 ** (-5.0 - jnp.arange(GROUP_HEADS, dtype=F32)))
    idx = jnp.arange(CHUNK, dtype=F32)
    dist = idx[:, None] - idx[None, :]
    dmat = jnp.where((dist >= 0)[None], jnp.exp(jnp.maximum(dist, 0.0)[None] * log_g[:, None, None]), 0.0)
    zeta = jnp.exp((CHUNK - 1 - idx)[:, None] * log_g)[:, _HEAD_PERM]
    xi = jnp.exp((idx + 1.0)[:, None] * log_g)[:, _HEAD_NAT]
    gch = jnp.exp(CHUNK * log_g)[_HEAD_NAT][None, :]
    return dict(
        cosa=cosa, sina=sina, cosc=cosc, sinc=sinc, dmat=dmat, zeta=zeta, xi=xi, gch=gch,
        bd_perm=jnp.asarray(_HEAD_PERM[:, None] == _HEAD_PERM[None, :], BF16),
        bd_nat=jnp.asarray(_HEAD_NAT[:, None] == _HEAD_NAT[None, :], BF16),
        rmask=jnp.asarray(_HEAD_PERM[:, None] == _HEAD_NAT[None, :], F32),
        smask=jnp.asarray((_LANE[:, None] // SSM_STATE) == (_HEAD_NAT[None, :] // (GROUP_HEADS // SSM_GROUPS)), F32),
        qmask_bf=jnp.asarray(np.concatenate([_head_mask(_HEAD_PERM)] * 2), BF16),
        vmask_bf=jnp.asarray(np.concatenate([_head_mask(_HEAD_NAT)] * 2), BF16),
        ltri=jnp.asarray(np.tril(np.ones((CHUNK, CHUNK), np.float32)), BF16),
    )


def _build_w_cat_t(w_in):
    wt = jnp.transpose(w_in, (0, 2, 1))

    depth = wt.shape[0]

    def perm(c0):
        blk = wt[:, c0:c0 + 256].reshape(depth, GROUP_HEADS, 2, 32, D_MODEL)
        return blk.transpose(0, 2, 1, 3, 4).reshape(depth, 256, D_MODEL)

    dt = jnp.repeat(wt[:, _BDT:_BDT + GROUP_HEADS], HEAD_DIM, axis=1)
    parts = [perm(_A0), perm(_A0 + 256), wt[:, _A0 + 512:_BDT], dt,
             perm(_C0), perm(_C0 + 256), wt[:, _C0 + 512:IN_COLS]]
    return jnp.concatenate(parts, axis=1).astype(BF16)


def kernel(x, p, norm_mix, w_in, attn_q_norm, attn_k_norm, ssm_conv_w, ssm_conv_b, ssm_dt_bias,
           ssm_a_log, ssm_d, ssm_norm, conv_dw_w, conv_dw_b, conv_ln_g, conv_ln_b, w_out,
           norm_ffn, w_up, w_down, norm_ple, w_ple, w_ple_gate):
    bsz, seq, _ = x.shape
    depth = w_in.shape[0]
    tm = 512
    tb = _tables(seq)
    row = lambda a: a[:, None, :]
    w_cat = _build_w_cat_t(w_in)
    g_mix, g_ffn, g_ple = row(norm_mix), row(norm_ffn), row(norm_ple)
    gq, gk = row(attn_q_norm[:, _PERM % HEAD_DIM]), row(attn_k_norm[:, _PERM % HEAD_DIM])
    cw = jnp.pad(ssm_conv_w, ((0, 0), (0, 8 - SSM_CONV), (0, 0)))
    dw = jnp.pad(conv_dw_w, ((0, 0), (0, 32 - CONF_KERNEL), (0, 0)))
    dtb, alog, dsk = row(ssm_dt_bias[:, _HEAD_NAT]), row(ssm_a_log[:, _HEAD_NAT]), row(ssm_d[:, _HEAD_NAT])
    wout, wup, wdn = w_out.astype(BF16), w_up.astype(BF16), w_down.astype(BF16)
    wple, wgate = w_ple.astype(BF16), w_ple_gate.astype(BF16)
    p_all = p.reshape(depth, bsz * seq, PLE_DIM)
    x2d = x.reshape(bsz * seq, D_MODEL)
    for i in range(depth):
        oa, ob, oc, yd = _inproj(x2d, g_mix, w_cat, tb["cosa"], tb["sina"], tb["cosc"], tb["sinc"],
                                 gq, gk, tb["bd_perm"], dw, row(conv_dw_b), row(conv_ln_g), row(conv_ln_b),
                                 seq, tm, i)
        ya = _attention(oa, tb["qmask_bf"], bsz, seq)
        yb, yc = _seqmix(ob, cw, row(ssm_conv_b), dtb, alog, dsk, row(ssm_norm), tb["ltri"], tb["smask"],
                         oc, tb["dmat"].reshape(GROUP_HEADS * CHUNK, CHUNK), tb["zeta"], tb["xi"], tb["gch"],
                         tb["bd_nat"], tb["rmask"], tb["qmask_bf"], tb["vmask_bf"], bsz, seq, i)
        x2d = _ffn(x2d, ya, yb, yc, yd, p_all, wout, g_ffn, wup, wdn, g_ple, wple, wgate, tm, i)
    return x2d.reshape(bsz, seq, D_MODEL)
```

```python
import functools

import numpy as np
import jax
import jax.numpy as jnp
from jax import lax
from jax.experimental import pallas as pl
from jax.experimental.pallas import tpu as pltpu

F32 = jnp.float32
BF16 = jnp.bfloat16

D_MODEL = 1024
GROUP_WIDTH = 256
GROUP_HEADS = 4
HEAD_DIM = 64
EPS = 1e-6
ATTN_PATTERNS = ((128, 1), (512, 4), (2048, 16))
ATTN_BLOCK = 128
ROPE_THETA = 10000.0
SSM_STATE = 128
SSM_GROUPS = 2
SSM_CONV = 4
CHUNK = 128
CONF_KERNEL = 31
D_FF = 4 * D_MODEL
DEPTH = 2
PLE_DIM = 256
NEG = -1e30
LOG2E = 1.4426950408889634
CONF_HALO = 32
CONF_ROWS = 64

_A0 = 0
_B0 = 3 * GROUP_WIDTH
_BZ, _BX, _BB, _BC = _B0, _B0 + 256, _B0 + 512, _B0 + 768
_BDT = _B0 + 1024
_C0 = _BDT + GROUP_HEADS
_D0 = _C0 + 4 * GROUP_WIDTH
IN_COLS = _D0 + 2 * GROUP_WIDTH

_LANE = np.arange(GROUP_WIDTH)
_PERM = ((_LANE % 128) // 32) * HEAD_DIM + (_LANE // 128) * 32 + (_LANE % 32)
_HEAD_PERM = (_LANE % 128) // 32
_HEAD_NAT = _LANE // HEAD_DIM

N_PROJ = 14 * GROUP_WIDTH


def _sigmoid(x):
    return 1.0 / (1.0 + jnp.exp(-x))


def _silu(x):
    return x * _sigmoid(x)


def _softplus(x):
    return jnp.maximum(x, 0.0) + jnp.log(1.0 + jnp.exp(-jnp.abs(x)))


def _dot(a, b):
    return jnp.dot(a, b, preferred_element_type=F32)


def _dot_nt(a, b):
    return lax.dot_general(a, b, (((1,), (1,)), ((), ())), preferred_element_type=F32)


def _dot_tn(a, b):
    return lax.dot_general(a, b, (((0,), (0,)), ((), ())), preferred_element_type=F32)


def _inproj_kernel(x_ref, g_ref, w_ref, cosa_ref, sina_ref, cosc_ref, sinc_ref,
                   gq_ref, gk_ref, bd_ref, dw_ref, db_ref, lg_ref, lb_ref,
                   oa_ref, ob_ref, oc_ref, yd_ref, hb0_ref, hb1_ref, ext_ref, *, nseq, layer):
    step = pl.program_id(0)
    tm = oa_ref.shape[0]
    g_ref, gq_ref, gk_ref, db_ref, lg_ref, lb_ref = (
        r.at[pl.ds(layer, 1)] for r in (g_ref, gq_ref, gk_ref, db_ref, lg_ref, lb_ref))

    def norm_into(dst_ref):
        x = x_ref[...]
        h = x * lax.rsqrt(jnp.mean(x * x, axis=-1, keepdims=True) + EPS) * g_ref[...]
        dst_ref[...] = h.astype(BF16)

    @pl.when(step == 0)
    def _():
        norm_into(hb0_ref)
        ext_ref[CONF_HALO + tm:CONF_HALO + tm + 8, :] = jnp.zeros((8, 256), F32)

    @pl.when(step % nseq == 1)
    def _():
        ext_ref[0:CONF_HALO, :] = jnp.zeros((CONF_HALO, 256), F32)

    rest = (w_ref, cosa_ref, sina_ref, cosc_ref, sinc_ref, gq_ref, gk_ref, bd_ref, dw_ref, db_ref, lg_ref,
            lb_ref, oa_ref, ob_ref, oc_ref, yd_ref, ext_ref)

    @pl.when((step > 0) & (step % 2 == 0))
    def _():
        norm_into(hb0_ref)
        _inproj_body(hb1_ref, *rest)

    @pl.when(step % 2 == 1)
    def _():
        norm_into(hb1_ref)
        _inproj_body(hb0_ref, *rest)


def _inproj_body(hb_ref, w_ref, cosa_ref, sina_ref, cosc_ref, sinc_ref, gq_ref, gk_ref, bd_ref,
                 dw_ref, db_ref, lg_ref, lb_ref, oa_ref, ob_ref, oc_ref, yd_ref, ext_ref):
    tm = oa_ref.shape[0]

    def mm(j):
        return _dot_nt(hb_ref[...], w_ref[j * 256:(j + 1) * 256, :])

    def rot(t, cos, sin):
        t1, t2 = t[:, :128], t[:, 128:]
        return jnp.concatenate([t1 * cos - t2 * sin, t2 * cos + t1 * sin], axis=1)

    def headnorm(t, gain):
        ss = _dot((t * t).astype(BF16), bd_ref[...])
        return t * lax.rsqrt(ss * (1.0 / HEAD_DIM) + EPS) * gain

    cosa, sina = cosa_ref[...], sina_ref[...]
    cosc, sinc = cosc_ref[...], sinc_ref[...]
    scale = HEAD_DIM ** -0.5
    ext_ref[CONF_HALO:CONF_HALO + tm, :] = mm(12) * _sigmoid(mm(13))
    conv_chunks = iter(range(tm // CONF_ROWS))

    def conv_some(n):
        for _ in range(n):
            c = next(conv_chunks, None)
            if c is not None:
                _conformer_rows(ext_ref, dw_ref, db_ref, lg_ref, lb_ref, yd_ref, c)

    oa_ref[:, 0:256] = mm(0)
    conv_some(1)
    oa_ref[:, 256:512] = mm(1)
    conv_some(1)
    oa_ref[:, 512:768] = mm(2)
    conv_some(1)
    for j in range(5):
        ob_ref[:, j * 256:(j + 1) * 256] = mm(3 + j)
        conv_some(1)
    oa_ref[:, 0:256] = rot(headnorm(oa_ref[:, 0:256], gq_ref[...]), cosa, sina) * (scale * LOG2E)
    oa_ref[:, 256:512] = rot(headnorm(oa_ref[:, 256:512], gk_ref[...]), cosa, sina)
    oc_ref[:, 0:256] = rot(mm(8), cosc, sinc)
    oc_ref[:, 256:512] = rot(mm(9), cosc, sinc) * scale
    oc_ref[:, 512:768] = mm(10)
    oc_ref[:, 768:1024] = mm(11)
    conv_some(tm // CONF_ROWS)
    ext_ref[0:CONF_HALO, :] = ext_ref[tm:tm + CONF_HALO, :]


def _conformer_rows(ext_ref, w_ref, b_ref, lg_ref, lb_ref, o_ref, c):
    nwin = CONF_ROWS + CONF_HALO + 8
    r0 = c * CONF_ROWS
    win = ext_ref[pl.ds(r0, nwin), :]
    acc = jnp.zeros((CONF_ROWS, 256), F32) + b_ref[...]
    for off in range(8):
        shifted = win if off == 0 else pltpu.roll(win, nwin - off, axis=0)
        for sh in range(CONF_HALO - CONF_KERNEL + 1, CONF_HALO + 1):
            if sh % 8 == off:
                k = sh - (CONF_HALO - CONF_KERNEL + 1)
                acc = acc + w_ref[k:k + 1, :] * shifted[sh - off:sh - off + CONF_ROWS, :]
    mu = jnp.mean(acc, axis=-1, keepdims=True)
    xc = acc - mu
    var = jnp.mean(xc * xc, axis=-1, keepdims=True)
    o_ref[pl.ds(r0, CONF_ROWS), :] = _silu(xc * lax.rsqrt(var + EPS) * lg_ref[...] + lb_ref[...])


def _inproj(x2d, g, w_cat, cosa, sina, cosc, sinc, gq, gk, bd, dw, db, lg, lb, seq, tm, layer):
    t = x2d.shape[0]
    nseq = seq // tm
    const = lambda i: (0, 0)
    lay = lambda i: (layer, 0, 0)
    nt = t // tm
    src = lambda i: (jnp.minimum(i, nt - 1), 0)
    tok = lambda i: (jnp.maximum(i - 1, 0), 0)
    pos = lambda i: (jnp.maximum(i - 1, 0) % nseq, 0)
    return pl.pallas_call(
        functools.partial(_inproj_kernel, nseq=nseq, layer=layer),
        grid=(nt + 1,),
        in_specs=[
            pl.BlockSpec((tm, D_MODEL), src),
            pl.BlockSpec((DEPTH, D_MODEL), const),
            pl.BlockSpec((None, N_PROJ, D_MODEL), lay),
            pl.BlockSpec((tm, 128), pos), pl.BlockSpec((tm, 128), pos),
            pl.BlockSpec((tm, 128), pos), pl.BlockSpec((tm, 128), pos),
            pl.BlockSpec((DEPTH, 256), const), pl.BlockSpec((DEPTH, 256), const),
            pl.BlockSpec((256, 256), const),
            pl.BlockSpec((None, 32, 256), lay), pl.BlockSpec((DEPTH, 256), const),
            pl.BlockSpec((DEPTH, 256), const), pl.BlockSpec((DEPTH, 256), const),
        ],
        out_specs=[
            pl.BlockSpec((tm, 768), tok), pl.BlockSpec((tm, 1280), tok),
            pl.BlockSpec((tm, 1024), tok), pl.BlockSpec((tm, 256), tok),
        ],
        out_shape=[
            jax.ShapeDtypeStruct((t, 768), F32), jax.ShapeDtypeStruct((t, 1280), F32),
            jax.ShapeDtypeStruct((t, 1024), F32), jax.ShapeDtypeStruct((t, 256), F32),
        ],
        scratch_shapes=[pltpu.VMEM((tm, D_MODEL), BF16), pltpu.VMEM((tm, D_MODEL), BF16),
                        pltpu.VMEM((tm + CONF_HALO + 8, 256), F32)],
        compiler_params=pltpu.CompilerParams(
            dimension_semantics=("arbitrary",), vmem_limit_bytes=56 * 1024 * 1024),
        name="inproj",
    )(x2d, g, w_cat, cosa, sina, cosc, sinc, gq, gk, bd, dw, db, lg, lb)


ATTN_TILE = 2048
ATTN_GROUP = 4


def _attn_kernel(*refs):
    q_refs, kp_refs, kc_refs, vp_refs, vc_refs = (refs[2 * i:2 * i + 2] for i in range(5))
    qm_ref, o_ref = refs[10], refs[11]
    m_refs, l_refs, u_refs = refs[12:16], refs[16:20], refs[20:22]
    tile = pl.program_id(1)

    ii = lax.broadcasted_iota(jnp.int32, (128, 256), 0)
    jj = lax.broadcasted_iota(jnp.int32, (128, 256), 1)
    band_bias = jnp.where((jj >= ii) & (jj <= ii + ATTN_BLOCK), 0.0, NEG).astype(F32)
    noprev_bias = jnp.where(jj < ATTN_BLOCK, NEG, 0.0).astype(F32)
    first_bias = band_bias + jnp.where(tile == 0, 1.0, 0.0).astype(F32) * noprev_bias
    lo64 = lax.broadcasted_iota(jnp.int32, (128, 128), 1) < HEAD_DIM

    def rows(start, dil):
        if dil == 1:
            return pl.ds(pl.multiple_of(start, 128), 128)
        return pl.ds(start, 128, stride=dil)

    def wide(pair, rr):
        return jnp.concatenate([pair[0][rr, :], pair[1][rr, :]], axis=1)

    def blocks(descs, dil, init):
        st = []
        for qs, klo_refs, vlo_refs, lo_s, hi_s, bias in descs:
            qr, lo, hi = rows(qs, dil), rows(lo_s, dil), rows(hi_s, dil)
            q = wide(q_refs, qr)
            k = jnp.concatenate([wide(klo_refs, lo), wide(kc_refs, hi)], axis=0).astype(BF16)
            v = [jnp.concatenate([vlo_refs[c][lo, :], vc_refs[c][hi, :]], axis=0).astype(BF16) for c in range(2)]
            qb = q.astype(BF16)
            q4 = jnp.concatenate([qb * qm_ref[h:h + 1, :] for h in range(GROUP_HEADS)], axis=0)
            st.append(dict(qr=qr, v=v, bias=bias, s_all=_dot_nt(q4, k)))
        if not init:
            for d in st:
                d["m_old"] = [m_refs[h][d["qr"], :] for h in range(GROUP_HEADS)]
                d["l_old"] = [l_refs[h][d["qr"], :] for h in range(GROUP_HEADS)]
        for d in st:
            d["m_new"], d["l_new"], d["alpha"], ps = [], [], [], []
            for h in range(GROUP_HEADS):
                s = d["s_all"][h * 128:(h + 1) * 128, :] + d["bias"]
                rmax = jnp.max(s, axis=-1, keepdims=True)
                m_new = jnp.broadcast_to(rmax, (128, 128)) if init else jnp.maximum(d["m_old"][h], rmax)
                p = jnp.exp2(s - jnp.concatenate([m_new, m_new], axis=1))
                rsum = jnp.sum(p, axis=-1, keepdims=True)
                if init:
                    d["l_new"].append(jnp.broadcast_to(rsum, (128, 128)))
                else:
                    alpha = jnp.exp2(d["m_old"][h] - m_new)
                    d["l_new"].append(alpha * d["l_old"][h] + rsum)
                    d["alpha"].append(alpha)
                d["m_new"].append(m_new)
                ps.append(p.astype(BF16))
            d["pv"] = []
            for c in range(2):
                pv = _dot(jnp.concatenate(ps[2 * c:2 * c + 2], axis=0), d["v"][c])
                d["pv"].append(jnp.where(lo64, pv[0:128, :], pv[128:256, :]))
        if not init:
            for d in st:
                d["u_old"] = [u_refs[c][d["qr"], :] for c in range(2)]
        for d in st:
            for h in range(GROUP_HEADS):
                m_refs[h][d["qr"], :] = d["m_new"][h]
                l_refs[h][d["qr"], :] = d["l_new"][h]
            for c in range(2):
                if init:
                    u_refs[c][d["qr"], :] = d["pv"][c]
                else:
                    a_c = jnp.where(lo64, d["alpha"][2 * c], d["alpha"][2 * c + 1])
                    u_refs[c][d["qr"], :] = d["u_old"][c] * a_c + d["pv"][c]

    def first_desc(r, span):
        return (r, kp_refs, vp_refs, ATTN_TILE - span + r, r, first_bias)

    def later_desc(r, m, span):
        base = r + (m - 1) * span
        return (base + span, kc_refs, vc_refs, base, base + span, band_bias)

    for pi, (_, dil) in enumerate(reversed(ATTN_PATTERNS)):
        span = ATTN_BLOCK * dil
        nblk = ATTN_TILE // span
        init = pi == 0
        grp = ATTN_GROUP
        if nblk == 1:
            def class_group(i, carry, dil=dil, span=span, init=init):
                blocks([first_desc(grp * i + j, span) for j in range(grp)], dil, init)
                return carry

            lax.fori_loop(0, dil // grp, class_group, 0)
        else:
            def per_class(r, carry, dil=dil, span=span, nblk=nblk, init=init):
                blocks([first_desc(r, span)] + [later_desc(r, j, span) for j in range(1, grp)], dil, init)

                def later_group(i, c):
                    blocks([later_desc(r, grp * (i + 1) + j, span) for j in range(grp)], dil, init)
                    return c

                if nblk > grp:
                    lax.fori_loop(0, nblk // grp - 1, later_group, 0)
                return carry

            lax.fori_loop(0, dil, per_class, 0)

    lo64w = lax.broadcasted_iota(jnp.int32, (256, 128), 1) < HEAD_DIM

    def finish(c, carry):
        rr = pl.ds(pl.multiple_of(c * 256, 256), 256)
        for hlf in range(2):
            den = jnp.where(lo64w, l_refs[2 * hlf][rr, :], l_refs[2 * hlf + 1][rr, :])
            o_ref[rr, hlf * 128:(hlf + 1) * 128] = u_refs[hlf][rr, :] / den
        return carry

    lax.fori_loop(0, ATTN_TILE // 256, finish, 0)


def _attention(qkv, qmask, bsz, seq):
    nt = seq // ATTN_TILE
    cur = lambda c: (lambda b, n: (b * nt + n, c))
    prev = lambda c: (lambda b, n: (b * nt + jnp.maximum(n - 1, 0), c))
    blk = (ATTN_TILE, 128)
    col_maps = [cur(0), cur(1),
                prev(2), prev(3), cur(2), cur(3),
                prev(4), prev(5), cur(4), cur(5)]
    return pl.pallas_call(
        _attn_kernel,
        grid=(bsz, nt),
        in_specs=[pl.BlockSpec(blk, m) for m in col_maps] + [pl.BlockSpec((16, 256), lambda b, n: (0, 0))],
        out_specs=pl.BlockSpec((ATTN_TILE, 256), lambda b, n: (b * nt + n, 0)),
        out_shape=jax.ShapeDtypeStruct((bsz * seq, GROUP_WIDTH), F32),
        scratch_shapes=[pltpu.VMEM((ATTN_TILE, 128), F32)] * 10,
        compiler_params=pltpu.CompilerParams(
            dimension_semantics=("arbitrary", "arbitrary"), vmem_limit_bytes=56 * 1024 * 1024),
        name="attn",
    )(*([qkv] * 10), qmask)


SEQ_TILE = 1024
SSM_ROWS = 64


def _ssd_tile(u_ref, cw_ref, cb_ref, dtb_ref, a_ref, dsk_ref, ng_ref, ltri_ref, smask_ref,
              vm_ref, o_ref, ext_ref, act_ref, hs_ref):
    ts = SEQ_TILE

    ext_ref[8:8 + ts, :] = u_ref[:, 256:1024]
    for c in range(ts // SSM_ROWS):
        for gc in range(3):
            cols = slice(gc * 256, (gc + 1) * 256)
            win = ext_ref[pl.ds(c * SSM_ROWS, SSM_ROWS + 8), cols]
            acc = cb_ref[:, cols] + cw_ref[SSM_CONV - 1:SSM_CONV, cols] * win[8:8 + SSM_ROWS, :]
            for sh in range(1, SSM_CONV):
                k = SSM_CONV - 1 - sh
                acc = acc + cw_ref[k:k + 1, cols] * pltpu.roll(win, sh, axis=0)[8:8 + SSM_ROWS, :]
            act_ref[pl.ds(c * SSM_ROWS, SSM_ROWS), cols] = _silu(acc)
    ext_ref[0:8, :] = ext_ref[ts:ts + 8, :]

    ii = lax.broadcasted_iota(jnp.int32, (CHUNK, CHUNK), 0)
    jj = lax.broadcasted_iota(jnp.int32, (CHUNK, CHUNK), 1)
    causal = jj <= ii
    ltri = ltri_ref[...]

    hs = hs_ref[...]
    for c in range(ts // CHUNK):
        rr = pl.ds(c * CHUNK, CHUNK)
        z = u_ref[rr, 0:256]
        xs = act_ref[rr, 0:256]
        bm = act_ref[rr, 256:512]
        cm = act_ref[rr, 512:768]
        dt = _softplus(u_ref[rr, 1024:1280] + dtb_ref[...])
        da = dt * (-jnp.exp(a_ref[...]))
        da_hi = da.astype(BF16)
        rem = da - da_hi.astype(F32)
        da_mid = rem.astype(BF16)
        da_lo = (rem - da_mid.astype(F32)).astype(BF16)
        acum = _dot(ltri, da_hi) + _dot(ltri, da_mid) + _dot(ltri, da_lo)
        total = acum[CHUNK - 1:CHUNK, :]
        bmb = bm.astype(BF16)
        cmb = cm.astype(BF16)
        xdt = xs * dt
        acum_t = (jnp.transpose(acum[:, 0:128]), jnp.transpose(acum[:, 128:256]))
        cbs = [_dot_nt(cmb[:, g * 128:(g + 1) * 128], bmb[:, g * 128:(g + 1) * 128]) for g in range(SSM_GROUPS)]
        mhs = []
        for h in range(GROUP_HEADS):
            col = acum[:, h * HEAD_DIM:h * HEAD_DIM + 1]
            rsel = (h % 2) * HEAD_DIM
            row = acum_t[h // 2][rsel:rsel + 1, :]
            lmat = jnp.exp(jnp.where(causal, col - row, NEG))
            mhs.append((cbs[h // 2] * lmat).astype(BF16))
        xdtb = xdt.astype(BF16)
        xdt4 = jnp.concatenate([xdtb * vm_ref[h:h + 1, :] for h in range(GROUP_HEADS)], axis=0)
        y = _dot(cmb, hs.astype(BF16)) * jnp.exp(acum) + _dot(jnp.concatenate(mhs, axis=1), xdt4)
        xw = (xs * (jnp.exp(total - acum) * dt)).astype(BF16)
        hs = jnp.exp(total) * hs + _dot_tn(bmb, xw) * smask_ref[...]
        y = (y + dsk_ref[...] * xs) * _silu(z)
        halves = []
        for g in range(SSM_GROUPS):
            yg = y[:, g * 128:(g + 1) * 128]
            halves.append(yg * lax.rsqrt(jnp.mean(yg * yg, axis=-1, keepdims=True) + EPS))
        o_ref[rr, :] = jnp.concatenate(halves, axis=1) * ng_ref[...]
    hs_ref[...] = hs


def _ret_tile(u_ref, dmat_ref, zeta_ref, xi_ref, gch_ref, bd_ref, rmask_ref, qm_ref, vm_ref,
              o_ref, r_ref):
    r = r_ref[...]
    for c in range(SEQ_TILE // CHUNK):
        rr = pl.ds(c * CHUNK, CHUNK)
        q = u_ref[rr, 0:256]
        k = u_ref[rr, 256:512]
        v = u_ref[rr, 512:768]
        kb = k.astype(BF16)
        qb = q.astype(BF16)
        q4 = jnp.concatenate([qb * qm_ref[h:h + 1, :] for h in range(GROUP_HEADS)], axis=0)
        s4 = _dot_nt(q4, kb) * dmat_ref[...]
        s_cat = jnp.concatenate([s4[h * CHUNK:(h + 1) * CHUNK, :] for h in range(GROUP_HEADS)], axis=1)
        vb = v.astype(BF16)
        v4 = jnp.concatenate([vb * vm_ref[h:h + 1, :] for h in range(GROUP_HEADS)], axis=0)
        y = _dot(qb, r.astype(BF16)) * xi_ref[...] + _dot(s_cat.astype(BF16), v4)
        o_ref[rr, :] = y
        kz = (k * zeta_ref[...]).astype(BF16)
        r = gch_ref[...] * r + _dot_tn(kz, vb) * rmask_ref[...]
    r_ref[...] = r
    for hlf in range(2):
        rr = pl.ds(hlf * (SEQ_TILE // 2), SEQ_TILE // 2)
        y = o_ref[rr, :]
        ss = _dot((y * y).astype(BF16), bd_ref[...])
        o_ref[rr, :] = y * lax.rsqrt(ss * (1.0 / HEAD_DIM) + EPS) * _silu(u_ref[rr, 768:1024])


def _seqmix_kernel(*refs, layer):
    ssd_in, ret_in = list(refs[0:10]), refs[10:19]
    for j in range(2, 7):
        ssd_in[j] = ssd_in[j].at[pl.ds(layer, 1)]
    yb_ref, yc_ref, ext_ref, act_ref, hs_ref, r_ref = refs[19:25]

    @pl.when(pl.program_id(1) == 0)
    def _():
        ext_ref[0:8, :] = jnp.zeros((8, 768), F32)
        hs_ref[...] = jnp.zeros(hs_ref.shape, F32)
        r_ref[...] = jnp.zeros(r_ref.shape, F32)

    _ssd_tile(*ssd_in, yb_ref, ext_ref, act_ref, hs_ref)
    _ret_tile(*ret_in, yc_ref, r_ref)


def _seqmix(ub, cw, cb, dtb, a_exp, dsk, ng, ltri, smask, uc, dmat, zeta, xi, gch, bdn, rmask, qmask, vmask,
            bsz, seq, layer):
    ts = SEQ_TILE
    nt = seq // ts
    const = lambda b, n: (0, 0)
    lay = lambda b, n: (layer, 0, 0)
    tok = lambda b, n: (b * nt + n, 0)
    return pl.pallas_call(
        functools.partial(_seqmix_kernel, layer=layer),
        grid=(bsz, nt),
        in_specs=[
            pl.BlockSpec((ts, 1280), tok),
            pl.BlockSpec((None, 8, 768), lay), pl.BlockSpec((DEPTH, 768), const),
            pl.BlockSpec((DEPTH, 256), const), pl.BlockSpec((DEPTH, 256), const),
            pl.BlockSpec((DEPTH, 256), const), pl.BlockSpec((DEPTH, 256), const),
            pl.BlockSpec((CHUNK, CHUNK), const), pl.BlockSpec((256, 256), const),
            pl.BlockSpec((16, 256), const),
            pl.BlockSpec((ts, 1024), tok),
            pl.BlockSpec((GROUP_HEADS * CHUNK, CHUNK), const),
            pl.BlockSpec((CHUNK, 256), const), pl.BlockSpec((CHUNK, 256), const),
            pl.BlockSpec((1, 256), const), pl.BlockSpec((256, 256), const),
            pl.BlockSpec((256, 256), const),
            pl.BlockSpec((16, 256), const), pl.BlockSpec((16, 256), const),
        ],
        out_specs=[pl.BlockSpec((ts, 256), tok), pl.BlockSpec((ts, 256), tok)],
        out_shape=[jax.ShapeDtypeStruct((bsz * seq, GROUP_WIDTH), F32)] * 2,
        scratch_shapes=[pltpu.VMEM((ts + 8, 768), F32), pltpu.VMEM((ts, 768), F32), pltpu.VMEM((256, 256), F32),
                        pltpu.VMEM((256, 256), F32)],
        compiler_params=pltpu.CompilerParams(
            dimension_semantics=("arbitrary", "arbitrary"), vmem_limit_bytes=56 * 1024 * 1024),
        name="seqmix",
    )(ub, cw, cb, dtb, a_exp, dsk, ng, ltri, smask, vmask, uc, dmat, zeta, xi, gch, bdn, rmask, qmask, vmask)


FF_CHUNK = 1024


def _rms_rows(x, g):
    return x * lax.rsqrt(jnp.mean(x * x, axis=-1, keepdims=True) + EPS) * g


def _ffn_kernel(x_ref, ya_ref, yb_ref, yc_ref, yd_ref, p_ref, wout_ref, gf_ref, wup_ref, wdn_ref,
                gp_ref, wple_ref, wgate_ref, o_ref, hb_ref, *, layer):
    gf_ref, gp_ref = gf_ref.at[pl.ds(layer, 1)], gp_ref.at[pl.ds(layer, 1)]
    tm = x_ref.shape[0]
    halves = [pl.ds(0, tm // 2), pl.ds(tm // 2, tm // 2)]
    x1 = []
    for rr in halves:
        acc = x_ref[rr, :]
        for m, y_ref in enumerate((ya_ref, yb_ref, yc_ref, yd_ref)):
            acc = acc + _dot(y_ref[rr, :].astype(BF16), wout_ref[m * 256:(m + 1) * 256, :])
        hb_ref[rr, :] = _rms_rows(acc, gf_ref[...]).astype(BF16)
        x1.append(acc)

    def down(rr, c):
        up = jnp.maximum(_dot(hb_ref[rr, :], wup_ref[:, c * FF_CHUNK:(c + 1) * FF_CHUNK]), 0.0)
        return _dot((up * up).astype(BF16), wdn_ref[c * FF_CHUNK:(c + 1) * FF_CHUNK, :])

    ffs = [down(rr, 0) for rr in halves]
    for c in range(1, D_FF // FF_CHUNK):
        ffs = [ff + down(rr, c) for ff, rr in zip(ffs, halves)]
    for rr, acc, ff in zip(halves, x1, ffs):
        x2 = acc + ff
        gate = _sigmoid(_dot(_rms_rows(x2, gp_ref[...]).astype(BF16), wgate_ref[...]))
        o_ref[rr, :] = x2 + _dot(p_ref[rr, :].astype(BF16), wple_ref[...]) * gate


def _ffn(x2d, ya, yb, yc, yd, p_all, wout, gf, wup, wdn, gp, wple, wgate, tm, layer):
    t = x2d.shape[0]
    tok = lambda i: (i, 0)
    lay = lambda i: (layer, 0, 0)
    once = dict(pipeline_mode=pl.Buffered(1))
    return pl.pallas_call(
        functools.partial(_ffn_kernel, layer=layer),
        grid=(t // tm,),
        in_specs=[
            pl.BlockSpec((tm, D_MODEL), tok),
            pl.BlockSpec((tm, 256), tok), pl.BlockSpec((tm, 256), tok),
            pl.BlockSpec((tm, 256), tok), pl.BlockSpec((tm, 256), tok),
            pl.BlockSpec((None, tm, PLE_DIM), lambda i: (layer, i, 0)),
            pl.BlockSpec((None, D_MODEL, D_MODEL), lay, **once),
            pl.BlockSpec((DEPTH, D_MODEL), lambda i: (0, 0)),
            pl.BlockSpec((None, D_MODEL, D_FF), lay, **once),
            pl.BlockSpec((None, D_FF, D_MODEL), lay, **once),
            pl.BlockSpec((DEPTH, D_MODEL), lambda i: (0, 0)),
            pl.BlockSpec((None, PLE_DIM, D_MODEL), lay, **once),
            pl.BlockSpec((None, D_MODEL, D_MODEL), lay, **once),
        ],
        out_specs=pl.BlockSpec((tm, D_MODEL), tok),
        out_shape=jax.ShapeDtypeStruct((t, D_MODEL), F32),
        scratch_shapes=[pltpu.VMEM((tm, D_MODEL), BF16)],
        compiler_params=pltpu.CompilerParams(
            dimension_semantics=("arbitrary",), vmem_limit_bytes=56 * 1024 * 1024),
        name="ffn",
    )(x2d, ya, yb, yc, yd, p_all, wout, gf, wup, wdn, gp, wple, wgate)


def _head_mask(head_of_lane):
    m = np.zeros((8, GROUP_WIDTH), np.float32)
    for h in range(GROUP_HEADS):
        m[h] = head_of_lane == h
    return m


def _tables(seq):
    pos = jnp.arange(seq, dtype=F32)
    ang_a = ROPE_THETA ** (-jnp.arange(0, HEAD_DIM, 2, dtype=F32) / HEAD_DIM)
    ang_c = 1.0 / (10000.0 ** jnp.linspace(0.0, 1.0, HEAD_DIM // 2, dtype=F32))

    def cs(ang):
        a = pos[:, None] * ang[None, :]
        return jnp.tile(jnp.cos(a), (1, 4)), jnp.tile(jnp.sin(a), (1, 4))

    cosa, sina = cs(ang_a)
    cosc, sinc = cs(ang_c)
    log_g = jnp.log(1.0 - 2.0 ** (-5.0 - jnp.arange(GROUP_HEADS, dtype=F32)))
    idx = jnp.arange(CHUNK, dtype=F32)
    dist = idx[:, None] - idx[None, :]
    dmat = jnp.where((dist >= 0)[None], jnp.exp(jnp.maximum(dist, 0.0)[None] * log_g[:, None, None]), 0.0)
    zeta = jnp.exp((CHUNK - 1 - idx)[:, None] * log_g)[:, _HEAD_PERM]
    xi = jnp.exp((idx + 1.0)[:, None] * log_g)[:, _HEAD_NAT]
    gch = jnp.exp(CHUNK * log_g)[_HEAD_NAT][None, :]
    return dict(
        cosa=cosa, sina=sina, cosc=cosc, sinc=sinc, dmat=dmat, zeta=zeta, xi=xi, gch=gch,
        bd_perm=jnp.asarray(_HEAD_PERM[:, None] == _HEAD_PERM[None, :], BF16),
        bd_nat=jnp.asarray(_HEAD_NAT[:, None] == _HEAD_NAT[None, :], BF16),
        rmask=jnp.asarray(_HEAD_PERM[:, None] == _HEAD_NAT[None, :], F32),
        smask=jnp.asarray((_LANE[:, None] // SSM_STATE) == (_HEAD_NAT[None, :] // (GROUP_HEADS // SSM_GROUPS)), F32),
        qmask_bf=jnp.asarray(np.concatenate([_head_mask(_HEAD_PERM)] * 2), BF16),
        vmask_bf=jnp.asarray(np.concatenate([_head_mask(_HEAD_NAT)] * 2), BF16),
        ltri=jnp.asarray(np.tril(np.ones((CHUNK, CHUNK), np.float32)), BF16),
    )


def _build_w_cat_t(w_in):
    wt = jnp.transpose(w_in, (0, 2, 1))

    depth = wt.shape[0]

    def perm(c0):
        blk = wt[:, c0:c0 + 256].reshape(depth, GROUP_HEADS, 2, 32, D_MODEL)
        return blk.transpose(0, 2, 1, 3, 4).reshape(depth, 256, D_MODEL)

    dt = jnp.repeat(wt[:, _BDT:_BDT + GROUP_HEADS], HEAD_DIM, axis=1)
    parts = [perm(_A0), perm(_A0 + 256), wt[:, _A0 + 512:_BDT], dt,
             perm(_C0), perm(_C0 + 256), wt[:, _C0 + 512:IN_COLS]]
    return jnp.concatenate(parts, axis=1).astype(BF16)


def kernel(x, p, norm_mix, w_in, attn_q_norm, attn_k_norm, ssm_conv_w, ssm_conv_b, ssm_dt_bias,
           ssm_a_log, ssm_d, ssm_norm, conv_dw_w, conv_dw_b, conv_ln_g, conv_ln_b, w_out,
           norm_ffn, w_up, w_down, norm_ple, w_ple, w_ple_gate):
    bsz, seq, _ = x.shape
    depth = w_in.shape[0]
    tm = 512
    tb = _tables(seq)
    w_cat = _build_w_cat_t(w_in)
    g_mix, g_ffn, g_ple = norm_mix, norm_ffn, norm_ple
    gq, gk = attn_q_norm[:, _PERM % HEAD_DIM], attn_k_norm[:, _PERM % HEAD_DIM]
    cw = jnp.pad(ssm_conv_w, ((0, 0), (0, 8 - SSM_CONV), (0, 0)))
    dw = jnp.pad(conv_dw_w, ((0, 0), (0, 32 - CONF_KERNEL), (0, 0)))
    dtb, alog, dsk = ssm_dt_bias[:, _HEAD_NAT], ssm_a_log[:, _HEAD_NAT], ssm_d[:, _HEAD_NAT]
    wout, wup, wdn = w_out.astype(BF16), w_up.astype(BF16), w_down.astype(BF16)
    wple, wgate = w_ple.astype(BF16), w_ple_gate.astype(BF16)
    p_all = p.reshape(depth, bsz * seq, PLE_DIM)
    x2d = x.reshape(bsz * seq, D_MODEL)
    for i in range(depth):
        oa, ob, oc, yd = _inproj(x2d, g_mix, w_cat, tb["cosa"], tb["sina"], tb["cosc"], tb["sinc"],
                                 gq, gk, tb["bd_perm"], dw, conv_dw_b, conv_ln_g, conv_ln_b,
                                 seq, tm, i)
        ya = _attention(oa, tb["qmask_bf"], bsz, seq)
        yb, yc = _seqmix(ob, cw, ssm_conv_b, dtb, alog, dsk, ssm_norm, tb["ltri"], tb["smask"],
                         oc, tb["dmat"].reshape(GROUP_HEADS * CHUNK, CHUNK), tb["zeta"], tb["xi"], tb["gch"],
                         tb["bd_nat"], tb["rmask"], tb["qmask_bf"], tb["vmask_bf"], bsz, seq, i)
        x2d = _ffn(x2d, ya, yb, yc, yd, p_all, wout, g_ffn, wup, wdn, g_ple, wple, wgate, tm, i)
    return x2d.reshape(bsz, seq, D_MODEL)
```

```python
import functools

import numpy as np
import jax
import jax.numpy as jnp
from jax import lax
from jax.experimental import pallas as pl
from jax.experimental.pallas import tpu as pltpu

F32 = jnp.float32
BF16 = jnp.bfloat16

D_MODEL = 1024
GROUP_WIDTH = 256
GROUP_HEADS = 4
HEAD_DIM = 64
EPS = 1e-6
ATTN_PATTERNS = ((128, 1), (512, 4), (2048, 16))
ATTN_BLOCK = 128
ROPE_THETA = 10000.0
SSM_STATE = 128
SSM_GROUPS = 2
SSM_CONV = 4
CHUNK = 128
CONF_KERNEL = 31
D_FF = 4 * D_MODEL
DEPTH = 2
PLE_DIM = 256
NEG = -1e30
LOG2E = 1.4426950408889634
CONF_HALO = 32
CONF_ROWS = 64

_A0 = 0
_B0 = 3 * GROUP_WIDTH
_BZ, _BX, _BB, _BC = _B0, _B0 + 256, _B0 + 512, _B0 + 768
_BDT = _B0 + 1024
_C0 = _BDT + GROUP_HEADS
_D0 = _C0 + 4 * GROUP_WIDTH
IN_COLS = _D0 + 2 * GROUP_WIDTH

_LANE = np.arange(GROUP_WIDTH)
_PERM = ((_LANE % 128) // 32) * HEAD_DIM + (_LANE // 128) * 32 + (_LANE % 32)
_HEAD_PERM = (_LANE % 128) // 32
_HEAD_NAT = _LANE // HEAD_DIM

N_PROJ = 14 * GROUP_WIDTH


def _sigmoid(x):
    return 1.0 / (1.0 + jnp.exp(-x))


def _silu(x):
    return x * _sigmoid(x)


def _softplus(x):
    return jnp.maximum(x, 0.0) + jnp.log(1.0 + jnp.exp(-jnp.abs(x)))


def _dot(a, b):
    return jnp.dot(a, b, preferred_element_type=F32)


def _dot_nt(a, b):
    return lax.dot_general(a, b, (((1,), (1,)), ((), ())), preferred_element_type=F32)


def _dot_tn(a, b):
    return lax.dot_general(a, b, (((0,), (0,)), ((), ())), preferred_element_type=F32)


def _inproj_kernel(x_ref, g_ref, w_ref, cosa_ref, sina_ref, cosc_ref, sinc_ref,
                   gq_ref, gk_ref, bd_ref, dw_ref, db_ref, lg_ref, lb_ref,
                   oa_ref, ob_ref, oc_ref, yd_ref, hb0_ref, hb1_ref, ext_ref, *, nseq, layer):
    step = pl.program_id(0)
    tm = oa_ref.shape[0]
    g_ref, gq_ref, gk_ref, db_ref, lg_ref, lb_ref = (
        r.at[pl.ds(layer, 1)] for r in (g_ref, gq_ref, gk_ref, db_ref, lg_ref, lb_ref))

    def norm_into(dst_ref):
        x = x_ref[...]
        h = x * lax.rsqrt(jnp.mean(x * x, axis=-1, keepdims=True) + EPS) * g_ref[...]
        dst_ref[...] = h.astype(BF16)

    @pl.when(step == 0)
    def _():
        norm_into(hb0_ref)
        ext_ref[CONF_HALO + tm:CONF_HALO + tm + 8, :] = jnp.zeros((8, 256), F32)

    @pl.when(step % nseq == 1)
    def _():
        ext_ref[0:CONF_HALO, :] = jnp.zeros((CONF_HALO, 256), F32)

    rest = (w_ref, cosa_ref, sina_ref, cosc_ref, sinc_ref, gq_ref, gk_ref, bd_ref, dw_ref, db_ref, lg_ref,
            lb_ref, oa_ref, ob_ref, oc_ref, yd_ref, ext_ref)

    @pl.when((step > 0) & (step % 2 == 0))
    def _():
        norm_into(hb0_ref)
        _inproj_body(hb1_ref, *rest)

    @pl.when(step % 2 == 1)
    def _():
        norm_into(hb1_ref)
        _inproj_body(hb0_ref, *rest)


def _inproj_body(hb_ref, w_ref, cosa_ref, sina_ref, cosc_ref, sinc_ref, gq_ref, gk_ref, bd_ref,
                 dw_ref, db_ref, lg_ref, lb_ref, oa_ref, ob_ref, oc_ref, yd_ref, ext_ref):
    tm = oa_ref.shape[0]

    def mm(j):
        return _dot_nt(hb_ref[...], w_ref[j * 256:(j + 1) * 256, :])

    def rot(t, cos, sin):
        t1, t2 = t[:, :128], t[:, 128:]
        return jnp.concatenate([t1 * cos - t2 * sin, t2 * cos + t1 * sin], axis=1)

    def headnorm(t, gain):
        ss = _dot((t * t).astype(BF16), bd_ref[...])
        return t * lax.rsqrt(ss * (1.0 / HEAD_DIM) + EPS) * gain

    cosa, sina = cosa_ref[...], sina_ref[...]
    cosc, sinc = cosc_ref[...], sinc_ref[...]
    scale = HEAD_DIM ** -0.5
    ext_ref[CONF_HALO:CONF_HALO + tm, :] = mm(12) * _sigmoid(mm(13))
    conv_chunks = iter(range(tm // CONF_ROWS))

    def conv_some(n):
        for _ in range(n):
            c = next(conv_chunks, None)
            if c is not None:
                _conformer_rows(ext_ref, dw_ref, db_ref, lg_ref, lb_ref, yd_ref, c)

    oa_ref[:, 0:256] = mm(0)
    conv_some(1)
    oa_ref[:, 256:512] = mm(1)
    conv_some(1)
    oa_ref[:, 512:768] = mm(2)
    conv_some(1)
    for j in range(5):
        ob_ref[:, j * 256:(j + 1) * 256] = mm(3 + j)
        conv_some(1)
    oa_ref[:, 0:256] = rot(headnorm(oa_ref[:, 0:256], gq_ref[...]), cosa, sina) * (scale * LOG2E)
    oa_ref[:, 256:512] = rot(headnorm(oa_ref[:, 256:512], gk_ref[...]), cosa, sina)
    oc_ref[:, 0:256] = rot(mm(8), cosc, sinc).astype(BF16)
    oc_ref[:, 256:512] = (rot(mm(9), cosc, sinc) * scale).astype(BF16)
    oc_ref[:, 512:768] = mm(10).astype(BF16)
    oc_ref[:, 768:1024] = mm(11).astype(BF16)
    conv_some(tm // CONF_ROWS)
    ext_ref[0:CONF_HALO, :] = ext_ref[tm:tm + CONF_HALO, :]


def _conformer_rows(ext_ref, w_ref, b_ref, lg_ref, lb_ref, o_ref, c):
    nwin = CONF_ROWS + CONF_HALO + 8
    r0 = c * CONF_ROWS
    win = ext_ref[pl.ds(r0, nwin), :]
    acc = jnp.zeros((CONF_ROWS, 256), F32) + b_ref[...]
    for off in range(8):
        shifted = win if off == 0 else pltpu.roll(win, nwin - off, axis=0)
        for sh in range(CONF_HALO - CONF_KERNEL + 1, CONF_HALO + 1):
            if sh % 8 == off:
                k = sh - (CONF_HALO - CONF_KERNEL + 1)
                acc = acc + w_ref[k:k + 1, :] * shifted[sh - off:sh - off + CONF_ROWS, :]
    mu = jnp.mean(acc, axis=-1, keepdims=True)
    xc = acc - mu
    var = jnp.mean(xc * xc, axis=-1, keepdims=True)
    o_ref[pl.ds(r0, CONF_ROWS), :] = _silu(xc * lax.rsqrt(var + EPS) * lg_ref[...] + lb_ref[...])


def _inproj(x2d, g, w_cat, cosa, sina, cosc, sinc, gq, gk, bd, dw, db, lg, lb, seq, tm, layer):
    t = x2d.shape[0]
    nseq = seq // tm
    const = lambda i: (0, 0)
    lay = lambda i: (layer, 0, 0)
    nt = t // tm
    src = lambda i: (jnp.minimum(i, nt - 1), 0)
    tok = lambda i: (jnp.maximum(i - 1, 0), 0)
    pos = lambda i: (jnp.maximum(i - 1, 0) % nseq, 0)
    return pl.pallas_call(
        functools.partial(_inproj_kernel, nseq=nseq, layer=layer),
        grid=(nt + 1,),
        in_specs=[
            pl.BlockSpec((tm, D_MODEL), src),
            pl.BlockSpec((DEPTH, D_MODEL), const),
            pl.BlockSpec((None, N_PROJ, D_MODEL), lay),
            pl.BlockSpec((tm, 128), pos), pl.BlockSpec((tm, 128), pos),
            pl.BlockSpec((tm, 128), pos), pl.BlockSpec((tm, 128), pos),
            pl.BlockSpec((DEPTH, 256), const), pl.BlockSpec((DEPTH, 256), const),
            pl.BlockSpec((256, 256), const),
            pl.BlockSpec((None, 32, 256), lay), pl.BlockSpec((DEPTH, 256), const),
            pl.BlockSpec((DEPTH, 256), const), pl.BlockSpec((DEPTH, 256), const),
        ],
        out_specs=[
            pl.BlockSpec((tm, 768), tok), pl.BlockSpec((tm, 1280), tok),
            pl.BlockSpec((tm, 1024), tok), pl.BlockSpec((tm, 256), tok),
        ],
        out_shape=[
            jax.ShapeDtypeStruct((t, 768), F32), jax.ShapeDtypeStruct((t, 1280), F32),
            jax.ShapeDtypeStruct((t, 1024), BF16), jax.ShapeDtypeStruct((t, 256), F32),
        ],
        scratch_shapes=[pltpu.VMEM((tm, D_MODEL), BF16), pltpu.VMEM((tm, D_MODEL), BF16),
                        pltpu.VMEM((tm + CONF_HALO + 8, 256), F32)],
        compiler_params=pltpu.CompilerParams(
            dimension_semantics=("arbitrary",), vmem_limit_bytes=56 * 1024 * 1024),
        name="inproj",
    )(x2d, g, w_cat, cosa, sina, cosc, sinc, gq, gk, bd, dw, db, lg, lb)


ATTN_TILE = 2048
ATTN_GROUP = 4


def _attn_kernel(*refs):
    q_refs, kp_refs, kc_refs, vp_refs, vc_refs = (refs[2 * i:2 * i + 2] for i in range(5))
    qm_ref, o_ref = refs[10], refs[11]
    m_refs, l_refs, u_refs = refs[12:16], refs[16:20], refs[20:22]
    tile = pl.program_id(1)

    ii = lax.broadcasted_iota(jnp.int32, (128, 256), 0)
    jj = lax.broadcasted_iota(jnp.int32, (128, 256), 1)
    band_bias = jnp.where((jj >= ii) & (jj <= ii + ATTN_BLOCK), 0.0, NEG).astype(F32)
    noprev_bias = jnp.where(jj < ATTN_BLOCK, NEG, 0.0).astype(F32)
    first_bias = band_bias + jnp.where(tile == 0, 1.0, 0.0).astype(F32) * noprev_bias
    lo64 = lax.broadcasted_iota(jnp.int32, (128, 128), 1) < HEAD_DIM

    def rows(start, dil):
        if dil == 1:
            return pl.ds(pl.multiple_of(start, 128), 128)
        return pl.ds(start, 128, stride=dil)

    def wide(pair, rr):
        return jnp.concatenate([pair[0][rr, :], pair[1][rr, :]], axis=1)

    def blocks(descs, dil, init):
        st = []
        for qs, klo_refs, vlo_refs, lo_s, hi_s, bias in descs:
            qr, lo, hi = rows(qs, dil), rows(lo_s, dil), rows(hi_s, dil)
            q = wide(q_refs, qr)
            k = jnp.concatenate([wide(klo_refs, lo), wide(kc_refs, hi)], axis=0).astype(BF16)
            v = [jnp.concatenate([vlo_refs[c][lo, :], vc_refs[c][hi, :]], axis=0).astype(BF16) for c in range(2)]
            qb = q.astype(BF16)
            q4 = jnp.concatenate([qb * qm_ref[h:h + 1, :] for h in range(GROUP_HEADS)], axis=0)
            st.append(dict(qr=qr, v=v, bias=bias, s_all=_dot_nt(q4, k)))
        if not init:
            for d in st:
                d["m_old"] = [m_refs[h][d["qr"], :] for h in range(GROUP_HEADS)]
                d["l_old"] = [l_refs[h][d["qr"], :] for h in range(GROUP_HEADS)]
        for d in st:
            d["m_new"], d["l_new"], d["alpha"], ps = [], [], [], []
            for h in range(GROUP_HEADS):
                s = d["s_all"][h * 128:(h + 1) * 128, :] + d["bias"]
                rmax = jnp.max(s, axis=-1, keepdims=True)
                m_new = jnp.broadcast_to(rmax, (128, 128)) if init else jnp.maximum(d["m_old"][h], rmax)
                p = jnp.exp2(s - jnp.concatenate([m_new, m_new], axis=1))
                rsum = jnp.sum(p, axis=-1, keepdims=True)
                if init:
                    d["l_new"].append(jnp.broadcast_to(rsum, (128, 128)))
                else:
                    alpha = jnp.exp2(d["m_old"][h] - m_new)
                    d["l_new"].append(alpha * d["l_old"][h] + rsum)
                    d["alpha"].append(alpha)
                d["m_new"].append(m_new)
                ps.append(p.astype(BF16))
            d["pv"] = []
            for c in range(2):
                pv = _dot(jnp.concatenate(ps[2 * c:2 * c + 2], axis=0), d["v"][c])
                d["pv"].append(jnp.where(lo64, pv[0:128, :], pv[128:256, :]))
        if not init:
            for d in st:
                d["u_old"] = [u_refs[c][d["qr"], :] for c in range(2)]
        for d in st:
            for h in range(GROUP_HEADS):
                m_refs[h][d["qr"], :] = d["m_new"][h]
                l_refs[h][d["qr"], :] = d["l_new"][h]
            for c in range(2):
                if init:
                    u_refs[c][d["qr"], :] = d["pv"][c]
                else:
                    a_c = jnp.where(lo64, d["alpha"][2 * c], d["alpha"][2 * c + 1])
                    u_refs[c][d["qr"], :] = d["u_old"][c] * a_c + d["pv"][c]

    def first_desc(r, span):
        return (r, kp_refs, vp_refs, ATTN_TILE - span + r, r, first_bias)

    def later_desc(r, m, span):
        base = r + (m - 1) * span
        return (base + span, kc_refs, vc_refs, base, base + span, band_bias)

    for pi, (_, dil) in enumerate(reversed(ATTN_PATTERNS)):
        span = ATTN_BLOCK * dil
        nblk = ATTN_TILE // span
        init = pi == 0
        grp = ATTN_GROUP
        if nblk == 1:
            def class_group(i, carry, dil=dil, span=span, init=init):
                blocks([first_desc(grp * i + j, span) for j in range(grp)], dil, init)
                return carry

            lax.fori_loop(0, dil // grp, class_group, 0)
        else:
            def per_class(r, carry, dil=dil, span=span, nblk=nblk, init=init):
                blocks([first_desc(r, span)] + [later_desc(r, j, span) for j in range(1, grp)], dil, init)

                def later_group(i, c):
                    blocks([later_desc(r, grp * (i + 1) + j, span) for j in range(grp)], dil, init)
                    return c

                if nblk > grp:
                    lax.fori_loop(0, nblk // grp - 1, later_group, 0)
                return carry

            lax.fori_loop(0, dil, per_class, 0)

    lo64w = lax.broadcasted_iota(jnp.int32, (256, 128), 1) < HEAD_DIM

    def finish(c, carry):
        rr = pl.ds(pl.multiple_of(c * 256, 256), 256)
        for hlf in range(2):
            den = jnp.where(lo64w, l_refs[2 * hlf][rr, :], l_refs[2 * hlf + 1][rr, :])
            o_ref[rr, hlf * 128:(hlf + 1) * 128] = u_refs[hlf][rr, :] / den
        return carry

    lax.fori_loop(0, ATTN_TILE // 256, finish, 0)


def _attention(qkv, qmask, bsz, seq):
    nt = seq // ATTN_TILE
    cur = lambda c: (lambda b, n: (b * nt + n, c))
    prev = lambda c: (lambda b, n: (b * nt + jnp.maximum(n - 1, 0), c))
    blk = (ATTN_TILE, 128)
    col_maps = [cur(0), cur(1),
                prev(2), prev(3), cur(2), cur(3),
                prev(4), prev(5), cur(4), cur(5)]
    return pl.pallas_call(
        _attn_kernel,
        grid=(bsz, nt),
        in_specs=[pl.BlockSpec(blk, m) for m in col_maps] + [pl.BlockSpec((16, 256), lambda b, n: (0, 0))],
        out_specs=pl.BlockSpec((ATTN_TILE, 256), lambda b, n: (b * nt + n, 0)),
        out_shape=jax.ShapeDtypeStruct((bsz * seq, GROUP_WIDTH), F32),
        scratch_shapes=[pltpu.VMEM((ATTN_TILE, 128), F32)] * 10,
        compiler_params=pltpu.CompilerParams(
            dimension_semantics=("arbitrary", "arbitrary"), vmem_limit_bytes=56 * 1024 * 1024),
        name="attn",
    )(*([qkv] * 10), qmask)


SEQ_TILE = 1024
SSM_ROWS = 64


def _ssd_tile(u_ref, cw_ref, cb_ref, dtb_ref, a_ref, dsk_ref, ng_ref, ltri_ref, smask_ref,
              vm_ref, o_ref, ext_ref, act_ref, hs_ref):
    ts = SEQ_TILE

    ext_ref[8:8 + ts, :] = u_ref[:, 256:1024]
    for c in range(ts // SSM_ROWS):
        for gc in range(3):
            cols = slice(gc * 256, (gc + 1) * 256)
            win = ext_ref[pl.ds(c * SSM_ROWS, SSM_ROWS + 8), cols]
            acc = cb_ref[:, cols] + cw_ref[SSM_CONV - 1:SSM_CONV, cols] * win[8:8 + SSM_ROWS, :]
            for sh in range(1, SSM_CONV):
                k = SSM_CONV - 1 - sh
                acc = acc + cw_ref[k:k + 1, cols] * pltpu.roll(win, sh, axis=0)[8:8 + SSM_ROWS, :]
            act_ref[pl.ds(c * SSM_ROWS, SSM_ROWS), cols] = _silu(acc)
    ext_ref[0:8, :] = ext_ref[ts:ts + 8, :]

    ii = lax.broadcasted_iota(jnp.int32, (CHUNK, CHUNK), 0)
    jj = lax.broadcasted_iota(jnp.int32, (CHUNK, CHUNK), 1)
    causal = jj <= ii
    ltri = ltri_ref[...]

    hs = hs_ref[...]
    for c in range(ts // CHUNK):
        rr = pl.ds(c * CHUNK, CHUNK)
        z = u_ref[rr, 0:256]
        xs = act_ref[rr, 0:256]
        bm = act_ref[rr, 256:512]
        cm = act_ref[rr, 512:768]
        dt = _softplus(u_ref[rr, 1024:1280] + dtb_ref[...])
        da = dt * (-jnp.exp(a_ref[...]))
        da_hi = da.astype(BF16)
        rem = da - da_hi.astype(F32)
        da_mid = rem.astype(BF16)
        da_lo = (rem - da_mid.astype(F32)).astype(BF16)
        acum = _dot(ltri, da_hi) + _dot(ltri, da_mid) + _dot(ltri, da_lo)
        total = acum[CHUNK - 1:CHUNK, :]
        bmb = bm.astype(BF16)
        cmb = cm.astype(BF16)
        xdt = xs * dt
        acum_t = (jnp.transpose(acum[:, 0:128]), jnp.transpose(acum[:, 128:256]))
        cbs = [_dot_nt(cmb[:, g * 128:(g + 1) * 128], bmb[:, g * 128:(g + 1) * 128]) for g in range(SSM_GROUPS)]
        mhs = []
        for h in range(GROUP_HEADS):
            col = acum[:, h * HEAD_DIM:h * HEAD_DIM + 1]
            rsel = (h % 2) * HEAD_DIM
            row = acum_t[h // 2][rsel:rsel + 1, :]
            lmat = jnp.exp(jnp.where(causal, col - row, NEG))
            mhs.append((cbs[h // 2] * lmat).astype(BF16))
        xdtb = xdt.astype(BF16)
        xdt4 = jnp.concatenate([xdtb * vm_ref[h:h + 1, :] for h in range(GROUP_HEADS)], axis=0)
        y = _dot(cmb, hs.astype(BF16)) * jnp.exp(acum) + _dot(jnp.concatenate(mhs, axis=1), xdt4)
        xw = (xs * (jnp.exp(total - acum) * dt)).astype(BF16)
        hs = jnp.exp(total) * hs + _dot_tn(bmb, xw) * smask_ref[...]
        y = (y + dsk_ref[...] * xs) * _silu(z)
        halves = []
        for g in range(SSM_GROUPS):
            yg = y[:, g * 128:(g + 1) * 128]
            halves.append(yg * lax.rsqrt(jnp.mean(yg * yg, axis=-1, keepdims=True) + EPS))
        o_ref[rr, :] = jnp.concatenate(halves, axis=1) * ng_ref[...]
    hs_ref[...] = hs


def _ret_tile(u_ref, dmat_ref, zeta_ref, xi_ref, gch_ref, bd_ref, rmask_ref, qm_ref, vm_ref,
              o_ref, r_ref):
    r = r_ref[...]
    for c in range(SEQ_TILE // CHUNK):
        rr = pl.ds(c * CHUNK, CHUNK)
        qb = u_ref[rr, 0:256]
        kb = u_ref[rr, 256:512]
        vb = u_ref[rr, 512:768]
        q4 = jnp.concatenate([qb * qm_ref[h:h + 1, :] for h in range(GROUP_HEADS)], axis=0)
        s4 = _dot_nt(q4, kb) * dmat_ref[...]
        s_cat = jnp.concatenate([s4[h * CHUNK:(h + 1) * CHUNK, :] for h in range(GROUP_HEADS)], axis=1)
        v4 = jnp.concatenate([vb * vm_ref[h:h + 1, :] for h in range(GROUP_HEADS)], axis=0)
        y = _dot(qb, r.astype(BF16)) * xi_ref[...] + _dot(s_cat.astype(BF16), v4)
        o_ref[rr, :] = y
        kz = (kb.astype(F32) * zeta_ref[...]).astype(BF16)
        r = gch_ref[...] * r + _dot_tn(kz, vb) * rmask_ref[...]
    r_ref[...] = r
    for hlf in range(2):
        rr = pl.ds(hlf * (SEQ_TILE // 2), SEQ_TILE // 2)
        y = o_ref[rr, :]
        ss = _dot((y * y).astype(BF16), bd_ref[...])
        o_ref[rr, :] = y * lax.rsqrt(ss * (1.0 / HEAD_DIM) + EPS) * _silu(u_ref[rr, 768:1024].astype(F32))


def _seqmix_kernel(*refs, layer):
    ssd_in, ret_in = list(refs[0:10]), refs[10:19]
    for j in range(2, 7):
        ssd_in[j] = ssd_in[j].at[pl.ds(layer, 1)]
    yb_ref, yc_ref, ext_ref, act_ref, hs_ref, r_ref = refs[19:25]

    @pl.when(pl.program_id(1) == 0)
    def _():
        ext_ref[0:8, :] = jnp.zeros((8, 768), F32)
        hs_ref[...] = jnp.zeros(hs_ref.shape, F32)
        r_ref[...] = jnp.zeros(r_ref.shape, F32)

    _ssd_tile(*ssd_in, yb_ref, ext_ref, act_ref, hs_ref)
    _ret_tile(*ret_in, yc_ref, r_ref)


def _seqmix(ub, cw, cb, dtb, a_exp, dsk, ng, ltri, smask, uc, dmat, zeta, xi, gch, bdn, rmask, qmask, vmask,
            bsz, seq, layer):
    ts = SEQ_TILE
    nt = seq // ts
    const = lambda b, n: (0, 0)
    lay = lambda b, n: (layer, 0, 0)
    tok = lambda b, n: (b * nt + n, 0)
    return pl.pallas_call(
        functools.partial(_seqmix_kernel, layer=layer),
        grid=(bsz, nt),
        in_specs=[
            pl.BlockSpec((ts, 1280), tok),
            pl.BlockSpec((None, 8, 768), lay), pl.BlockSpec((DEPTH, 768), const),
            pl.BlockSpec((DEPTH, 256), const), pl.BlockSpec((DEPTH, 256), const),
            pl.BlockSpec((DEPTH, 256), const), pl.BlockSpec((DEPTH, 256), const),
            pl.BlockSpec((CHUNK, CHUNK), const), pl.BlockSpec((256, 256), const),
            pl.BlockSpec((16, 256), const),
            pl.BlockSpec((ts, 1024), tok),
            pl.BlockSpec((GROUP_HEADS * CHUNK, CHUNK), const),
            pl.BlockSpec((CHUNK, 256), const), pl.BlockSpec((CHUNK, 256), const),
            pl.BlockSpec((1, 256), const), pl.BlockSpec((256, 256), const),
            pl.BlockSpec((256, 256), const),
            pl.BlockSpec((16, 256), const), pl.BlockSpec((16, 256), const),
        ],
        out_specs=[pl.BlockSpec((ts, 256), tok), pl.BlockSpec((ts, 256), tok)],
        out_shape=[jax.ShapeDtypeStruct((bsz * seq, GROUP_WIDTH), F32)] * 2,
        scratch_shapes=[pltpu.VMEM((ts + 8, 768), F32), pltpu.VMEM((ts, 768), F32), pltpu.VMEM((256, 256), F32),
                        pltpu.VMEM((256, 256), F32)],
        compiler_params=pltpu.CompilerParams(
            dimension_semantics=("arbitrary", "arbitrary"), vmem_limit_bytes=56 * 1024 * 1024),
        name="seqmix",
    )(ub, cw, cb, dtb, a_exp, dsk, ng, ltri, smask, vmask, uc, dmat, zeta, xi, gch, bdn, rmask, qmask, vmask)


FF_CHUNK = 1024


def _rms_rows(x, g):
    return x * lax.rsqrt(jnp.mean(x * x, axis=-1, keepdims=True) + EPS) * g


def _ffn_kernel(x_ref, ya_ref, yb_ref, yc_ref, yd_ref, p_ref, wout_ref, gf_ref, wup_ref, wdn_ref,
                gp_ref, wple_ref, wgate_ref, o_ref, hb_ref, *, layer):
    gf_ref, gp_ref = gf_ref.at[pl.ds(layer, 1)], gp_ref.at[pl.ds(layer, 1)]
    tm = x_ref.shape[0]
    halves = [pl.ds(0, tm // 2), pl.ds(tm // 2, tm // 2)]
    x1 = []
    for rr in halves:
        acc = x_ref[rr, :]
        for m, y_ref in enumerate((ya_ref, yb_ref, yc_ref, yd_ref)):
            acc = acc + _dot(y_ref[rr, :].astype(BF16), wout_ref[m * 256:(m + 1) * 256, :])
        hb_ref[rr, :] = _rms_rows(acc, gf_ref[...]).astype(BF16)
        x1.append(acc)

    def down(rr, c):
        up = jnp.maximum(_dot(hb_ref[rr, :], wup_ref[:, c * FF_CHUNK:(c + 1) * FF_CHUNK]), 0.0)
        return _dot((up * up).astype(BF16), wdn_ref[c * FF_CHUNK:(c + 1) * FF_CHUNK, :])

    ffs = [down(rr, 0) for rr in halves]
    for c in range(1, D_FF // FF_CHUNK):
        ffs = [ff + down(rr, c) for ff, rr in zip(ffs, halves)]
    for rr, acc, ff in zip(halves, x1, ffs):
        x2 = acc + ff
        gate = _sigmoid(_dot(_rms_rows(x2, gp_ref[...]).astype(BF16), wgate_ref[...]))
        o_ref[rr, :] = x2 + _dot(p_ref[rr, :].astype(BF16), wple_ref[...]) * gate


def _ffn(x2d, ya, yb, yc, yd, p_all, wout, gf, wup, wdn, gp, wple, wgate, tm, layer):
    t = x2d.shape[0]
    tok = lambda i: (i, 0)
    lay = lambda i: (layer, 0, 0)
    once = dict(pipeline_mode=pl.Buffered(1))
    return pl.pallas_call(
        functools.partial(_ffn_kernel, layer=layer),
        grid=(t // tm,),
        in_specs=[
            pl.BlockSpec((tm, D_MODEL), tok),
            pl.BlockSpec((tm, 256), tok), pl.BlockSpec((tm, 256), tok),
            pl.BlockSpec((tm, 256), tok), pl.BlockSpec((tm, 256), tok),
            pl.BlockSpec((None, tm, PLE_DIM), lambda i: (layer, i, 0)),
            pl.BlockSpec((None, D_MODEL, D_MODEL), lay, **once),
            pl.BlockSpec((DEPTH, D_MODEL), lambda i: (0, 0)),
            pl.BlockSpec((None, D_MODEL, D_FF), lay, **once),
            pl.BlockSpec((None, D_FF, D_MODEL), lay, **once),
            pl.BlockSpec((DEPTH, D_MODEL), lambda i: (0, 0)),
            pl.BlockSpec((None, PLE_DIM, D_MODEL), lay, **once),
            pl.BlockSpec((None, D_MODEL, D_MODEL), lay, **once),
        ],
        out_specs=pl.BlockSpec((tm, D_MODEL), tok),
        out_shape=jax.ShapeDtypeStruct((t, D_MODEL), F32),
        scratch_shapes=[pltpu.VMEM((tm, D_MODEL), BF16)],
        compiler_params=pltpu.CompilerParams(
            dimension_semantics=("arbitrary",), vmem_limit_bytes=56 * 1024 * 1024),
        name="ffn",
    )(x2d, ya, yb, yc, yd, p_all, wout, gf, wup, wdn, gp, wple, wgate)


def _head_mask(head_of_lane):
    m = np.zeros((8, GROUP_WIDTH), np.float32)
    for h in range(GROUP_HEADS):
        m[h] = head_of_lane == h
    return m


def _tables(seq):
    pos = jnp.arange(seq, dtype=F32)
    ang_a = ROPE_THETA ** (-jnp.arange(0, HEAD_DIM, 2, dtype=F32) / HEAD_DIM)
    ang_c = 1.0 / (10000.0 ** jnp.linspace(0.0, 1.0, HEAD_DIM // 2, dtype=F32))

    def cs(ang):
        a = pos[:, None] * ang[None, :]
        return jnp.tile(jnp.cos(a), (1, 4)), jnp.tile(jnp.sin(a), (1, 4))

    cosa, sina = cs(ang_a)
    cosc, sinc = cs(ang_c)
    log_g = jnp.log(1.0 - 2.0 ** (-5.0 - jnp.arange(GROUP_HEADS, dtype=F32)))
    idx = jnp.arange(CHUNK, dtype=F32)
    dist = idx[:, None] - idx[None, :]
    dmat = jnp.where((dist >= 0)[None], jnp.exp(jnp.maximum(dist, 0.0)[None] * log_g[:, None, None]), 0.0)
    zeta = jnp.exp((CHUNK - 1 - idx)[:, None] * log_g)[:, _HEAD_PERM]
    xi = jnp.exp((idx + 1.0)[:, None] * log_g)[:, _HEAD_NAT]
    gch = jnp.exp(CHUNK * log_g)[_HEAD_NAT][None, :]
    return dict(
        cosa=cosa, sina=sina, cosc=cosc, sinc=sinc, dmat=dmat, zeta=zeta, xi=xi, gch=gch,
        bd_perm=jnp.asarray(_HEAD_PERM[:, None] == _HEAD_PERM[None, :], BF16),
        bd_nat=jnp.asarray(_HEAD_NAT[:, None] == _HEAD_NAT[None, :], BF16),
        rmask=jnp.asarray(_HEAD_PERM[:, None] == _HEAD_NAT[None, :], F32),
        smask=jnp.asarray((_LANE[:, None] // SSM_STATE) == (_HEAD_NAT[None, :] // (GROUP_HEADS // SSM_GROUPS)), F32),
        qmask_bf=jnp.asarray(np.concatenate([_head_mask(_HEAD_PERM)] * 2), BF16),
        vmask_bf=jnp.asarray(np.concatenate([_head_mask(_HEAD_NAT)] * 2), BF16),
        ltri=jnp.asarray(np.tril(np.ones((CHUNK, CHUNK), np.float32)), BF16),
    )


def _build_w_cat_t(w_in):
    wt = jnp.transpose(w_in, (0, 2, 1))

    depth = wt.shape[0]

    def perm(c0):
        blk = wt[:, c0:c0 + 256].reshape(depth, GROUP_HEADS, 2, 32, D_MODEL)
        return blk.transpose(0, 2, 1, 3, 4).reshape(depth, 256, D_MODEL)

    dt = jnp.repeat(wt[:, _BDT:_BDT + GROUP_HEADS], HEAD_DIM, axis=1)
    parts = [perm(_A0), perm(_A0 + 256), wt[:, _A0 + 512:_BDT], dt,
             perm(_C0), perm(_C0 + 256), wt[:, _C0 + 512:IN_COLS]]
    return jnp.concatenate(parts, axis=1).astype(BF16)


def kernel(x, p, norm_mix, w_in, attn_q_norm, attn_k_norm, ssm_conv_w, ssm_conv_b, ssm_dt_bias,
           ssm_a_log, ssm_d, ssm_norm, conv_dw_w, conv_dw_b, conv_ln_g, conv_ln_b, w_out,
           norm_ffn, w_up, w_down, norm_ple, w_ple, w_ple_gate):
    bsz, seq, _ = x.shape
    depth = w_in.shape[0]
    tm = 512
    tb = _tables(seq)
    w_cat = _build_w_cat_t(w_in)
    g_mix, g_ffn, g_ple = norm_mix, norm_ffn, norm_ple
    gq, gk = attn_q_norm[:, _PERM % HEAD_DIM], attn_k_norm[:, _PERM % HEAD_DIM]
    cw = jnp.pad(ssm_conv_w, ((0, 0), (0, 8 - SSM_CONV), (0, 0)))
    dw = jnp.pad(conv_dw_w, ((0, 0), (0, 32 - CONF_KERNEL), (0, 0)))
    dtb, alog, dsk = ssm_dt_bias[:, _HEAD_NAT], ssm_a_log[:, _HEAD_NAT], ssm_d[:, _HEAD_NAT]
    wout, wup, wdn = w_out.astype(BF16), w_up.astype(BF16), w_down.astype(BF16)
    wple, wgate = w_ple.astype(BF16), w_ple_gate.astype(BF16)
    p_all = p.reshape(depth, bsz * seq, PLE_DIM)
    x2d = x.reshape(bsz * seq, D_MODEL)
    for i in range(depth):
        oa, ob, oc, yd = _inproj(x2d, g_mix, w_cat, tb["cosa"], tb["sina"], tb["cosc"], tb["sinc"],
                                 gq, gk, tb["bd_perm"], dw, conv_dw_b, conv_ln_g, conv_ln_b,
                                 seq, tm, i)
        ya = _attention(oa, tb["qmask_bf"], bsz, seq)
        yb, yc = _seqmix(ob, cw, ssm_conv_b, dtb, alog, dsk, ssm_norm, tb["ltri"], tb["smask"],
                         oc, tb["dmat"].reshape(GROUP_HEADS * CHUNK, CHUNK), tb["zeta"], tb["xi"], tb["gch"],
                         tb["bd_nat"], tb["rmask"], tb["qmask_bf"], tb["vmask_bf"], bsz, seq, i)
        x2d = _ffn(x2d, ya, yb, yc, yd, p_all, wout, g_ffn, wup, wdn, g_ple, wple, wgate, tm, i)
    return x2d.reshape(bsz, seq, D_MODEL)
```

```python
import functools

import numpy as np
import jax
import jax.numpy as jnp
from jax import lax
from jax.experimental import pallas as pl
from jax.experimental.pallas import tpu as pltpu

F32 = jnp.float32
BF16 = jnp.bfloat16

D_MODEL = 1024
GROUP_WIDTH = 256
GROUP_HEADS = 4
HEAD_DIM = 64
EPS = 1e-6
ATTN_PATTERNS = ((128, 1), (512, 4), (2048, 16))
ATTN_BLOCK = 128
ROPE_THETA = 10000.0
SSM_STATE = 128
SSM_GROUPS = 2
SSM_CONV = 4
CHUNK = 128
CONF_KERNEL = 31
D_FF = 4 * D_MODEL
DEPTH = 2
PLE_DIM = 256
NEG = -1e30
LOG2E = 1.4426950408889634
CONF_HALO = 32
CONF_ROWS = 64

_A0 = 0
_B0 = 3 * GROUP_WIDTH
_BZ, _BX, _BB, _BC = _B0, _B0 + 256, _B0 + 512, _B0 + 768
_BDT = _B0 + 1024
_C0 = _BDT + GROUP_HEADS
_D0 = _C0 + 4 * GROUP_WIDTH
IN_COLS = _D0 + 2 * GROUP_WIDTH

_LANE = np.arange(GROUP_WIDTH)
_PERM = ((_LANE % 128) // 32) * HEAD_DIM + (_LANE // 128) * 32 + (_LANE % 32)
_HEAD_PERM = (_LANE % 128) // 32
_HEAD_NAT = _LANE // HEAD_DIM

N_PROJ = 14 * GROUP_WIDTH


def _sigmoid(x):
    return 1.0 / (1.0 + jnp.exp(-x))


def _silu(x):
    return x * _sigmoid(x)


def _softplus(x):
    return jnp.maximum(x, 0.0) + jnp.log(1.0 + jnp.exp(-jnp.abs(x)))


def _dot(a, b):
    return jnp.dot(a, b, preferred_element_type=F32)


def _dot_nt(a, b):
    return lax.dot_general(a, b, (((1,), (1,)), ((), ())), preferred_element_type=F32)


def _dot_tn(a, b):
    return lax.dot_general(a, b, (((0,), (0,)), ((), ())), preferred_element_type=F32)


def _inproj_kernel(x_ref, g_ref, w_ref, cosa_ref, sina_ref, cosc_ref, sinc_ref,
                   gq_ref, gk_ref, bd_ref, dw_ref, db_ref, lg_ref, lb_ref,
                   oa_ref, ob_ref, oc_ref, yd_ref, hb0_ref, hb1_ref, ext_ref, *, nseq, layer):
    step = pl.program_id(0)
    tm = oa_ref.shape[0]
    g_ref, gq_ref, gk_ref, db_ref, lg_ref, lb_ref = (
        r.at[pl.ds(layer, 1)] for r in (g_ref, gq_ref, gk_ref, db_ref, lg_ref, lb_ref))

    def norm_into(dst_ref):
        x = x_ref[...]
        h = x * lax.rsqrt(jnp.mean(x * x, axis=-1, keepdims=True) + EPS) * g_ref[...]
        dst_ref[...] = h.astype(BF16)

    @pl.when(step == 0)
    def _():
        norm_into(hb0_ref)
        ext_ref[CONF_HALO + tm:CONF_HALO + tm + 8, :] = jnp.zeros((8, 256), F32)

    @pl.when(step % nseq == 1)
    def _():
        ext_ref[0:CONF_HALO, :] = jnp.zeros((CONF_HALO, 256), F32)

    rest = (w_ref, cosa_ref, sina_ref, cosc_ref, sinc_ref, gq_ref, gk_ref, bd_ref, dw_ref, db_ref, lg_ref,
            lb_ref, oa_ref, ob_ref, oc_ref, yd_ref, ext_ref)

    @pl.when((step > 0) & (step % 2 == 0))
    def _():
        norm_into(hb0_ref)
        _inproj_body(hb1_ref, *rest)

    @pl.when(step % 2 == 1)
    def _():
        norm_into(hb1_ref)
        _inproj_body(hb0_ref, *rest)


def _inproj_body(hb_ref, w_ref, cosa_ref, sina_ref, cosc_ref, sinc_ref, gq_ref, gk_ref, bd_ref,
                 dw_ref, db_ref, lg_ref, lb_ref, oa_ref, ob_ref, oc_ref, yd_ref, ext_ref):
    tm = oa_ref.shape[0]

    def mm(j):
        return _dot_nt(hb_ref[...], w_ref[j * 256:(j + 1) * 256, :])

    def rot(t, cos, sin):
        t1, t2 = t[:, :128], t[:, 128:]
        return jnp.concatenate([t1 * cos - t2 * sin, t2 * cos + t1 * sin], axis=1)

    def headnorm(t, gain):
        ss = _dot((t * t).astype(BF16), bd_ref[...])
        return t * lax.rsqrt(ss * (1.0 / HEAD_DIM) + EPS) * gain

    cosa, sina = cosa_ref[...], sina_ref[...]
    cosc, sinc = cosc_ref[...], sinc_ref[...]
    scale = HEAD_DIM ** -0.5
    ext_ref[CONF_HALO:CONF_HALO + tm, :] = mm(12) * _sigmoid(mm(13))
    conv_chunks = iter(range(tm // CONF_ROWS))

    def conv_some(n):
        for _ in range(n):
            c = next(conv_chunks, None)
            if c is not None:
                _conformer_rows(ext_ref, dw_ref, db_ref, lg_ref, lb_ref, yd_ref, c)

    oa_ref[:, 0:256] = mm(0)
    conv_some(1)
    oa_ref[:, 256:512] = mm(1)
    conv_some(1)
    oa_ref[:, 512:768] = mm(2)
    conv_some(1)
    for j in range(5):
        ob_ref[:, j * 256:(j + 1) * 256] = mm(3 + j)
        conv_some(1)
    oa_ref[:, 0:256] = rot(headnorm(oa_ref[:, 0:256], gq_ref[...]), cosa, sina) * (scale * LOG2E)
    oa_ref[:, 256:512] = rot(headnorm(oa_ref[:, 256:512], gk_ref[...]), cosa, sina)
    oc_ref[:, 0:256] = rot(mm(8), cosc, sinc)
    oc_ref[:, 256:512] = rot(mm(9), cosc, sinc) * scale
    oc_ref[:, 512:768] = mm(10)
    oc_ref[:, 768:1024] = mm(11)
    conv_some(tm // CONF_ROWS)
    ext_ref[0:CONF_HALO, :] = ext_ref[tm:tm + CONF_HALO, :]


def _conformer_rows(ext_ref, w_ref, b_ref, lg_ref, lb_ref, o_ref, c):
    nwin = CONF_ROWS + CONF_HALO + 8
    r0 = c * CONF_ROWS
    win = ext_ref[pl.ds(r0, nwin), :]
    acc = jnp.zeros((CONF_ROWS, 256), F32) + b_ref[...]
    for off in range(8):
        shifted = win if off == 0 else pltpu.roll(win, nwin - off, axis=0)
        for sh in range(CONF_HALO - CONF_KERNEL + 1, CONF_HALO + 1):
            if sh % 8 == off:
                k = sh - (CONF_HALO - CONF_KERNEL + 1)
                acc = acc + w_ref[k:k + 1, :] * shifted[sh - off:sh - off + CONF_ROWS, :]
    mu = jnp.mean(acc, axis=-1, keepdims=True)
    xc = acc - mu
    var = jnp.mean(xc * xc, axis=-1, keepdims=True)
    o_ref[pl.ds(r0, CONF_ROWS), :] = _silu(xc * lax.rsqrt(var + EPS) * lg_ref[...] + lb_ref[...])


def _inproj(x2d, g, w_cat, cosa, sina, cosc, sinc, gq, gk, bd, dw, db, lg, lb, seq, tm, layer):
    t = x2d.shape[0]
    nseq = seq // tm
    const = lambda i: (0, 0)
    lay = lambda i: (layer, 0, 0)
    nt = t // tm
    src = lambda i: (jnp.minimum(i, nt - 1), 0)
    tok = lambda i: (jnp.maximum(i - 1, 0), 0)
    pos = lambda i: (jnp.maximum(i - 1, 0) % nseq, 0)
    return pl.pallas_call(
        functools.partial(_inproj_kernel, nseq=nseq, layer=layer),
        grid=(nt + 1,),
        in_specs=[
            pl.BlockSpec((tm, D_MODEL), src),
            pl.BlockSpec((DEPTH, D_MODEL), const),
            pl.BlockSpec((None, N_PROJ, D_MODEL), lay),
            pl.BlockSpec((tm, 128), pos), pl.BlockSpec((tm, 128), pos),
            pl.BlockSpec((tm, 128), pos), pl.BlockSpec((tm, 128), pos),
            pl.BlockSpec((DEPTH, 256), const), pl.BlockSpec((DEPTH, 256), const),
            pl.BlockSpec((256, 256), const),
            pl.BlockSpec((None, 32, 256), lay), pl.BlockSpec((DEPTH, 256), const),
            pl.BlockSpec((DEPTH, 256), const), pl.BlockSpec((DEPTH, 256), const),
        ],
        out_specs=[
            pl.BlockSpec((tm, 768), tok), pl.BlockSpec((tm, 1280), tok),
            pl.BlockSpec((tm, 1024), tok), pl.BlockSpec((tm, 256), tok),
        ],
        out_shape=[
            jax.ShapeDtypeStruct((t, 768), F32), jax.ShapeDtypeStruct((t, 1280), F32),
            jax.ShapeDtypeStruct((t, 1024), F32), jax.ShapeDtypeStruct((t, 256), F32),
        ],
        scratch_shapes=[pltpu.VMEM((tm, D_MODEL), BF16), pltpu.VMEM((tm, D_MODEL), BF16),
                        pltpu.VMEM((tm + CONF_HALO + 8, 256), F32)],
        compiler_params=pltpu.CompilerParams(
            dimension_semantics=("arbitrary",), vmem_limit_bytes=56 * 1024 * 1024),
        name="inproj",
    )(x2d, g, w_cat, cosa, sina, cosc, sinc, gq, gk, bd, dw, db, lg, lb)


ATTN_TILE = 2048
ATTN_GROUP = 4


def _attn_kernel(*refs):
    q_refs, kp_refs, kc_refs, vp_refs, vc_refs = (refs[2 * i:2 * i + 2] for i in range(5))
    qm_ref, o_ref = refs[10], refs[11]
    m_refs, l_refs, u_refs = refs[12:16], refs[16:20], refs[20:22]
    tile = pl.program_id(1)

    ii = lax.broadcasted_iota(jnp.int32, (128, 256), 0)
    jj = lax.broadcasted_iota(jnp.int32, (128, 256), 1)
    band_bias = jnp.where((jj >= ii) & (jj <= ii + ATTN_BLOCK), 0.0, NEG).astype(F32)
    noprev_bias = jnp.where(jj < ATTN_BLOCK, NEG, 0.0).astype(F32)
    first_bias = band_bias + jnp.where(tile == 0, 1.0, 0.0).astype(F32) * noprev_bias
    lo64 = lax.broadcasted_iota(jnp.int32, (128, 128), 1) < HEAD_DIM

    def rows(start, dil):
        if dil == 1:
            return pl.ds(pl.multiple_of(start, 128), 128)
        return pl.ds(start, 128, stride=dil)

    def wide(pair, rr):
        return jnp.concatenate([pair[0][rr, :], pair[1][rr, :]], axis=1)

    def blocks(descs, dil, init):
        st = []
        for qs, klo_refs, vlo_refs, lo_s, hi_s, bias in descs:
            qr, lo, hi = rows(qs, dil), rows(lo_s, dil), rows(hi_s, dil)
            q = wide(q_refs, qr)
            k = jnp.concatenate([wide(klo_refs, lo), wide(kc_refs, hi)], axis=0).astype(BF16)
            v = [jnp.concatenate([vlo_refs[c][lo, :], vc_refs[c][hi, :]], axis=0).astype(BF16) for c in range(2)]
            qb = q.astype(BF16)
            q4 = jnp.concatenate([qb * qm_ref[h:h + 1, :] for h in range(GROUP_HEADS)], axis=0)
            st.append(dict(qr=qr, v=v, bias=bias, s_all=_dot_nt(q4, k)))
        if not init:
            for d in st:
                d["m_old"] = [m_refs[h][d["qr"], :] for h in range(GROUP_HEADS)]
                d["l_old"] = [l_refs[h][d["qr"], :] for h in range(GROUP_HEADS)]
        for d in st:
            d["m_new"], d["l_new"], d["alpha"], ps = [], [], [], []
            for h in range(GROUP_HEADS):
                s = d["s_all"][h * 128:(h + 1) * 128, :] + d["bias"]
                rmax = jnp.max(s, axis=-1, keepdims=True)
                m_new = jnp.broadcast_to(rmax, (128, 128)) if init else jnp.maximum(d["m_old"][h], rmax)
                p = jnp.exp2(s - jnp.concatenate([m_new, m_new], axis=1))
                rsum = jnp.sum(p, axis=-1, keepdims=True)
                if init:
                    d["l_new"].append(jnp.broadcast_to(rsum, (128, 128)))
                else:
                    alpha = jnp.exp2(d["m_old"][h] - m_new)
                    d["l_new"].append(alpha * d["l_old"][h] + rsum)
                    d["alpha"].append(alpha)
                d["m_new"].append(m_new)
                ps.append(p.astype(BF16))
            d["pv"] = []
            for c in range(2):
                pv = _dot(jnp.concatenate(ps[2 * c:2 * c + 2], axis=0), d["v"][c])
                d["pv"].append(jnp.where(lo64, pv[0:128, :], pv[128:256, :]))
        if not init:
            for d in st:
                d["u_old"] = [u_refs[c][d["qr"], :] for c in range(2)]
        for d in st:
            for h in range(GROUP_HEADS):
                m_refs[h][d["qr"], :] = d["m_new"][h]
                l_refs[h][d["qr"], :] = d["l_new"][h]
            for c in range(2):
                if init:
                    u_refs[c][d["qr"], :] = d["pv"][c]
                else:
                    a_c = jnp.where(lo64, d["alpha"][2 * c], d["alpha"][2 * c + 1])
                    u_refs[c][d["qr"], :] = d["u_old"][c] * a_c + d["pv"][c]

    def first_desc(r, span):
        return (r, kp_refs, vp_refs, ATTN_TILE - span + r, r, first_bias)

    def later_desc(r, m, span):
        base = r + (m - 1) * span
        return (base + span, kc_refs, vc_refs, base, base + span, band_bias)

    for pi, (_, dil) in enumerate(reversed(ATTN_PATTERNS)):
        span = ATTN_BLOCK * dil
        nblk = ATTN_TILE // span
        init = pi == 0
        grp = ATTN_GROUP
        if nblk == 1:
            def class_group(i, carry, dil=dil, span=span, init=init):
                blocks([first_desc(grp * i + j, span) for j in range(grp)], dil, init)
                return carry

            lax.fori_loop(0, dil // grp, class_group, 0)
        else:
            def per_class(r, carry, dil=dil, span=span, nblk=nblk, init=init):
                blocks([first_desc(r, span)] + [later_desc(r, j, span) for j in range(1, grp)], dil, init)

                def later_group(i, c):
                    blocks([later_desc(r, grp * (i + 1) + j, span) for j in range(grp)], dil, init)
                    return c

                if nblk > grp:
                    lax.fori_loop(0, nblk // grp - 1, later_group, 0)
                return carry

            lax.fori_loop(0, dil, per_class, 0)

    lo64w = lax.broadcasted_iota(jnp.int32, (256, 128), 1) < HEAD_DIM

    def finish(c, carry):
        rr = pl.ds(pl.multiple_of(c * 256, 256), 256)
        for hlf in range(2):
            den = jnp.where(lo64w, l_refs[2 * hlf][rr, :], l_refs[2 * hlf + 1][rr, :])
            o_ref[rr, hlf * 128:(hlf + 1) * 128] = u_refs[hlf][rr, :] / den
        return carry

    lax.fori_loop(0, ATTN_TILE // 256, finish, 0)


def _attention(qkv, qmask, bsz, seq):
    nt = seq // ATTN_TILE
    cur = lambda c: (lambda b, n: (b * nt + n, c))
    prev = lambda c: (lambda b, n: (b * nt + jnp.maximum(n - 1, 0), c))
    blk = (ATTN_TILE, 128)
    col_maps = [cur(0), cur(1),
                prev(2), prev(3), cur(2), cur(3),
                prev(4), prev(5), cur(4), cur(5)]
    return pl.pallas_call(
        _attn_kernel,
        grid=(bsz, nt),
        in_specs=[pl.BlockSpec(blk, m) for m in col_maps] + [pl.BlockSpec((16, 256), lambda b, n: (0, 0))],
        out_specs=pl.BlockSpec((ATTN_TILE, 256), lambda b, n: (b * nt + n, 0)),
        out_shape=jax.ShapeDtypeStruct((bsz * seq, GROUP_WIDTH), F32),
        scratch_shapes=[pltpu.VMEM((ATTN_TILE, 128), F32)] * 10,
        compiler_params=pltpu.CompilerParams(
            dimension_semantics=("arbitrary", "arbitrary"), vmem_limit_bytes=56 * 1024 * 1024),
        name="attn",
    )(*([qkv] * 10), qmask)


SEQ_TILE = 1024
SSM_ROWS = 64


def _ssd_tile(u_ref, cw_ref, cb_ref, dtb_ref, a_ref, dsk_ref, ng_ref, ltri_ref, smask_ref,
              vm_ref, o_ref, ext_ref, act_ref, hs_ref):
    ts = SEQ_TILE

    ext_ref[8:8 + ts, :] = u_ref[:, 256:1024]
    for c in range(ts // SSM_ROWS):
        for gc in range(3):
            cols = slice(gc * 256, (gc + 1) * 256)
            win = ext_ref[pl.ds(c * SSM_ROWS, SSM_ROWS + 8), cols]
            acc = cb_ref[:, cols] + cw_ref[SSM_CONV - 1:SSM_CONV, cols] * win[8:8 + SSM_ROWS, :]
            for sh in range(1, SSM_CONV):
                k = SSM_CONV - 1 - sh
                acc = acc + cw_ref[k:k + 1, cols] * pltpu.roll(win, sh, axis=0)[8:8 + SSM_ROWS, :]
            act_ref[pl.ds(c * SSM_ROWS, SSM_ROWS), cols] = _silu(acc)
    ext_ref[0:8, :] = ext_ref[ts:ts + 8, :]

    ii = lax.broadcasted_iota(jnp.int32, (CHUNK, CHUNK), 0)
    jj = lax.broadcasted_iota(jnp.int32, (CHUNK, CHUNK), 1)
    causal = jj <= ii
    ltri = ltri_ref[...]

    hs = hs_ref[...]
    for c in range(ts // CHUNK):
        rr = pl.ds(c * CHUNK, CHUNK)
        z = u_ref[rr, 0:256]
        xs = act_ref[rr, 0:256]
        bm = act_ref[rr, 256:512]
        cm = act_ref[rr, 512:768]
        dt = _softplus(u_ref[rr, 1024:1280] + dtb_ref[...])
        da = dt * (-jnp.exp(a_ref[...]))
        da_hi = da.astype(BF16)
        rem = da - da_hi.astype(F32)
        da_mid = rem.astype(BF16)
        da_lo = (rem - da_mid.astype(F32)).astype(BF16)
        acum = _dot(ltri, da_hi) + _dot(ltri, da_mid) + _dot(ltri, da_lo)
        total = acum[CHUNK - 1:CHUNK, :]
        bmb = bm.astype(BF16)
        cmb = cm.astype(BF16)
        xdt = xs * dt
        acum_t = (jnp.transpose(acum[:, 0:128]), jnp.transpose(acum[:, 128:256]))
        cbs = [_dot_nt(cmb[:, g * 128:(g + 1) * 128], bmb[:, g * 128:(g + 1) * 128]) for g in range(SSM_GROUPS)]
        mhs = []
        for h in range(GROUP_HEADS):
            col = acum[:, h * HEAD_DIM:h * HEAD_DIM + 1]
            rsel = (h % 2) * HEAD_DIM
            row = acum_t[h // 2][rsel:rsel + 1, :]
            lmat = jnp.exp(jnp.where(causal, col - row, NEG))
            mhs.append((cbs[h // 2] * lmat).astype(BF16))
        xdtb = xdt.astype(BF16)
        xdt4 = jnp.concatenate([xdtb * vm_ref[h:h + 1, :] for h in range(GROUP_HEADS)], axis=0)
        y = _dot(cmb, hs.astype(BF16)) * jnp.exp(acum) + _dot(jnp.concatenate(mhs, axis=1), xdt4)
        xw = (xs * (jnp.exp(total - acum) * dt)).astype(BF16)
        hs = jnp.exp(total) * hs + _dot_tn(bmb, xw) * smask_ref[...]
        y = (y + dsk_ref[...] * xs) * _silu(z)
        halves = []
        for g in range(SSM_GROUPS):
            yg = y[:, g * 128:(g + 1) * 128]
            halves.append(yg * lax.rsqrt(jnp.mean(yg * yg, axis=-1, keepdims=True) + EPS))
        o_ref[rr, :] = jnp.concatenate(halves, axis=1) * ng_ref[...]
    hs_ref[...] = hs


def _ret_tile(u_ref, dmat_ref, zeta_ref, xi_ref, gch_ref, bd_ref, rmask_ref, qm_ref, vm_ref,
              o_ref, r_ref):
    r = r_ref[...]
    for c in range(SEQ_TILE // CHUNK):
        rr = pl.ds(c * CHUNK, CHUNK)
        q = u_ref[rr, 0:256]
        k = u_ref[rr, 256:512]
        v = u_ref[rr, 512:768]
        kb = k.astype(BF16)
        qb = q.astype(BF16)
        q4 = jnp.concatenate([qb * qm_ref[h:h + 1, :] for h in range(GROUP_HEADS)], axis=0)
        s4 = _dot_nt(q4, kb) * dmat_ref[...]
        s_cat = jnp.concatenate([s4[h * CHUNK:(h + 1) * CHUNK, :] for h in range(GROUP_HEADS)], axis=1)
        vb = v.astype(BF16)
        v4 = jnp.concatenate([vb * vm_ref[h:h + 1, :] for h in range(GROUP_HEADS)], axis=0)
        y = _dot(qb, r.astype(BF16)) * xi_ref[...] + _dot(s_cat.astype(BF16), v4)
        o_ref[rr, :] = y
        kz = (k * zeta_ref[...]).astype(BF16)
        r = gch_ref[...] * r + _dot_tn(kz, vb) * rmask_ref[...]
    r_ref[...] = r
    for hlf in range(2):
        rr = pl.ds(hlf * (SEQ_TILE // 2), SEQ_TILE // 2)
        y = o_ref[rr, :]
        ss = _dot((y * y).astype(BF16), bd_ref[...])
        o_ref[rr, :] = y * lax.rsqrt(ss * (1.0 / HEAD_DIM) + EPS) * _silu(u_ref[rr, 768:1024])


def _seqmix_kernel(*refs, layer):
    ssd_in, ret_in = list(refs[0:10]), refs[10:19]
    for j in range(2, 7):
        ssd_in[j] = ssd_in[j].at[pl.ds(layer, 1)]
    yb_ref, yc_ref, ext_ref, act_ref, hs_ref, r_ref = refs[19:25]

    @pl.when(pl.program_id(1) == 0)
    def _():
        ext_ref[0:8, :] = jnp.zeros((8, 768), F32)
        hs_ref[...] = jnp.zeros(hs_ref.shape, F32)
        r_ref[...] = jnp.zeros(r_ref.shape, F32)

    _ssd_tile(*ssd_in, yb_ref, ext_ref, act_ref, hs_ref)
    _ret_tile(*ret_in, yc_ref, r_ref)


def _seqmix(ub, cw, cb, dtb, a_exp, dsk, ng, ltri, smask, uc, dmat, zeta, xi, gch, bdn, rmask, qmask, vmask,
            bsz, seq, layer):
    ts = SEQ_TILE
    nt = seq // ts
    const = lambda b, n: (0, 0)
    lay = lambda b, n: (layer, 0, 0)
    tok = lambda b, n: (b * nt + n, 0)
    return pl.pallas_call(
        functools.partial(_seqmix_kernel, layer=layer),
        grid=(bsz, nt),
        in_specs=[
            pl.BlockSpec((ts, 1280), tok),
            pl.BlockSpec((None, 8, 768), lay), pl.BlockSpec((DEPTH, 768), const),
            pl.BlockSpec((DEPTH, 256), const), pl.BlockSpec((DEPTH, 256), const),
            pl.BlockSpec((DEPTH, 256), const), pl.BlockSpec((DEPTH, 256), const),
            pl.BlockSpec((CHUNK, CHUNK), const), pl.BlockSpec((256, 256), const),
            pl.BlockSpec((16, 256), const),
            pl.BlockSpec((ts, 1024), tok),
            pl.BlockSpec((GROUP_HEADS * CHUNK, CHUNK), const),
            pl.BlockSpec((CHUNK, 256), const), pl.BlockSpec((CHUNK, 256), const),
            pl.BlockSpec((1, 256), const), pl.BlockSpec((256, 256), const),
            pl.BlockSpec((256, 256), const),
            pl.BlockSpec((16, 256), const), pl.BlockSpec((16, 256), const),
        ],
        out_specs=[pl.BlockSpec((ts, 256), tok), pl.BlockSpec((ts, 256), tok)],
        out_shape=[jax.ShapeDtypeStruct((bsz * seq, GROUP_WIDTH), F32)] * 2,
        scratch_shapes=[pltpu.VMEM((ts + 8, 768), F32), pltpu.VMEM((ts, 768), F32), pltpu.VMEM((256, 256), F32),
                        pltpu.VMEM((256, 256), F32)],
        compiler_params=pltpu.CompilerParams(
            dimension_semantics=("arbitrary", "arbitrary"), vmem_limit_bytes=56 * 1024 * 1024),
        name="seqmix",
    )(ub, cw, cb, dtb, a_exp, dsk, ng, ltri, smask, vmask, uc, dmat, zeta, xi, gch, bdn, rmask, qmask, vmask)


FF_CHUNK = 1024


def _rms_rows(x, g):
    return x * lax.rsqrt(jnp.mean(x * x, axis=-1, keepdims=True) + EPS) * g


def _ffn_kernel(x_ref, ya_ref, yb_ref, yc_ref, yd_ref, p_ref, wout_ref, gf_ref, wup_ref, wdn_ref,
                gp_ref, wple_ref, wgate_ref, o_ref, hb_ref, *, layer):
    gf_ref, gp_ref = gf_ref.at[pl.ds(layer, 1)], gp_ref.at[pl.ds(layer, 1)]
    tm = x_ref.shape[0]
    halves = [pl.ds(0, tm // 2), pl.ds(tm // 2, tm // 2)]
    x1 = []
    for rr in halves:
        ycat = jnp.concatenate([y_ref[rr, :].astype(BF16) for y_ref in (ya_ref, yb_ref, yc_ref, yd_ref)], axis=1)
        acc = x_ref[rr, :] + _dot(ycat, wout_ref[...])
        hb_ref[rr, :] = _rms_rows(acc, gf_ref[...]).astype(BF16)
        x1.append(acc)

    def down(rr, c):
        up = jnp.maximum(_dot(hb_ref[rr, :], wup_ref[:, c * FF_CHUNK:(c + 1) * FF_CHUNK]), 0.0)
        return _dot((up * up).astype(BF16), wdn_ref[c * FF_CHUNK:(c + 1) * FF_CHUNK, :])

    ffs = [down(rr, 0) for rr in halves]
    for c in range(1, D_FF // FF_CHUNK):
        ffs = [ff + down(rr, c) for ff, rr in zip(ffs, halves)]
    for rr, acc, ff in zip(halves, x1, ffs):
        x2 = acc + ff
        gate = _sigmoid(_dot(_rms_rows(x2, gp_ref[...]).astype(BF16), wgate_ref[...]))
        o_ref[rr, :] = x2 + _dot(p_ref[rr, :].astype(BF16), wple_ref[...]) * gate


def _ffn(x2d, ya, yb, yc, yd, p_all, wout, gf, wup, wdn, gp, wple, wgate, tm, layer):
    t = x2d.shape[0]
    tok = lambda i: (i, 0)
    lay = lambda i: (layer, 0, 0)
    once = dict(pipeline_mode=pl.Buffered(1))
    return pl.pallas_call(
        functools.partial(_ffn_kernel, layer=layer),
        grid=(t // tm,),
        in_specs=[
            pl.BlockSpec((tm, D_MODEL), tok),
            pl.BlockSpec((tm, 256), tok), pl.BlockSpec((tm, 256), tok),
            pl.BlockSpec((tm, 256), tok), pl.BlockSpec((tm, 256), tok),
            pl.BlockSpec((None, tm, PLE_DIM), lambda i: (layer, i, 0)),
            pl.BlockSpec((None, D_MODEL, D_MODEL), lay, **once),
            pl.BlockSpec((DEPTH, D_MODEL), lambda i: (0, 0)),
            pl.BlockSpec((None, D_MODEL, D_FF), lay, **once),
            pl.BlockSpec((None, D_FF, D_MODEL), lay, **once),
            pl.BlockSpec((DEPTH, D_MODEL), lambda i: (0, 0)),
            pl.BlockSpec((None, PLE_DIM, D_MODEL), lay, **once),
            pl.BlockSpec((None, D_MODEL, D_MODEL), lay, **once),
        ],
        out_specs=pl.BlockSpec((tm, D_MODEL), tok),
        out_shape=jax.ShapeDtypeStruct((t, D_MODEL), F32),
        scratch_shapes=[pltpu.VMEM((tm, D_MODEL), BF16)],
        compiler_params=pltpu.CompilerParams(
            dimension_semantics=("arbitrary",), vmem_limit_bytes=56 * 1024 * 1024),
        name="ffn",
    )(x2d, ya, yb, yc, yd, p_all, wout, gf, wup, wdn, gp, wple, wgate)


def _head_mask(head_of_lane):
    m = np.zeros((8, GROUP_WIDTH), np.float32)
    for h in range(GROUP_HEADS):
        m[h] = head_of_lane == h
    return m


def _tables(seq):
    pos = jnp.arange(seq, dtype=F32)
    ang_a = ROPE_THETA ** (-jnp.arange(0, HEAD_DIM, 2, dtype=F32) / HEAD_DIM)
    ang_c = 1.0 / (10000.0 ** jnp.linspace(0.0, 1.0, HEAD_DIM // 2, dtype=F32))

    def cs(ang):
        a = pos[:, None] * ang[None, :]
        return jnp.tile(jnp.cos(a), (1, 4)), jnp.tile(jnp.sin(a), (1, 4))

    cosa, sina = cs(ang_a)
    cosc, sinc = cs(ang_c)
    log_g = jnp.log(1.0 - 2.0 ** (-5.0 - jnp.arange(GROUP_HEADS, dtype=F32)))
    idx = jnp.arange(CHUNK, dtype=F32)
    dist = idx[:, None] - idx[None, :]
    dmat = jnp.where((dist >= 0)[None], jnp.exp(jnp.maximum(dist, 0.0)[None] * log_g[:, None, None]), 0.0)
    zeta = jnp.exp((CHUNK - 1 - idx)[:, None] * log_g)[:, _HEAD_PERM]
    xi = jnp.exp((idx + 1.0)[:, None] * log_g)[:, _HEAD_NAT]
    gch = jnp.exp(CHUNK * log_g)[_HEAD_NAT][None, :]
    return dict(
        cosa=cosa, sina=sina, cosc=cosc, sinc=sinc, dmat=dmat, zeta=zeta, xi=xi, gch=gch,
        bd_perm=jnp.asarray(_HEAD_PERM[:, None] == _HEAD_PERM[None, :], BF16),
        bd_nat=jnp.asarray(_HEAD_NAT[:, None] == _HEAD_NAT[None, :], BF16),
        rmask=jnp.asarray(_HEAD_PERM[:, None] == _HEAD_NAT[None, :], F32),
        smask=jnp.asarray((_LANE[:, None] // SSM_STATE) == (_HEAD_NAT[None, :] // (GROUP_HEADS // SSM_GROUPS)), F32),
        qmask_bf=jnp.asarray(np.concatenate([_head_mask(_HEAD_PERM)] * 2), BF16),
        vmask_bf=jnp.asarray(np.concatenate([_head_mask(_HEAD_NAT)] * 2), BF16),
        ltri=jnp.asarray(np.tril(np.ones((CHUNK, CHUNK), np.float32)), BF16),
    )


def _build_w_cat_t(w_in):
    wt = jnp.transpose(w_in, (0, 2, 1))

    depth = wt.shape[0]

    def perm(c0):
        blk = wt[:, c0:c0 + 256].reshape(depth, GROUP_HEADS, 2, 32, D_MODEL)
        return blk.transpose(0, 2, 1, 3, 4).reshape(depth, 256, D_MODEL)

    dt = jnp.repeat(wt[:, _BDT:_BDT + GROUP_HEADS], HEAD_DIM, axis=1)
    parts = [perm(_A0), perm(_A0 + 256), wt[:, _A0 + 512:_BDT], dt,
             perm(_C0), perm(_C0 + 256), wt[:, _C0 + 512:IN_COLS]]
    return jnp.concatenate(parts, axis=1).astype(BF16)


def kernel(x, p, norm_mix, w_in, attn_q_norm, attn_k_norm, ssm_conv_w, ssm_conv_b, ssm_dt_bias,
           ssm_a_log, ssm_d, ssm_norm, conv_dw_w, conv_dw_b, conv_ln_g, conv_ln_b, w_out,
           norm_ffn, w_up, w_down, norm_ple, w_ple, w_ple_gate):
    bsz, seq, _ = x.shape
    depth = w_in.shape[0]
    tm = 512
    tb = _tables(seq)
    w_cat = _build_w_cat_t(w_in)
    g_mix, g_ffn, g_ple = norm_mix, norm_ffn, norm_ple
    gq, gk = attn_q_norm[:, _PERM % HEAD_DIM], attn_k_norm[:, _PERM % HEAD_DIM]
    cw = jnp.pad(ssm_conv_w, ((0, 0), (0, 8 - SSM_CONV), (0, 0)))
    dw = jnp.pad(conv_dw_w, ((0, 0), (0, 32 - CONF_KERNEL), (0, 0)))
    dtb, alog, dsk = ssm_dt_bias[:, _HEAD_NAT], ssm_a_log[:, _HEAD_NAT], ssm_d[:, _HEAD_NAT]
    wout, wup, wdn = w_out.astype(BF16), w_up.astype(BF16), w_down.astype(BF16)
    wple, wgate = w_ple.astype(BF16), w_ple_gate.astype(BF16)
    p_all = p.reshape(depth, bsz * seq, PLE_DIM)
    x2d = x.reshape(bsz * seq, D_MODEL)
    for i in range(depth):
        oa, ob, oc, yd = _inproj(x2d, g_mix, w_cat, tb["cosa"], tb["sina"], tb["cosc"], tb["sinc"],
                                 gq, gk, tb["bd_perm"], dw, conv_dw_b, conv_ln_g, conv_ln_b,
                                 seq, tm, i)
        ya = _attention(oa, tb["qmask_bf"], bsz, seq)
        yb, yc = _seqmix(ob, cw, ssm_conv_b, dtb, alog, dsk, ssm_norm, tb["ltri"], tb["smask"],
                         oc, tb["dmat"].reshape(GROUP_HEADS * CHUNK, CHUNK), tb["zeta"], tb["xi"], tb["gch"],
                         tb["bd_nat"], tb["rmask"], tb["qmask_bf"], tb["vmask_bf"], bsz, seq, i)
        x2d = _ffn(x2d, ya, yb, yc, yd, p_all, wout, g_ffn, wup, wdn, g_ple, wple, wgate, tm, i)
    return x2d.reshape(bsz, seq, D_MODEL)
```
